```python
import math
import jax
import jax.numpy as jnp
from jax import lax
import numpy as np

D_MODEL = 2048
BATCH = 2
SEQ = 4096
DEPTH = 4

D_MIX = D_MODEL
GLA_HEADS = 4
GLA_DV = 3 * D_MIX // 8
GLA_DK = GLA_DV // 2
GLA_HD_K = GLA_DK // GLA_HEADS
GLA_HD_V = GLA_DV // GLA_HEADS
GLA_RANK = 16
GLA_TAU = 16.0
GLA_CHUNK = 64
DIL_HEADS = 6
DIL_HD = 128
DIL_DIM = DIL_HEADS * DIL_HD
DIL_BRANCHES = ((128, 1), (512, 4), (2048, 16))
DIL_QBLOCK = 128
CONV_C = D_MIX - GLA_DV - DIL_DIM
CONV_K = 31
D_FF = 5504
IN_SIZES = (GLA_DK, GLA_DK, GLA_DV, GLA_RANK, GLA_RANK, GLA_DV,
            DIL_DIM, DIL_DIM, DIL_DIM, 2 * CONV_C)
N_IN = sum(IN_SIZES)
DEEPNORM_ALPHA = (2.0 * DEPTH) ** 0.25
DEEPNORM_BETA = (8.0 * DEPTH) ** -0.25
LN_EPS = 1e-5

kernel_name = 'hymba_style_bidir_gla_dilated_conformer_deepnorm'


def layer_norm(x, g, b):
    xf = x.astype(jnp.float32)
    mu = jnp.mean(xf, axis=-1, keepdims=True)
    var = jnp.mean(jnp.square(xf - mu), axis=-1, keepdims=True)
    return ((xf - mu) * lax.rsqrt(var + LN_EPS) * g + b).astype(x.dtype)


def swiglu(h, w_gate, w_up, w_down):
    return (jax.nn.silu(h @ w_gate) * (h @ w_up)) @ w_down


def gla_scan(q, k, v, log_a):
    B, T, H, K = q.shape
    V = v.shape[-1]
    C = GLA_CHUNK
    N = T // C
    qc = q.astype(jnp.float32).reshape(B, N, C, H, K)
    kc = k.astype(jnp.float32).reshape(B, N, C, H, K)
    vc = v.astype(jnp.float32).reshape(B, N, C, H, V)
    b = jnp.cumsum(log_a.reshape(B, N, C, H, K), axis=2)
    b_end = b[:, :, -1]
    q_dec = qc * jnp.exp(b)
    k_inv = kc * jnp.exp(-b)
    k_end = kc * jnp.exp(b_end[:, :, None] - b)
    causal = jnp.tril(jnp.ones((C, C), dtype=bool))
    att = jnp.einsum('bnchk,bnshk->bnhcs', q_dec, k_inv)
    att = jnp.where(causal, att, 0.0)
    o_intra = jnp.einsum('bnhcs,bnshv->bnchv', att, vc)
    u = jnp.einsum('bnshk,bnshv->bnhkv', k_end, vc)

    def step(S, xs):
        dec, un = xs
        return dec[..., None] * S + un, S

    S0 = jnp.zeros((B, H, K, V), jnp.float32)
    _, S_prev = lax.scan(step, S0, (jnp.exp(b_end).transpose(1, 0, 2, 3),
                                    u.transpose(1, 0, 2, 3, 4)))
    S_prev = S_prev.transpose(1, 0, 2, 3, 4)
    o_inter = jnp.einsum('bnchk,bnhkv->bnchv', q_dec, S_prev)
    return (o_intra + o_inter).reshape(B, T, H, V)


def dilated_attention(q, k, v):
    B, T, H, E = q.shape
    offs = np.stack([d * np.arange(-((w // 2) // d), (w // 2) // d + 1)
                     for (w, d) in DIL_BRANCHES])
    slopes = 2.0 ** (-8.0 * np.arange(1, H + 1) / H)
    alibi = jnp.asarray(-slopes[:, None, None] * np.abs(offs)[None], jnp.float32)
    offs_j = jnp.asarray(offs, jnp.int32)
    scale = E ** -0.5

    def block(i):
        t0 = i * DIL_QBLOCK
        qb = lax.dynamic_slice_in_dim(q, t0, DIL_QBLOCK, axis=1)
        pos = t0 + jnp.arange(DIL_QBLOCK)[:, None, None] + offs_j[None]
        valid = (pos >= 0) & (pos < T)
        pc = jnp.clip(pos, 0, T - 1)
        kg = jnp.take(k, pc, axis=1)
        vg = jnp.take(v, pc, axis=1)
        s = jnp.einsum('bqhe,bqnjhe->bhqnj', qb, kg,
                       preferred_element_type=jnp.float32) * scale + alibi[:, None]
        s = jnp.where(valid[None, None], s, -1e30)
        m = jnp.max(s, axis=-1, keepdims=True)
        p = jnp.exp(s - m)
        z = jnp.sum(p, axis=-1, keepdims=True)
        o = jnp.einsum('bhqnj,bqnjhe->bqnhe', p / z, vg.astype(jnp.float32))
        lse = (m + jnp.log(z))[..., 0]
        wts = jax.nn.softmax(lse, axis=-1)
        return jnp.einsum('bhqn,bqnhe->bqhe', wts, o)

    out = lax.map(block, jnp.arange(T // DIL_QBLOCK))
    return out.transpose(1, 0, 2, 3, 4).reshape(B, T, H, E)


def hybrid_mixer(h, w_in, dec_w_f, dec_b_f, dec_w_b, dec_b_b, gla_norm_g,
                 conv_w, conv_b, conv_ln_g, conv_ln_b, w_out):
    B, T, _ = h.shape
    proj = h @ w_in
    idx = [int(c) for c in np.cumsum(IN_SIZES)[:-1]]
    gq, gk, gv, rf, rb, gg, dq, dk, dv, cv = jnp.split(proj, idx, axis=-1)

    q = gq.reshape(B, T, GLA_HEADS, GLA_HD_K) * (GLA_HD_K ** -0.5)
    k = gk.reshape(B, T, GLA_HEADS, GLA_HD_K)
    v = gv.reshape(B, T, GLA_HEADS, GLA_HD_V)
    la_f = (jax.nn.log_sigmoid((rf @ dec_w_f + dec_b_f).astype(jnp.float32)) / GLA_TAU
            ).reshape(B, T, GLA_HEADS, GLA_HD_K)
    la_b = (jax.nn.log_sigmoid((rb @ dec_w_b + dec_b_b).astype(jnp.float32)) / GLA_TAU
            ).reshape(B, T, GLA_HEADS, GLA_HD_K)
    o_f = gla_scan(q, k, v, la_f)
    o_b = jnp.flip(gla_scan(jnp.flip(q, 1), jnp.flip(k, 1), jnp.flip(v, 1),
                            jnp.flip(la_b, 1)), 1)
    o = o_f + o_b
    o = o * lax.rsqrt(jnp.mean(jnp.square(o), axis=-1, keepdims=True) + LN_EPS)
    o = o.reshape(B, T, GLA_DV) * gla_norm_g
    gla_out = (o * jax.nn.silu(gg.astype(jnp.float32))).astype(h.dtype)

    dil_out = dilated_attention(dq.reshape(B, T, DIL_HEADS, DIL_HD),
                                dk.reshape(B, T, DIL_HEADS, DIL_HD),
                                dv.reshape(B, T, DIL_HEADS, DIL_HD))
    dil_out = dil_out.reshape(B, T, DIL_DIM).astype(h.dtype)

    c_val, c_gate = jnp.split(cv, 2, axis=-1)
    u = c_val * jax.nn.sigmoid(c_gate)
    y = lax.conv_general_dilated(u, conv_w[:, None, :], window_strides=(1,),
                                 padding=[(CONV_K // 2, CONV_K // 2)],
                                 dimension_numbers=('NWC', 'WIO', 'NWC'),
                                 feature_group_count=CONV_C) + conv_b
    conv_out = jax.nn.silu(layer_norm(y, conv_ln_g, conv_ln_b))

    cat = jnp.concatenate([gla_out, dil_out, conv_out], axis=-1)
    return cat @ w_out


def setup_inputs(seed: int = 0) -> dict:
    key = jax.random.key(seed)
    ks = jax.random.split(key, 32)
    L, D, F = DEPTH, D_MODEL, D_FF

    def nrm(k, shape, scale):
        return jax.random.normal(k, shape, jnp.float32) * scale

    return {
        'x': nrm(ks[0], (BATCH, SEQ, D), 1.0),
        'ffn1_w_gate': nrm(ks[1], (L, D, F), D ** -0.5),
        'ffn1_w_up': nrm(ks[2], (L, D, F), D ** -0.5),
        'ffn1_w_down': nrm(ks[3], (L, F, D), DEEPNORM_BETA * F ** -0.5),
        'ln1_g': 1.0 + nrm(ks[4], (L, D), 0.02),
        'ln1_b': nrm(ks[5], (L, D), 0.02),
        'w_in': nrm(ks[6], (L, D, N_IN), D ** -0.5),
        'gla_decay_w_fwd': nrm(ks[7], (L, GLA_RANK, GLA_DK), GLA_RANK ** -0.5),
        'gla_decay_b_fwd': nrm(ks[8], (L, GLA_DK), 0.1),
        'gla_decay_w_bwd': nrm(ks[9], (L, GLA_RANK, GLA_DK), GLA_RANK ** -0.5),
        'gla_decay_b_bwd': nrm(ks[10], (L, GLA_DK), 0.1),
        'gla_norm_g': 1.0 + nrm(ks[11], (L, GLA_DV), 0.02),
        'conv_w': nrm(ks[12], (L, CONV_K, CONV_C), CONV_K ** -0.5),
        'conv_b': nrm(ks[13], (L, CONV_C), 0.02),
        'conv_ln_g': 1.0 + nrm(ks[14], (L, CONV_C), 0.02),
        'conv_ln_b': nrm(ks[15], (L, CONV_C), 0.02),
        'w_out': nrm(ks[16], (L, D_MIX, D), DEEPNORM_BETA * D_MIX ** -0.5),
        'ln2_g': 1.0 + nrm(ks[17], (L, D), 0.02),
        'ln2_b': nrm(ks[18], (L, D), 0.02),
        'ffn2_w_gate': nrm(ks[19], (L, D, F), D ** -0.5),
        'ffn2_w_up': nrm(ks[20], (L, D, F), D ** -0.5),
        'ffn2_w_down': nrm(ks[21], (L, F, D), DEEPNORM_BETA * F ** -0.5),
        'ln3_g': 1.0 + nrm(ks[22], (L, D), 0.02),
        'ln3_b': nrm(ks[23], (L, D), 0.02),
    }


def reference(x, ffn1_w_gate, ffn1_w_up, ffn1_w_down, ln1_g, ln1_b, w_in,
              gla_decay_w_fwd, gla_decay_b_fwd, gla_decay_w_bwd, gla_decay_b_bwd,
              gla_norm_g, conv_w, conv_b, conv_ln_g, conv_ln_b, w_out, ln2_g, ln2_b,
              ffn2_w_gate, ffn2_w_up, ffn2_w_down, ln3_g, ln3_b):
    for l in range(DEPTH):
        x = layer_norm(DEEPNORM_ALPHA * x
                       + 0.5 * swiglu(x, ffn1_w_gate[l], ffn1_w_up[l], ffn1_w_down[l]),
                       ln1_g[l], ln1_b[l])
        mix = hybrid_mixer(x, w_in[l], gla_decay_w_fwd[l], gla_decay_b_fwd[l],
                           gla_decay_w_bwd[l], gla_decay_b_bwd[l], gla_norm_g[l],
                           conv_w[l], conv_b[l], conv_ln_g[l], conv_ln_b[l], w_out[l])
        x = layer_norm(DEEPNORM_ALPHA * x + mix, ln2_g[l], ln2_b[l])
        x = layer_norm(DEEPNORM_ALPHA * x
                       + 0.5 * swiglu(x, ffn2_w_gate[l], ffn2_w_up[l], ffn2_w_down[l]),
                       ln3_g[l], ln3_b[l])
    return x
```

```python
import functools

import numpy as np
import jax
import jax.numpy as jnp
from jax import lax
from jax.experimental import pallas as pl
from jax.experimental.pallas import tpu as pltpu

GLA_HEADS = 4
GLA_TAU = 16.0
GLA_CHUNK = 64
DIL_HEADS = 6
DIL_BRANCHES = ((128, 1), (512, 4), (2048, 16))
LN_EPS = 1e-5
MASK_VALUE = -1e30

LANES = 128
MXU_WIDTH = 256
VMEM_LIMIT_BYTES = 56 * 1024 * 1024

F32 = jnp.float32
BF16 = jnp.bfloat16
HIGHEST = lax.Precision.HIGHEST
_NT_DIMS = (((1,), (1,)), ((), ()))
_TN_DIMS = (((0,), (0,)), ((), ()))


def _round_up(n, m):
    return (n + m - 1) // m * m


def _layer_norm(y, g, b):
    mu = jnp.mean(y, axis=-1, keepdims=True)
    yc = y - mu
    var = jnp.mean(yc * yc, axis=-1, keepdims=True)
    return yc * lax.rsqrt(var + LN_EPS) * g + b


def _params(n_grid_axes):
    return pltpu.CompilerParams(dimension_semantics=("arbitrary",) * n_grid_axes,
                                vmem_limit_bytes=VMEM_LIMIT_BYTES)


def _ffn_ln_kernel(x_ref, wg_ref, wu_ref, wd_ref, g_ref, b_ref, o_ref, xb_ref, *, alpha):
    f = pl.program_id(1)

    @pl.when(f == 0)
    def _():
        xb_ref[...] = x_ref[...].astype(BF16)
        o_ref[...] = jnp.zeros_like(o_ref)

    xb = xb_ref[...]
    gate = jnp.dot(xb, wg_ref[...], preferred_element_type=F32)
    up = jnp.dot(xb, wu_ref[...], preferred_element_type=F32)
    h = (gate * jax.nn.sigmoid(gate) * up).astype(BF16)
    o_ref[...] += jnp.dot(h, wd_ref[...], preferred_element_type=F32)

    @pl.when(f == pl.num_programs(1) - 1)
    def _():
        y = alpha * x_ref[...] + 0.5 * o_ref[...]
        o_ref[...] = _layer_norm(y, g_ref[...], b_ref[...])


def _ffn_ln(x, wg, wu, wd, g, b, *, alpha, tm, tf):
    m, d = x.shape
    fp = wg.shape[1]
    return pl.pallas_call(
        functools.partial(_ffn_ln_kernel, alpha=alpha),
        grid=(m // tm, fp // tf),
        in_specs=[
            pl.BlockSpec((tm, d), lambda i, f: (i, 0)),
            pl.BlockSpec((d, tf), lambda i, f: (0, f)),
            pl.BlockSpec((d, tf), lambda i, f: (0, f)),
            pl.BlockSpec((tf, d), lambda i, f: (f, 0)),
            pl.BlockSpec((1, d), lambda i, f: (0, 0)),
            pl.BlockSpec((1, d), lambda i, f: (0, 0)),
        ],
        out_specs=pl.BlockSpec((tm, d), lambda i, f: (i, 0)),
        out_shape=jax.ShapeDtypeStruct((m, d), F32),
        scratch_shapes=[pltpu.VMEM((tm, d), BF16)],
        compiler_params=_params(2),
        name="ffn_ln",
    )(x, wg, wu, wd, g, b)


def _in_proj_kernel(x_ref, w_ref, o_ref, xb_ref):
    @pl.when(pl.program_id(1) == 0)
    def _():
        xb_ref[...] = x_ref[...].astype(BF16)

    o_ref[...] = jnp.dot(xb_ref[...], w_ref[...], preferred_element_type=F32)


def _in_proj(x, w, *, tm, tn):
    m, d = x.shape
    n = w.shape[1]
    return pl.pallas_call(
        _in_proj_kernel,
        grid=(m // tm, n // tn),
        in_specs=[
            pl.BlockSpec((tm, d), lambda i, j: (i, 0)),
            pl.BlockSpec((d, tn), lambda i, j: (0, j)),
        ],
        out_specs=pl.BlockSpec((tm, tn), lambda i, j: (i, j)),
        out_shape=jax.ShapeDtypeStruct((m, n), F32),
        scratch_shapes=[pltpu.VMEM((tm, d), BF16)],
        compiler_params=_params(2),
        name="in_proj",
    )(x, w)


def _log_sigmoid(z):
    return jnp.minimum(z, 0.0) - jnp.log1p(jnp.exp(-jnp.abs(z)))


def _gla_kernel(q_ref, k_ref, v_ref, gg_ref, r_ref, wf_ref, wb_ref, bf_ref, bb_ref, ng_ref,
                o_ref, oacc_ref, st_ref, *, seq, tile, chunk, q_scale, inv_dv):
    n_tiles = seq // tile
    n_chunks = tile // chunk
    shift = chunk.bit_length() - 1
    row = lax.broadcasted_iota(jnp.int32, (tile, tile), 0)
    col = lax.broadcasted_iota(jnp.int32, (tile, tile), 1)
    same_chunk = lax.shift_right_logical(row, shift) == lax.shift_right_logical(col, shift)

    def run(forward):
        tri = same_chunk & ((col <= row) if forward else (col >= row))
        tri_f = tri.astype(F32)
        w = wf_ref[0] if forward else wb_ref[0]
        bias = bf_ref[0] if forward else bb_ref[0]
        st_ref[...] = jnp.zeros_like(st_ref)

        def body(i, carry):
            ti = i if forward else n_tiles - 1 - i
            rows = pl.ds(pl.multiple_of(ti * tile, tile), tile)
            z = jnp.dot(r_ref[rows, :], w, precision=HIGHEST, preferred_element_type=F32) + bias
            la = _log_sigmoid(z) * (1.0 / GLA_TAU)
            bcum = jnp.dot(tri_f, la, precision=HIGHEST, preferred_element_type=F32)
            q = q_ref[rows, :] * q_scale
            k = k_ref[rows, :]
            vb = v_ref[rows, :].astype(BF16)
            q_dec = (q * jnp.exp(bcum)).astype(BF16)
            k_inv = (k * jnp.exp(-bcum)).astype(BF16)
            att = lax.dot_general(q_dec, k_inv, _NT_DIMS, preferred_element_type=F32)
            att = jnp.where(tri, att, 0.0).astype(BF16)
            o_intra = jnp.dot(att, vb, preferred_element_type=F32)

            o_inter = [None] * n_chunks
            for c in (range(n_chunks) if forward else reversed(range(n_chunks))):
                lo = c * chunk
                end_row = lo + chunk - 1 if forward else lo
                b_end = bcum[end_row:end_row + 1, :]
                st = st_ref[...]
                o_inter[c] = lax.dot_general(q_dec[lo:lo + chunk], st.astype(BF16), _NT_DIMS,
                                             preferred_element_type=F32)
                k_end = (k[lo:lo + chunk] * jnp.exp(b_end - bcum[lo:lo + chunk])).astype(BF16)
                upd = lax.dot_general(vb[lo:lo + chunk], k_end, _TN_DIMS, preferred_element_type=F32)
                st_ref[...] = st * jnp.exp(b_end) + upd
            o = o_intra + jnp.concatenate(o_inter, axis=0)

            if forward:
                oacc_ref[rows, :] = o
            else:
                o = oacc_ref[rows, :] + o
                ms = jnp.sum(o * o, axis=-1, keepdims=True) * inv_dv
                gate = gg_ref[rows, :]
                o = o * lax.rsqrt(ms + LN_EPS) * ng_ref[0] * (gate * jax.nn.sigmoid(gate))
                o_ref[rows, :] = o.astype(o_ref.dtype)
            return carry

        lax.fori_loop(0, n_tiles, body, 0)

    run(True)
    run(False)


def _gla(proj, wdec_f, wdec_b, bdec_f, bdec_b, norm_g, *, batch, seq, offs, hk, hv, q_scale, dv_head):
    heads = wdec_f.shape[0]
    tile = min(256, seq)
    kern = functools.partial(_gla_kernel, seq=seq, tile=tile, chunk=GLA_CHUNK, q_scale=q_scale,
                             inv_dv=1.0 / dv_head)
    cq, ck, cv, cg, cr = (offs["gq"] // hk, offs["gk"] // hk, offs["gv"] // hv, offs["gg"] // hv,
                          offs["r"] // LANES)
    return pl.pallas_call(
        kern,
        grid=(batch, heads),
        in_specs=[
            pl.BlockSpec((seq, hk), lambda b, h: (b, cq + h)),
            pl.BlockSpec((seq, hk), lambda b, h: (b, ck + h)),
            pl.BlockSpec((seq, hv), lambda b, h: (b, cv + h)),
            pl.BlockSpec((seq, hv), lambda b, h: (b, cg + h)),
            pl.BlockSpec((seq, LANES), lambda b, h: (b, cr)),
            pl.BlockSpec((1, LANES, hk), lambda b, h: (h, 0, 0)),
            pl.BlockSpec((1, LANES, hk), lambda b, h: (h, 0, 0)),
            pl.BlockSpec((1, 1, hk), lambda b, h: (h, 0, 0)),
            pl.BlockSpec((1, 1, hk), lambda b, h: (h, 0, 0)),
            pl.BlockSpec((1, 1, hv), lambda b, h: (h, 0, 0)),
        ],
        out_specs=pl.BlockSpec((seq, hv), lambda b, h: (b, h)),
        out_shape=jax.ShapeDtypeStruct((batch * seq, heads * hv), BF16),
        scratch_shapes=[pltpu.VMEM((seq, hv), F32), pltpu.VMEM((hv, hk), F32)],
        compiler_params=_params(2),
        name="gla",
    )(proj, proj, proj, proj, proj, wdec_f, wdec_b, bdec_f, bdec_b, norm_g)


def _dil_kernel(slopes_ref, q_ref, k_ref, v_ref, o_ref, m_ref, l_ref, acc_ref, *, seq, scale):
    slope = slopes_ref[pl.program_id(1)]
    hd = q_ref.shape[-1]

    for bi, (w, d) in enumerate(DIL_BRANCHES):
        first = bi == 0
        reach = (w // 2) // d
        length = seq // d
        qb = min(128, length)
        kb = min(length, qb + 2 * reach)
        n_qb = length // qb
        bias_step = slope * float(d)

        def block(r, j, d=d, first=first, reach=reach, length=length, qb=qb, kb=kb, bias_step=bias_step):
            i0 = j * qb
            ks = jnp.clip(i0 - reach, 0, length - kb)
            if d == 1:
                qrows = pl.ds(pl.multiple_of(i0, qb), qb)
                krows = pl.ds(pl.multiple_of(ks, 8), kb)
            else:
                qrows = pl.ds(r + d * i0, qb, stride=d)
                krows = pl.ds(r + d * ks, kb, stride=d)
            q = (q_ref[qrows, :] * scale).astype(BF16)
            k = k_ref[krows, :].astype(BF16)
            v = v_ref[krows, :].astype(BF16)
            s = lax.dot_general(q, k, _NT_DIMS, preferred_element_type=F32)
            qi = i0 + lax.broadcasted_iota(jnp.int32, (qb, kb), 0)
            ki = ks + lax.broadcasted_iota(jnp.int32, (qb, kb), 1)
            dist = jnp.abs(ki - qi)
            s = s - bias_step * dist.astype(F32)
            s = jnp.where(dist <= reach, s, MASK_VALUE)
            m_blk = jnp.max(s, axis=-1, keepdims=True)
            if first:
                m_new = m_blk
            else:
                m_prev = m_ref[qrows, :][:, :1]
                m_new = jnp.maximum(m_prev, m_blk)
            p = jnp.exp(s - m_new)
            l_new = jnp.sum(p, axis=-1, keepdims=True)
            acc_new = jnp.dot(p.astype(BF16), v, preferred_element_type=F32)
            if not first:
                a = jnp.exp(m_prev - m_new)
                l_new = a * l_ref[qrows, :][:, :1] + l_new
                acc_new = a * acc_ref[qrows, :] + acc_new
            m_ref[qrows, :] = jnp.broadcast_to(m_new, (qb, hd))
            l_ref[qrows, :] = jnp.broadcast_to(l_new, (qb, hd))
            acc_ref[qrows, :] = acc_new

        def residue(r, carry, block=block, n_qb=n_qb):
            def per_block(j, c):
                block(r, j)
                return c
            return lax.fori_loop(0, n_qb, per_block, carry)

        lax.fori_loop(0, d, residue, 0)

    rt = min(256, seq)

    def finish(i, carry):
        rows = pl.ds(pl.multiple_of(i * rt, rt), rt)
        o_ref[rows, :] = (acc_ref[rows, :] / l_ref[rows, :]).astype(o_ref.dtype)
        return carry

    lax.fori_loop(0, seq // rt, finish, 0)


def _dil(proj, slopes, *, batch, seq, offs, hd):
    cq, ck, cv = offs["dq"] // hd, offs["dk"] // hd, offs["dv"] // hd
    kern = functools.partial(_dil_kernel, seq=seq, scale=hd ** -0.5)
    return pl.pallas_call(
        kern,
        grid=(batch, DIL_HEADS),
        in_specs=[
            pl.BlockSpec(memory_space=pltpu.SMEM),
            pl.BlockSpec((seq, hd), lambda b, h: (b, cq + h)),
            pl.BlockSpec((seq, hd), lambda b, h: (b, ck + h)),
            pl.BlockSpec((seq, hd), lambda b, h: (b, cv + h)),
        ],
        out_specs=pl.BlockSpec((seq, hd), lambda b, h: (b, h)),
        out_shape=jax.ShapeDtypeStruct((batch * seq, DIL_HEADS * hd), BF16),
        scratch_shapes=[pltpu.VMEM((seq, hd), F32)] * 3,
        compiler_params=_params(2),
        name="dil",
    )(slopes, proj, proj, proj)


def _conv_kernel(prev_ref, cur_ref, next_ref, w_ref, cb_ref, g_ref, b_ref, o_ref, u_ref, *, halo, taps):
    i = pl.program_id(1)
    c = o_ref.shape[-1]
    tq = o_ref.shape[0]

    def glu(ref):
        x = ref[...]
        return x[:, :c] * jax.nn.sigmoid(x[:, c:])

    u_ref[0:halo, :] = jnp.where(i > 0, glu(prev_ref), 0.0)
    u_ref[halo:halo + tq, :] = glu(cur_ref)
    u_ref[halo + tq:, :] = jnp.where(i < pl.num_programs(1) - 1, glu(next_ref), 0.0)

    pad = taps // 2
    y = jnp.zeros((tq, c), F32) + cb_ref[...]
    for t in range(taps):
        y = y + u_ref[pl.ds(halo - pad + t, tq), :] * w_ref[t:t + 1, :]
    y = _layer_norm(y, g_ref[...], b_ref[...])
    o_ref[...] = (y * jax.nn.sigmoid(y)).astype(o_ref.dtype)


def _conv(proj, conv_w, conv_b, ln_g, ln_b, *, batch, seq, offs):
    taps, c = conv_w.shape
    halo = 16
    assert taps // 2 <= halo
    tq = min(512, seq)
    nt = seq // tq
    hb = tq // halo
    ccol = offs["cv"] // (2 * c)
    kern = functools.partial(_conv_kernel, halo=halo, taps=taps)
    return pl.pallas_call(
        kern,
        grid=(batch, nt),
        in_specs=[
            pl.BlockSpec((halo, 2 * c), lambda b, i: (jnp.maximum((b * nt + i) * hb - 1, 0), ccol)),
            pl.BlockSpec((tq, 2 * c), lambda b, i: (b * nt + i, ccol)),
            pl.BlockSpec((halo, 2 * c),
                         lambda b, i: (jnp.minimum((b * nt + i + 1) * hb, batch * nt * hb - 1), ccol)),
            pl.BlockSpec((taps, c), lambda b, i: (0, 0)),
            pl.BlockSpec((1, c), lambda b, i: (0, 0)),
            pl.BlockSpec((1, c), lambda b, i: (0, 0)),
            pl.BlockSpec((1, c), lambda b, i: (0, 0)),
        ],
        out_specs=pl.BlockSpec((tq, c), lambda b, i: (b * nt + i, 0)),
        out_shape=jax.ShapeDtypeStruct((batch * seq, c), BF16),
        scratch_shapes=[pltpu.VMEM((tq + 2 * halo, c), F32)],
        compiler_params=_params(2),
        name="conv",
    )(proj, proj, proj, conv_w, conv_b, ln_g, ln_b)


def _out_ln_kernel(x_ref, a_ref, d_ref, c_ref, wa_ref, wd_ref, wc_ref, g_ref, b_ref, o_ref, *, alpha):
    mix = jnp.dot(a_ref[...], wa_ref[...], preferred_element_type=F32)
    mix += jnp.dot(d_ref[...], wd_ref[...], preferred_element_type=F32)
    mix += jnp.dot(c_ref[...], wc_ref[...], preferred_element_type=F32)
    o_ref[...] = _layer_norm(alpha * x_ref[...] + mix, g_ref[...], b_ref[...])


def _out_ln(x, gla_o, dil_o, conv_o, wa, wd, wc, g, b, *, alpha, tm):
    m, d = x.shape
    row = lambda i: (i, 0)
    fixed = lambda i: (0, 0)
    return pl.pallas_call(
        functools.partial(_out_ln_kernel, alpha=alpha),
        grid=(m // tm,),
        in_specs=[
            pl.BlockSpec((tm, d), row),
            pl.BlockSpec((tm, gla_o.shape[1]), row),
            pl.BlockSpec((tm, dil_o.shape[1]), row),
            pl.BlockSpec((tm, conv_o.shape[1]), row),
            pl.BlockSpec(wa.shape, fixed),
            pl.BlockSpec(wd.shape, fixed),
            pl.BlockSpec(wc.shape, fixed),
            pl.BlockSpec((1, d), fixed),
            pl.BlockSpec((1, d), fixed),
        ],
        out_specs=pl.BlockSpec((tm, d), row),
        out_shape=jax.ShapeDtypeStruct((m, d), F32),
        compiler_params=_params(1),
        name="out_ln",
    )(x, gla_o, dil_o, conv_o, wa, wd, wc, g, b)


def _pad_heads(w, heads, width):
    lead = w.shape[:-1]
    hd = w.shape[-1] // heads
    w = w.reshape(lead + (heads, hd))
    w = jnp.pad(w, [(0, 0)] * len(lead) + [(0, 0), (0, width - hd)])
    return w.reshape(lead + (heads * width,))


def kernel(x, ffn1_w_gate, ffn1_w_up, ffn1_w_down, ln1_g, ln1_b, w_in, gla_decay_w_fwd, gla_decay_b_fwd, gla_decay_w_bwd, gla_decay_b_bwd, gla_norm_g, conv_w, conv_b, conv_ln_g, conv_ln_b, w_out, ln2_g, ln2_b, ffn2_w_gate, ffn2_w_up, ffn2_w_down, ln3_g, ln3_b):
    batch, seq, d_model = x.shape
    depth, _, d_ff = ffn1_w_gate.shape
    rank, gla_dk = gla_decay_w_fwd.shape[1:]
    gla_dv = gla_norm_g.shape[1]
    conv_c = conv_w.shape[2]
    dil_dim = w_out.shape[1] - gla_dv - conv_c
    dil_hd = dil_dim // DIL_HEADS
    hdk, hdv = gla_dk // GLA_HEADS, gla_dv // GLA_HEADS
    hk, hv = _round_up(hdk, LANES), _round_up(hdv, LANES)
    assert dil_hd == LANES and 2 * rank <= LANES and seq % GLA_CHUNK == 0

    alpha = (2.0 * depth) ** 0.25
    m = batch * seq
    tm = min(512, m)
    tf = 2 * MXU_WIDTH
    tn = 2 * MXU_WIDTH
    fp = _round_up(d_ff, tf)

    widths = (("gq", GLA_HEADS * hk), ("gk", GLA_HEADS * hk), ("gv", GLA_HEADS * hv), ("gg", GLA_HEADS * hv),
              ("cv", 2 * conv_c), ("dq", dil_dim), ("dk", dil_dim), ("dv", dil_dim), ("r", LANES))
    offs, n_proj = {}, 0
    for name, wdt in widths:
        offs[name] = n_proj
        n_proj += wdt
    assert offs["cv"] % (2 * conv_c) == 0
    n_proj_pad = _round_up(n_proj, tn)

    in_sizes = (gla_dk, gla_dk, gla_dv, rank, rank, gla_dv, dil_dim, dil_dim, dil_dim, 2 * conv_c)
    cuts = [int(c) for c in np.cumsum(in_sizes)[:-1]]
    slopes = jnp.asarray(2.0 ** (-8.0 * np.arange(1, DIL_HEADS + 1) / DIL_HEADS), F32)

    def head_mats(wdec, row0):
        w3 = _pad_heads(wdec, GLA_HEADS, hk).reshape(rank, GLA_HEADS, hk).transpose(1, 0, 2)
        return jnp.pad(w3, ((0, 0), (row0, LANES - rank - row0), (0, 0)))

    xf = x.reshape(m, d_model)
    for l in range(depth):
        def ffn_weights(wg, wu, wd):
            padc = ((0, 0), (0, fp - d_ff))
            return (jnp.pad(wg, padc).astype(BF16), jnp.pad(wu, padc).astype(BF16),
                    jnp.pad(wd, ((0, fp - d_ff), (0, 0))).astype(BF16))

        wg1, wu1, wd1 = ffn_weights(ffn1_w_gate[l], ffn1_w_up[l], ffn1_w_down[l])
        xf = _ffn_ln(xf, wg1, wu1, wd1, ln1_g[l][None], ln1_b[l][None], alpha=alpha, tm=tm, tf=tf)

        gq, gk, gv, rf, rb, gg, dq, dk, dv, cv = jnp.split(w_in[l], cuts, axis=1)
        w_in_p = jnp.concatenate(
            [_pad_heads(gq, GLA_HEADS, hk), _pad_heads(gk, GLA_HEADS, hk), _pad_heads(gv, GLA_HEADS, hv),
             _pad_heads(gg, GLA_HEADS, hv), cv, dq, dk, dv, rf, rb,
             jnp.zeros((d_model, n_proj_pad - n_proj + LANES - 2 * rank), F32)], axis=1).astype(BF16)
        proj = _in_proj(xf, w_in_p, tm=tm, tn=tn)

        gla_o = _gla(proj, head_mats(gla_decay_w_fwd[l], 0), head_mats(gla_decay_w_bwd[l], rank),
                     _pad_heads(gla_decay_b_fwd[l], GLA_HEADS, hk).reshape(GLA_HEADS, 1, hk),
                     _pad_heads(gla_decay_b_bwd[l], GLA_HEADS, hk).reshape(GLA_HEADS, 1, hk),
                     _pad_heads(gla_norm_g[l], GLA_HEADS, hv).reshape(GLA_HEADS, 1, hv),
                     batch=batch, seq=seq, offs=offs, hk=hk, hv=hv, q_scale=hdk ** -0.5, dv_head=hdv)
        dil_o = _dil(proj, slopes, batch=batch, seq=seq, offs=offs, hd=dil_hd)
        conv_o = _conv(proj, conv_w[l], conv_b[l][None], conv_ln_g[l][None], conv_ln_b[l][None],
                       batch=batch, seq=seq, offs=offs)

        wo = w_out[l]
        wa = jnp.pad(wo[:gla_dv].reshape(GLA_HEADS, hdv, d_model), ((0, 0), (0, hv - hdv), (0, 0)))
        wa = wa.reshape(GLA_HEADS * hv, d_model).astype(BF16)
        wdl = wo[gla_dv:gla_dv + dil_dim].astype(BF16)
        wc = wo[gla_dv + dil_dim:].astype(BF16)
        xf = _out_ln(xf, gla_o, dil_o, conv_o, wa, wdl, wc, ln2_g[l][None], ln2_b[l][None], alpha=alpha, tm=tm)

        wg2, wu2, wd2 = ffn_weights(ffn2_w_gate[l], ffn2_w_up[l], ffn2_w_down[l])
        xf = _ffn_ln(xf, wg2, wu2, wd2, ln3_g[l][None], ln3_b[l][None], alpha=alpha, tm=tm, tf=tf)
    return xf.reshape(batch, seq, d_model)
```

```python
import functools

import numpy as np
import jax
import jax.numpy as jnp
from jax import lax
from jax.experimental import pallas as pl
from jax.experimental.pallas import tpu as pltpu

GLA_HEADS = 4
GLA_TAU = 16.0
GLA_CHUNK = 64
DIL_HEADS = 6
DIL_BRANCHES = ((128, 1), (512, 4), (2048, 16))
LN_EPS = 1e-5
MASK_VALUE = -1e30

LANES = 128
MXU_WIDTH = 256
VMEM_LIMIT_BYTES = 56 * 1024 * 1024

F32 = jnp.float32
BF16 = jnp.bfloat16
_NT_DIMS = (((1,), (1,)), ((), ()))
_TN_DIMS = (((0,), (0,)), ((), ()))


def _round_up(n, m):
    return (n + m - 1) // m * m


def _layer_norm(y, g, b):
    mu = jnp.mean(y, axis=-1, keepdims=True)
    yc = y - mu
    var = jnp.mean(yc * yc, axis=-1, keepdims=True)
    return yc * lax.rsqrt(var + LN_EPS) * g + b


def _params(n_grid_axes):
    return pltpu.CompilerParams(dimension_semantics=("arbitrary",) * n_grid_axes,
                                vmem_limit_bytes=VMEM_LIMIT_BYTES)


def _layer_spec(shape, layer):
    zeros = (0,) * len(shape)
    return pl.BlockSpec((None,) + tuple(shape), lambda *_: (layer,) + zeros)


def _swiglu_part(xb, wg, wu, wd):
    gate = jnp.dot(xb, wg, preferred_element_type=F32)
    up = jnp.dot(xb, wu, preferred_element_type=F32)
    h = (gate * jax.nn.sigmoid(gate) * up).astype(BF16)
    return jnp.dot(h, wd, preferred_element_type=F32)


def _ffn_ln_kernel(x_ref, wg_ref, wu_ref, wd_ref, wgt_ref, wut_ref, wdt_ref, g_ref, b_ref, o_ref, xb_ref, *, alpha):
    f = pl.program_id(1)

    @pl.when(f == 0)
    def _():
        xb = x_ref[...].astype(BF16)
        xb_ref[...] = xb
        o_ref[...] = _swiglu_part(xb, wgt_ref[...], wut_ref[...], wdt_ref[...])

    o_ref[...] += _swiglu_part(xb_ref[...], wg_ref[...], wu_ref[...], wd_ref[...])

    @pl.when(f == pl.num_programs(1) - 1)
    def _():
        y = alpha * x_ref[...] + 0.5 * o_ref[...]
        o_ref[...] = _layer_norm(y, g_ref[...], b_ref[...])


def _ffn_tiles(d_ff):
    for tf in (3 * MXU_WIDTH, 2 * MXU_WIDTH, MXU_WIDTH):
        tail = d_ff % tf
        if tail and tail % LANES == 0 and (d_ff - tail) % tail == 0:
            return tf, tail
    raise NotImplementedError(f"no FFN tiling for d_ff={d_ff}")


def _ffn_ln(x, wg, wu, wd, g, b, layer, *, alpha, tm):
    m, d = x.shape
    d_ff = wg.shape[2]
    tf, tail = _ffn_tiles(d_ff)
    n_f = d_ff // tf
    tail_blk = (d_ff - tail) // tail
    return pl.pallas_call(
        functools.partial(_ffn_ln_kernel, alpha=alpha),
        grid=(m // tm, n_f),
        in_specs=[
            pl.BlockSpec((tm, d), lambda i, f: (i, 0)),
            pl.BlockSpec((None, d, tf), lambda i, f: (layer, 0, f)),
            pl.BlockSpec((None, d, tf), lambda i, f: (layer, 0, f)),
            pl.BlockSpec((None, tf, d), lambda i, f: (layer, f, 0)),
            pl.BlockSpec((None, d, tail), lambda i, f: (layer, 0, tail_blk)),
            pl.BlockSpec((None, d, tail), lambda i, f: (layer, 0, tail_blk)),
            pl.BlockSpec((None, tail, d), lambda i, f: (layer, tail_blk, 0)),
            _layer_spec((1, d), layer),
            _layer_spec((1, d), layer),
        ],
        out_specs=pl.BlockSpec((tm, d), lambda i, f: (i, 0)),
        out_shape=jax.ShapeDtypeStruct((m, d), F32),
        scratch_shapes=[pltpu.VMEM((tm, d), BF16)],
        compiler_params=_params(2),
        name="ffn_ln",
    )(x, wg, wu, wd, wg, wu, wd, g, b)


def _in_proj_kernel(x_ref, w_ref, o_ref, xb_ref):
    @pl.when(pl.program_id(1) == 0)
    def _():
        xb_ref[...] = x_ref[...].astype(BF16)

    o_ref[...] = jnp.dot(xb_ref[...], w_ref[...], preferred_element_type=F32)


def _in_proj(x, w, layer, *, tm, tn):
    m, d = x.shape
    n = w.shape[2]
    return pl.pallas_call(
        _in_proj_kernel,
        grid=(m // tm, n // tn),
        in_specs=[
            pl.BlockSpec((tm, d), lambda i, j: (i, 0)),
            pl.BlockSpec((None, d, tn), lambda i, j: (layer, 0, j)),
        ],
        out_specs=pl.BlockSpec((tm, tn), lambda i, j: (i, j)),
        out_shape=jax.ShapeDtypeStruct((m, n), F32),
        scratch_shapes=[pltpu.VMEM((tm, d), BF16)],
        compiler_params=_params(2),
        name="in_proj",
    )(x, w)


def _log_sigmoid(z):
    return jnp.minimum(z, 0.0) - jnp.log(1.0 + jnp.exp(-jnp.abs(z)))


def _gla_kernel(q_ref, k_ref, v_ref, gg_ref, r_ref, wf_ref, wb_ref, bf_ref, bb_ref, ng_ref,
                o_ref, of_ref, ob_ref, stf_ref, stb_ref, *, seq, tile, chunk, q_scale, inv_dv):
    n_tiles = seq // tile
    n_chunks = tile // chunk
    hk = q_ref.shape[-1]
    shift = chunk.bit_length() - 1
    row = lax.broadcasted_iota(jnp.int32, (tile, tile), 0)
    col = lax.broadcasted_iota(jnp.int32, (tile, tile), 1)
    same_chunk = lax.shift_right_logical(row, shift) == lax.shift_right_logical(col, shift)

    def tile_step(ti, forward):
        st_ref, oacc_ref = (stf_ref, of_ref) if forward else (stb_ref, ob_ref)
        w = wf_ref[0] if forward else wb_ref[0]
        bias = bf_ref[0] if forward else bb_ref[0]
        tri = same_chunk & ((col <= row) if forward else (col >= row))
        rows = pl.ds(pl.multiple_of(ti * tile, tile), tile)

        z = jnp.dot(r_ref[rows, :].astype(BF16), w, preferred_element_type=F32) + bias
        la = _log_sigmoid(z) * (1.0 / GLA_TAU)
        la_hi = la.astype(BF16)
        la_lo = (la - la_hi.astype(F32)).astype(BF16)
        cum = jnp.dot(tri.astype(BF16), jnp.concatenate([la_hi, la_lo], axis=1), preferred_element_type=F32)
        bcum = cum[:, :hk] + cum[:, hk:]

        vb = v_ref[rows, :].astype(BF16)
        q_dec = (q_ref[rows, :] * q_scale * jnp.exp(bcum)).astype(BF16)
        k_inv = k_ref[rows, :] * jnp.exp(-bcum)
        att = lax.dot_general(q_dec, k_inv.astype(BF16), _NT_DIMS, preferred_element_type=F32)
        att = jnp.where(tri, att, 0.0).astype(BF16)
        o_intra = jnp.dot(att, vb, preferred_element_type=F32)

        o_inter = [None] * n_chunks
        for c in (range(n_chunks) if forward else reversed(range(n_chunks))):
            lo = c * chunk
            end_row = lo + chunk - 1 if forward else lo
            e_end = jnp.exp(bcum[end_row:end_row + 1, :])
            st = st_ref[...]
            o_inter[c] = lax.dot_general(q_dec[lo:lo + chunk], st.astype(BF16), _NT_DIMS,
                                         preferred_element_type=F32)
            k_end = (k_inv[lo:lo + chunk] * e_end).astype(BF16)
            upd = lax.dot_general(vb[lo:lo + chunk], k_end, _TN_DIMS, preferred_element_type=F32)
            st_ref[...] = st * e_end + upd
        oacc_ref[rows, :] = o_intra + jnp.concatenate(o_inter, axis=0)

    stf_ref[...] = jnp.zeros_like(stf_ref)
    stb_ref[...] = jnp.zeros_like(stb_ref)

    def body(i, carry):
        tile_step(i, True)
        tile_step(n_tiles - 1 - i, False)
        return carry

    lax.fori_loop(0, n_tiles, body, 0)

    def finish(i, carry):
        rows = pl.ds(pl.multiple_of(i * tile, tile), tile)
        o = of_ref[rows, :] + ob_ref[rows, :]
        ms = jnp.sum(o * o, axis=-1, keepdims=True) * inv_dv
        gate = gg_ref[rows, :]
        o = o * lax.rsqrt(ms + LN_EPS) * ng_ref[0] * (gate * jax.nn.sigmoid(gate))
        o_ref[rows, :] = o.astype(o_ref.dtype)
        return carry

    lax.fori_loop(0, n_tiles, finish, 0)


def _gla(proj, wdec_f, wdec_b, bdec_f, bdec_b, norm_g, layer, *, batch, seq, offs, hk, hv, q_scale, dv_head):
    heads = wdec_f.shape[1]
    tile = min(256, seq)
    kern = functools.partial(_gla_kernel, seq=seq, tile=tile, chunk=GLA_CHUNK, q_scale=q_scale,
                             inv_dv=1.0 / dv_head)
    cq, ck, cv, cg, cr = (offs["gq"] // hk, offs["gk"] // hk, offs["gv"] // hv, offs["gg"] // hv,
                          offs["r"] // LANES)
    per_head = lambda rows, cols: pl.BlockSpec((None, 1, rows, cols), lambda b, h: (layer, h, 0, 0))
    return pl.pallas_call(
        kern,
        grid=(batch, heads),
        in_specs=[
            pl.BlockSpec((seq, hk), lambda b, h: (b, cq + h)),
            pl.BlockSpec((seq, hk), lambda b, h: (b, ck + h)),
            pl.BlockSpec((seq, hv), lambda b, h: (b, cv + h)),
            pl.BlockSpec((seq, hv), lambda b, h: (b, cg + h)),
            pl.BlockSpec((seq, LANES), lambda b, h: (b, cr)),
            per_head(LANES, hk), per_head(LANES, hk),
            per_head(1, hk), per_head(1, hk), per_head(1, hv),
        ],
        out_specs=pl.BlockSpec((seq, hv), lambda b, h: (b, h)),
        out_shape=jax.ShapeDtypeStruct((batch * seq, heads * hv), BF16),
        scratch_shapes=[pltpu.VMEM((seq, hv), F32), pltpu.VMEM((seq, hv), F32),
                        pltpu.VMEM((hv, hk), F32), pltpu.VMEM((hv, hk), F32)],
        compiler_params=_params(2),
        name="gla",
    )(proj, proj, proj, proj, proj, wdec_f, wdec_b, bdec_f, bdec_b, norm_g)


def _dil_kernel(slopes_ref, q_ref, k_ref, v_ref, o_ref, *stats, seq, scale):
    n_br = len(DIL_BRANCHES)
    m_refs, l_refs, acc_refs = stats[:n_br], stats[n_br:2 * n_br], stats[2 * n_br:]
    slope = slopes_ref[pl.program_id(1)]
    hd = q_ref.shape[-1]

    def block(n, idx):
        w, d = DIL_BRANCHES[n]
        reach = (w // 2) // d
        length = seq // d
        qb = min(128, length)
        kb = min(length, qb + 2 * reach)
        n_qb = length // qb
        if d == 1:
            r, jb = 0, idx
        else:
            r, jb = lax.div(idx, jnp.int32(n_qb)), lax.rem(idx, jnp.int32(n_qb))
        i0 = jb * qb
        ks = jnp.clip(i0 - reach, 0, length - kb)
        if d == 1:
            qrows = pl.ds(pl.multiple_of(i0, qb), qb)
            krows = pl.ds(pl.multiple_of(ks, 8), kb)
        else:
            qrows = pl.ds(r + d * i0, qb, stride=d)
            krows = pl.ds(r + d * ks, kb, stride=d)
        q = (q_ref[qrows, :] * scale).astype(BF16)
        k = k_ref[krows, :].astype(BF16)
        v = v_ref[krows, :].astype(BF16)
        s = lax.dot_general(q, k, _NT_DIMS, preferred_element_type=F32)
        rel = (lax.broadcasted_iota(jnp.int32, (qb, kb), 1) - lax.broadcasted_iota(jnp.int32, (qb, kb), 0))
        dist = jnp.abs(rel + (ks - i0))
        s = jnp.where(dist <= reach, s - (slope * float(d)) * dist.astype(F32), MASK_VALUE)
        m_blk = jnp.max(s, axis=-1, keepdims=True)
        p = jnp.exp(s - m_blk)
        m_refs[n][qrows, :] = jnp.broadcast_to(m_blk, (qb, hd))
        l_refs[n][qrows, :] = jnp.broadcast_to(jnp.sum(p, axis=-1, keepdims=True), (qb, hd))
        acc_refs[n][qrows, :] = jnp.dot(p.astype(BF16), v, preferred_element_type=F32)

    counts = [seq // min(128, seq // d) for _, d in DIL_BRANCHES]
    if len(set(counts)) == 1:
        def body(idx, carry):
            for n in range(n_br):
                block(n, idx)
            return carry
        lax.fori_loop(0, counts[0], body, 0, unroll=2)
    else:
        for n in range(n_br):
            def body(idx, carry, n=n):
                block(n, idx)
                return carry
            lax.fori_loop(0, counts[n], body, 0)

    rt = min(256, seq)

    def finish(i, carry):
        rows = pl.ds(pl.multiple_of(i * rt, rt), rt)
        ms = [m_ref[rows, :] for m_ref in m_refs]
        m_all = functools.reduce(jnp.maximum, ms)
        num = jnp.zeros((rt, hd), F32)
        den = jnp.zeros((rt, hd), F32)
        for n in range(n_br):
            wgt = jnp.exp(ms[n] - m_all)
            num += wgt * acc_refs[n][rows, :]
            den += wgt * l_refs[n][rows, :]
        o_ref[rows, :] = (num / den).astype(o_ref.dtype)
        return carry

    lax.fori_loop(0, seq // rt, finish, 0)


def _dil(proj, slopes, *, batch, seq, offs, hd):
    cq, ck, cv = offs["dq"] // hd, offs["dk"] // hd, offs["dv"] // hd
    kern = functools.partial(_dil_kernel, seq=seq, scale=hd ** -0.5)
    return pl.pallas_call(
        kern,
        grid=(batch, DIL_HEADS),
        in_specs=[
            pl.BlockSpec(memory_space=pltpu.SMEM),
            pl.BlockSpec((seq, hd), lambda b, h: (b, cq + h)),
            pl.BlockSpec((seq, hd), lambda b, h: (b, ck + h)),
            pl.BlockSpec((seq, hd), lambda b, h: (b, cv + h)),
        ],
        out_specs=pl.BlockSpec((seq, hd), lambda b, h: (b, h)),
        out_shape=jax.ShapeDtypeStruct((batch * seq, DIL_HEADS * hd), BF16),
        scratch_shapes=[pltpu.VMEM((seq, hd), F32)] * (3 * len(DIL_BRANCHES)),
        compiler_params=_params(2),
        name="dil",
    )(slopes, proj, proj, proj)


def _conv_kernel(prev_ref, cur_ref, next_ref, w_ref, cb_ref, g_ref, b_ref, o_ref, u_ref, *, halo, taps):
    i = pl.program_id(1)
    c = o_ref.shape[-1]
    tq = o_ref.shape[0]

    def glu(ref):
        x = ref[...]
        return x[:, :c] * jax.nn.sigmoid(x[:, c:])

    u_ref[0:halo, :] = jnp.where(i > 0, glu(prev_ref), 0.0)
    u_ref[halo:halo + tq, :] = glu(cur_ref)
    u_ref[halo + tq:, :] = jnp.where(i < pl.num_programs(1) - 1, glu(next_ref), 0.0)

    pad = taps // 2
    y = jnp.zeros((tq, c), F32) + cb_ref[...]
    for t in range(taps):
        y = y + u_ref[pl.ds(halo - pad + t, tq), :] * w_ref[t:t + 1, :]
    y = _layer_norm(y, g_ref[...], b_ref[...])
    o_ref[...] = (y * jax.nn.sigmoid(y)).astype(o_ref.dtype)


def _conv(proj, conv_w, conv_b, ln_g, ln_b, layer, *, batch, seq, offs):
    taps, c = conv_w.shape[1:]
    halo = 16
    assert taps // 2 <= halo
    tq = min(512, seq)
    nt = seq // tq
    hb = tq // halo
    ccol = offs["cv"] // (2 * c)
    kern = functools.partial(_conv_kernel, halo=halo, taps=taps)
    return pl.pallas_call(
        kern,
        grid=(batch, nt),
        in_specs=[
            pl.BlockSpec((halo, 2 * c), lambda b, i: (jnp.maximum((b * nt + i) * hb - 1, 0), ccol)),
            pl.BlockSpec((tq, 2 * c), lambda b, i: (b * nt + i, ccol)),
            pl.BlockSpec((halo, 2 * c),
                         lambda b, i: (jnp.minimum((b * nt + i + 1) * hb, batch * nt * hb - 1), ccol)),
            _layer_spec((taps, c), layer),
            _layer_spec((1, c), layer),
            _layer_spec((1, c), layer),
            _layer_spec((1, c), layer),
        ],
        out_specs=pl.BlockSpec((tq, c), lambda b, i: (b * nt + i, 0)),
        out_shape=jax.ShapeDtypeStruct((batch * seq, c), BF16),
        scratch_shapes=[pltpu.VMEM((tq + 2 * halo, c), F32)],
        compiler_params=_params(2),
        name="conv",
    )(proj, proj, proj, conv_w, conv_b, ln_g, ln_b)


def _out_ln_kernel(x_ref, a_ref, d_ref, c_ref, w_ref, g_ref, b_ref, o_ref, *, alpha):
    na, nd = a_ref.shape[1], d_ref.shape[1]
    mix = jnp.dot(a_ref[...], w_ref[0:na, :], preferred_element_type=F32)
    mix += jnp.dot(d_ref[...], w_ref[na:na + nd, :], preferred_element_type=F32)
    mix += jnp.dot(c_ref[...], w_ref[na + nd:, :], preferred_element_type=F32)
    o_ref[...] = _layer_norm(alpha * x_ref[...] + mix, g_ref[...], b_ref[...])


def _out_ln(x, gla_o, dil_o, conv_o, w, g, b, layer, *, alpha, tm):
    m, d = x.shape
    row = lambda i: (i, 0)
    return pl.pallas_call(
        functools.partial(_out_ln_kernel, alpha=alpha),
        grid=(m // tm,),
        in_specs=[
            pl.BlockSpec((tm, d), row),
            pl.BlockSpec((tm, gla_o.shape[1]), row),
            pl.BlockSpec((tm, dil_o.shape[1]), row),
            pl.BlockSpec((tm, conv_o.shape[1]), row),
            _layer_spec(w.shape[1:], layer),
            _layer_spec((1, d), layer),
            _layer_spec((1, d), layer),
        ],
        out_specs=pl.BlockSpec((tm, d), row),
        out_shape=jax.ShapeDtypeStruct((m, d), F32),
        compiler_params=_params(1),
        name="out_ln",
    )(x, gla_o, dil_o, conv_o, w, g, b)


def _pad_heads(w, heads, width):
    lead = w.shape[:-1]
    hd = w.shape[-1] // heads
    w = w.reshape(lead + (heads, hd))
    w = jnp.pad(w, [(0, 0)] * len(lead) + [(0, 0), (0, width - hd)])
    return w.reshape(lead + (heads * width,))


def kernel(x, ffn1_w_gate, ffn1_w_up, ffn1_w_down, ln1_g, ln1_b, w_in, gla_decay_w_fwd, gla_decay_b_fwd, gla_decay_w_bwd, gla_decay_b_bwd, gla_norm_g, conv_w, conv_b, conv_ln_g, conv_ln_b, w_out, ln2_g, ln2_b, ffn2_w_gate, ffn2_w_up, ffn2_w_down, ln3_g, ln3_b):
    batch, seq, d_model = x.shape
    depth = ffn1_w_gate.shape[0]
    rank, gla_dk = gla_decay_w_fwd.shape[1:]
    gla_dv = gla_norm_g.shape[1]
    conv_c = conv_w.shape[2]
    dil_dim = w_out.shape[1] - gla_dv - conv_c
    dil_hd = dil_dim // DIL_HEADS
    hdk, hdv = gla_dk // GLA_HEADS, gla_dv // GLA_HEADS
    hk, hv = _round_up(hdk, LANES), _round_up(hdv, LANES)
    assert dil_hd == LANES and 2 * rank <= LANES and seq % GLA_CHUNK == 0

    alpha = (2.0 * depth) ** 0.25
    m = batch * seq
    tm = min(512, m)
    tn = 2 * MXU_WIDTH

    widths = (("gq", GLA_HEADS * hk), ("gk", GLA_HEADS * hk), ("gv", GLA_HEADS * hv), ("gg", GLA_HEADS * hv),
              ("cv", 2 * conv_c), ("dq", dil_dim), ("dk", dil_dim), ("dv", dil_dim), ("r", LANES))
    offs, n_proj = {}, 0
    for name, wdt in widths:
        offs[name] = n_proj
        n_proj += wdt
    assert offs["cv"] % (2 * conv_c) == 0
    n_proj_pad = _round_up(n_proj, tn)

    in_sizes = (gla_dk, gla_dk, gla_dv, rank, rank, gla_dv, dil_dim, dil_dim, dil_dim, 2 * conv_c)
    cuts = [int(c) for c in np.cumsum(in_sizes)[:-1]]
    gq, gk, gv, rf, rb, gg, dq, dk, dv, cv = jnp.split(w_in, cuts, axis=2)
    w_in_p = jnp.concatenate(
        [_pad_heads(gq, GLA_HEADS, hk), _pad_heads(gk, GLA_HEADS, hk), _pad_heads(gv, GLA_HEADS, hv),
         _pad_heads(gg, GLA_HEADS, hv), cv, dq, dk, dv, rf, rb,
         jnp.zeros((depth, d_model, n_proj_pad - n_proj + LANES - 2 * rank), F32)], axis=2).astype(BF16)

    def head_mats(wdec, row0):
        w4 = _pad_heads(wdec, GLA_HEADS, hk).reshape(depth, rank, GLA_HEADS, hk).transpose(0, 2, 1, 3)
        return jnp.pad(w4, ((0, 0), (0, 0), (row0, LANES - rank - row0), (0, 0))).astype(BF16)

    wdec_f, wdec_b = head_mats(gla_decay_w_fwd, 0), head_mats(gla_decay_w_bwd, rank)
    bdec_f = _pad_heads(gla_decay_b_fwd, GLA_HEADS, hk).reshape(depth, GLA_HEADS, 1, hk)
    bdec_b = _pad_heads(gla_decay_b_bwd, GLA_HEADS, hk).reshape(depth, GLA_HEADS, 1, hk)
    norm_g = _pad_heads(gla_norm_g, GLA_HEADS, hv).reshape(depth, GLA_HEADS, 1, hv)

    w_gla = jnp.pad(w_out[:, :gla_dv].reshape(depth, GLA_HEADS, hdv, d_model), ((0, 0), (0, 0), (0, hv - hdv), (0, 0)))
    w_out_p = jnp.concatenate([w_gla.reshape(depth, GLA_HEADS * hv, d_model), w_out[:, gla_dv:]], axis=1).astype(BF16)

    ffn1 = [w.astype(BF16) for w in (ffn1_w_gate, ffn1_w_up, ffn1_w_down)]
    ffn2 = [w.astype(BF16) for w in (ffn2_w_gate, ffn2_w_up, ffn2_w_down)]
    vec = lambda p: p[:, None, :]
    slopes = jnp.asarray(2.0 ** (-8.0 * np.arange(1, DIL_HEADS + 1) / DIL_HEADS), F32)

    xf = x.reshape(m, d_model)
    for l in range(depth):
        xf = _ffn_ln(xf, *ffn1, vec(ln1_g), vec(ln1_b), l, alpha=alpha, tm=tm)
        proj = _in_proj(xf, w_in_p, l, tm=min(2 * tm, m), tn=tn)
        gla_o = _gla(proj, wdec_f, wdec_b, bdec_f, bdec_b, norm_g, l, batch=batch, seq=seq, offs=offs,
                     hk=hk, hv=hv, q_scale=hdk ** -0.5, dv_head=hdv)
        dil_o = _dil(proj, slopes, batch=batch, seq=seq, offs=offs, hd=dil_hd)
        conv_o = _conv(proj, conv_w, vec(conv_b), vec(conv_ln_g), vec(conv_ln_b), l, batch=batch, seq=seq, offs=offs)
        xf = _out_ln(xf, gla_o, dil_o, conv_o, w_out_p, vec(ln2_g), vec(ln2_b), l, alpha=alpha, tm=tm)
        xf = _ffn_ln(xf, *ffn2, vec(ln3_g), vec(ln3_b), l, alpha=alpha, tm=tm)
    return xf.reshape(batch, seq, d_model)
```

```python
import functools

import numpy as np
import jax
import jax.numpy as jnp
from jax import lax
from jax.experimental import pallas as pl
from jax.experimental.pallas import tpu as pltpu

GLA_HEADS = 4
GLA_TAU = 16.0
GLA_CHUNK = 64
DIL_HEADS = 6
DIL_BRANCHES = ((128, 1), (512, 4), (2048, 16))
LN_EPS = 1e-5
MASK_VALUE = -1e30

LANES = 128
MXU_WIDTH = 256
VMEM_LIMIT_BYTES = 56 * 1024 * 1024

F32 = jnp.float32
BF16 = jnp.bfloat16
_NT_DIMS = (((1,), (1,)), ((), ()))
_TN_DIMS = (((0,), (0,)), ((), ()))


def _round_up(n, m):
    return (n + m - 1) // m * m


def _layer_norm(y, g, b):
    mu = jnp.mean(y, axis=-1, keepdims=True)
    yc = y - mu
    var = jnp.mean(yc * yc, axis=-1, keepdims=True)
    return yc * lax.rsqrt(var + LN_EPS) * g + b


def _params(n_grid_axes):
    return pltpu.CompilerParams(dimension_semantics=("arbitrary",) * n_grid_axes,
                                vmem_limit_bytes=VMEM_LIMIT_BYTES)


def _layer_spec(shape, layer):
    zeros = (0,) * len(shape)
    return pl.BlockSpec((None,) + tuple(shape), lambda *_: (layer,) + zeros)


def _swiglu_part(xb, wg, wu, wd):
    gate = jnp.dot(xb, wg, preferred_element_type=F32)
    up = jnp.dot(xb, wu, preferred_element_type=F32)
    h = (gate * jax.nn.sigmoid(gate) * up).astype(BF16)
    return jnp.dot(h, wd, preferred_element_type=F32)


def _ffn_ln_kernel(x_ref, wg_ref, wu_ref, wd_ref, wgt_ref, wut_ref, wdt_ref, g_ref, b_ref, o_ref, xb_ref, *, alpha):
    f = pl.program_id(1)

    @pl.when(f == 0)
    def _():
        xb = x_ref[...].astype(BF16)
        xb_ref[...] = xb
        o_ref[...] = _swiglu_part(xb, wgt_ref[...], wut_ref[...], wdt_ref[...])

    o_ref[...] += _swiglu_part(xb_ref[...], wg_ref[...], wu_ref[...], wd_ref[...])

    @pl.when(f == pl.num_programs(1) - 1)
    def _():
        y = alpha * x_ref[...] + 0.5 * o_ref[...]
        o_ref[...] = _layer_norm(y, g_ref[...], b_ref[...])


def _ffn_tiles(d_ff):
    for tf in (3 * MXU_WIDTH, 2 * MXU_WIDTH, MXU_WIDTH):
        tail = d_ff % tf
        if tail and tail % LANES == 0 and (d_ff - tail) % tail == 0:
            return tf, tail
    raise NotImplementedError(f"no FFN tiling for d_ff={d_ff}")


def _ffn_ln(x, wg, wu, wd, g, b, layer, *, alpha, tm):
    m, d = x.shape
    d_ff = wg.shape[2]
    tf, tail = _ffn_tiles(d_ff)
    n_f = d_ff // tf
    tail_blk = (d_ff - tail) // tail
    return pl.pallas_call(
        functools.partial(_ffn_ln_kernel, alpha=alpha),
        grid=(m // tm, n_f),
        in_specs=[
            pl.BlockSpec((tm, d), lambda i, f: (i, 0)),
            pl.BlockSpec((None, d, tf), lambda i, f: (layer, 0, f)),
            pl.BlockSpec((None, d, tf), lambda i, f: (layer, 0, f)),
            pl.BlockSpec((None, tf, d), lambda i, f: (layer, f, 0)),
            pl.BlockSpec((None, d, tail), lambda i, f: (layer, 0, tail_blk)),
            pl.BlockSpec((None, d, tail), lambda i, f: (layer, 0, tail_blk)),
            pl.BlockSpec((None, tail, d), lambda i, f: (layer, tail_blk, 0)),
            _layer_spec((1, d), layer),
            _layer_spec((1, d), layer),
        ],
        out_specs=pl.BlockSpec((tm, d), lambda i, f: (i, 0)),
        out_shape=jax.ShapeDtypeStruct((m, d), F32),
        scratch_shapes=[pltpu.VMEM((tm, d), BF16)],
        compiler_params=_params(2),
        name="ffn_ln",
    )(x, wg, wu, wd, wg, wu, wd, g, b)


def _in_proj_kernel(x_ref, w_ref, o_ref, xb_ref):
    @pl.when(pl.program_id(1) == 0)
    def _():
        xb_ref[...] = x_ref[...].astype(BF16)

    o_ref[...] = jnp.dot(xb_ref[...], w_ref[...], preferred_element_type=F32)


def _in_proj(x, w, layer, *, tm, tn):
    m, d = x.shape
    n = w.shape[2]
    return pl.pallas_call(
        _in_proj_kernel,
        grid=(m // tm, n // tn),
        in_specs=[
            pl.BlockSpec((tm, d), lambda i, j: (i, 0)),
            pl.BlockSpec((None, d, tn), lambda i, j: (layer, 0, j)),
        ],
        out_specs=pl.BlockSpec((tm, tn), lambda i, j: (i, j)),
        out_shape=jax.ShapeDtypeStruct((m, n), F32),
        scratch_shapes=[pltpu.VMEM((tm, d), BF16)],
        compiler_params=_params(2),
        name="in_proj",
    )(x, w)


def _log_sigmoid(z):
    return jnp.minimum(z, 0.0) - jnp.log(1.0 + jnp.exp(-jnp.abs(z)))


def _gla_kernel(q_ref, k_ref, v_ref, gg_ref, r_ref, wf_ref, wb_ref, bf_ref, bb_ref, ng_ref,
                o_ref, of_ref, ob_ref, stf_ref, stb_ref, *, seq, tile, chunk, q_scale, inv_dv):
    n_tiles = seq // tile
    n_chunks = tile // chunk
    hk = q_ref.shape[-1]
    shift = chunk.bit_length() - 1
    row = lax.broadcasted_iota(jnp.int32, (tile, tile), 0)
    col = lax.broadcasted_iota(jnp.int32, (tile, tile), 1)
    same_chunk = lax.shift_right_logical(row, shift) == lax.shift_right_logical(col, shift)

    tris = (same_chunk & (col <= row), same_chunk & (col >= row))
    st_refs, oacc_refs = (stf_ref, stb_ref), (of_ref, ob_ref)
    ws, biases = (wf_ref[0], wb_ref[0]), (bf_ref[0], bb_ref[0])
    both = (0, 1)

    def pair_step(i):
        rows = [pl.ds(pl.multiple_of(ti * tile, tile), tile) for ti in (i, n_tiles - 1 - i)]
        z = [jnp.dot(r_ref[rows[j], :].astype(BF16), ws[j], preferred_element_type=F32) + biases[j]
             for j in both]
        la = [_log_sigmoid(z[j]) * (1.0 / GLA_TAU) for j in both]
        la_hi = [la[j].astype(BF16) for j in both]
        la_lo = [(la[j] - la_hi[j].astype(F32)).astype(BF16) for j in both]
        cum = [jnp.dot(tris[j].astype(BF16), jnp.concatenate([la_hi[j], la_lo[j]], axis=1),
                       preferred_element_type=F32) for j in both]
        bcum = [cum[j][:, :hk] + cum[j][:, hk:] for j in both]

        vb = [v_ref[rows[j], :].astype(BF16) for j in both]
        q_dec = [(q_ref[rows[j], :] * q_scale * jnp.exp(bcum[j])).astype(BF16) for j in both]
        k_inv = [k_ref[rows[j], :] * jnp.exp(-bcum[j]) for j in both]
        att = [lax.dot_general(q_dec[j], k_inv[j].astype(BF16), _NT_DIMS, preferred_element_type=F32)
               for j in both]
        att = [jnp.where(tris[j], att[j], 0.0).astype(BF16) for j in both]
        o_intra = [jnp.dot(att[j], vb[j], preferred_element_type=F32) for j in both]

        o_inter = [[None] * n_chunks for _ in both]
        for s in range(n_chunks):
            for j in both:
                c = s if j == 0 else n_chunks - 1 - s
                lo = c * chunk
                end_row = lo + chunk - 1 if j == 0 else lo
                e_end = jnp.exp(bcum[j][end_row:end_row + 1, :])
                st = st_refs[j][...]
                o_inter[j][c] = lax.dot_general(q_dec[j][lo:lo + chunk], st.astype(BF16), _NT_DIMS,
                                                preferred_element_type=F32)
                k_end = (k_inv[j][lo:lo + chunk] * e_end).astype(BF16)
                upd = lax.dot_general(vb[j][lo:lo + chunk], k_end, _TN_DIMS, preferred_element_type=F32)
                st_refs[j][...] = st * e_end + upd
        for j in both:
            oacc_refs[j][rows[j], :] = o_intra[j] + jnp.concatenate(o_inter[j], axis=0)

    stf_ref[...] = jnp.zeros_like(stf_ref)
    stb_ref[...] = jnp.zeros_like(stb_ref)

    def body(i, carry):
        pair_step(i)
        return carry

    lax.fori_loop(0, n_tiles, body, 0)

    def finish(i, carry):
        rows = pl.ds(pl.multiple_of(i * tile, tile), tile)
        o = of_ref[rows, :] + ob_ref[rows, :]
        ms = jnp.sum(o * o, axis=-1, keepdims=True) * inv_dv
        gate = gg_ref[rows, :]
        o = o * lax.rsqrt(ms + LN_EPS) * ng_ref[0] * (gate * jax.nn.sigmoid(gate))
        o_ref[rows, :] = o.astype(o_ref.dtype)
        return carry

    lax.fori_loop(0, n_tiles, finish, 0)


def _gla(proj, wdec_f, wdec_b, bdec_f, bdec_b, norm_g, layer, *, batch, seq, offs, hk, hv, q_scale, dv_head):
    heads = wdec_f.shape[1]
    tile = min(256, seq)
    kern = functools.partial(_gla_kernel, seq=seq, tile=tile, chunk=GLA_CHUNK, q_scale=q_scale,
                             inv_dv=1.0 / dv_head)
    cq, ck, cv, cg, cr = (offs["gq"] // hk, offs["gk"] // hk, offs["gv"] // hv, offs["gg"] // hv,
                          offs["r"] // LANES)
    per_head = lambda rows, cols: pl.BlockSpec((None, 1, rows, cols), lambda b, h: (layer, h, 0, 0))
    return pl.pallas_call(
        kern,
        grid=(batch, heads),
        in_specs=[
            pl.BlockSpec((seq, hk), lambda b, h: (b, cq + h)),
            pl.BlockSpec((seq, hk), lambda b, h: (b, ck + h)),
            pl.BlockSpec((seq, hv), lambda b, h: (b, cv + h)),
            pl.BlockSpec((seq, hv), lambda b, h: (b, cg + h)),
            pl.BlockSpec((seq, LANES), lambda b, h: (b, cr)),
            per_head(LANES, hk), per_head(LANES, hk),
            per_head(1, hk), per_head(1, hk), per_head(1, hv),
        ],
        out_specs=pl.BlockSpec((seq, hv), lambda b, h: (b, h)),
        out_shape=jax.ShapeDtypeStruct((batch * seq, heads * hv), BF16),
        scratch_shapes=[pltpu.VMEM((seq, hv), F32), pltpu.VMEM((seq, hv), F32),
                        pltpu.VMEM((hv, hk), F32), pltpu.VMEM((hv, hk), F32)],
        compiler_params=_params(2),
        name="gla",
    )(proj, proj, proj, proj, proj, wdec_f, wdec_b, bdec_f, bdec_b, norm_g)


def _dil_cfgs(seq):
    cfgs = []
    for w, d in DIL_BRANCHES:
        reach = (w // 2) // d
        length = seq // d
        qb = min(128, length)
        kb = min(length, qb + 2 * reach)
        n_qb = length // qb
        rel_starts = sorted({min(max(jb * qb - reach, 0), length - kb) - jb * qb for jb in range(n_qb)},
                            reverse=True)
        cfgs.append((d, reach, length, qb, kb, n_qb, rel_starts))
    return cfgs


def _dil_kernel(slopes_ref, q_ref, k_ref, v_ref, o_ref, *stats, seq, scale):
    n_br = len(DIL_BRANCHES)
    m_refs, l_refs, acc_refs, bias_refs = (stats[i * n_br:(i + 1) * n_br] for i in range(4))
    slope = slopes_ref[pl.program_id(1)]
    hd = q_ref.shape[-1]

    cfgs = _dil_cfgs(seq)
    for n, (d, reach, length, qb, kb, n_qb, rel_starts) in enumerate(cfgs):
        rel = lax.broadcasted_iota(jnp.int32, (qb, kb), 1) - lax.broadcasted_iota(jnp.int32, (qb, kb), 0)
        for vi, rs in enumerate(rel_starts):
            dist = jnp.abs(rel + rs)
            bias_refs[n][vi] = jnp.where(dist <= reach, -(slope * float(d)) * dist.astype(F32), MASK_VALUE)

    def blocks(jobs):
        place = []
        for n, idx in jobs:
            d, reach, length, qb, kb, n_qb, rel_starts = cfgs[n]
            if d == 1:
                r, jb = 0, idx
            else:
                r, jb = lax.div(idx, jnp.int32(n_qb)), lax.rem(idx, jnp.int32(n_qb))
            i0 = jb * qb
            ks = jnp.clip(i0 - reach, 0, length - kb)
            variant = sum(((ks - i0) <= rs).astype(jnp.int32) for rs in rel_starts[1:])
            if d == 1:
                qrows = pl.ds(pl.multiple_of(i0, qb), qb)
                krows = pl.ds(pl.multiple_of(ks, 8), kb)
            else:
                qrows = pl.ds(r + d * i0, qb, stride=d)
                krows = pl.ds(r + d * ks, kb, stride=d)
            place.append((n, qrows, krows, variant, qb, kb))
        s = []
        for n, qrows, krows, variant, qb, kb in place:
            q = (q_ref[qrows, :] * scale).astype(BF16)
            k = k_ref[krows, :].astype(BF16)
            s.append(lax.dot_general(q, k, _NT_DIMS, preferred_element_type=F32) + bias_refs[n][variant])
        m = [jnp.max(sj, axis=-1, keepdims=True) for sj in s]
        p = [jnp.exp(sj - mj).astype(BF16) for sj, mj in zip(s, m)]
        for (n, qrows, krows, variant, qb, kb), mj, pj in zip(place, m, p):
            v_ones = jnp.concatenate([v_ref[krows, :].astype(BF16), jnp.ones((kb, hd), BF16)], axis=1)
            acc_l = jnp.dot(pj, v_ones, preferred_element_type=F32)
            acc_refs[n][qrows, :] = acc_l[:, :hd]
            l_refs[n][qrows, :] = acc_l[:, hd:]
            m_refs[n][qrows, :] = jnp.broadcast_to(mj, (qb, hd))

    counts = [seq // min(128, seq // d) for _, d in DIL_BRANCHES]
    group = 4
    if len(set(counts)) == 1 and counts[0] % group == 0:
        def body(i, carry):
            blocks([(n, group * i + u) for u in range(group) for n in range(n_br)])
            return carry
        lax.fori_loop(0, counts[0] // group, body, 0)
    else:
        for n in range(n_br):
            def body(idx, carry, n=n):
                blocks([(n, idx)])
                return carry
            lax.fori_loop(0, counts[n], body, 0)

    rt = min(256, seq)

    def finish(i, carry):
        rows = pl.ds(pl.multiple_of(i * rt, rt), rt)
        ms = [m_ref[rows, :] for m_ref in m_refs]
        m_all = functools.reduce(jnp.maximum, ms)
        num = jnp.zeros((rt, hd), F32)
        den = jnp.zeros((rt, hd), F32)
        for n in range(n_br):
            wgt = jnp.exp(ms[n] - m_all)
            num += wgt * acc_refs[n][rows, :]
            den += wgt * l_refs[n][rows, :]
        o_ref[rows, :] = (num / den).astype(o_ref.dtype)
        return carry

    lax.fori_loop(0, seq // rt, finish, 0)


def _dil(proj, slopes, *, batch, seq, offs, hd):
    cq, ck, cv = offs["dq"] // hd, offs["dk"] // hd, offs["dv"] // hd
    kern = functools.partial(_dil_kernel, seq=seq, scale=hd ** -0.5)
    return pl.pallas_call(
        kern,
        grid=(batch, DIL_HEADS),
        in_specs=[
            pl.BlockSpec(memory_space=pltpu.SMEM),
            pl.BlockSpec((seq, hd), lambda b, h: (b, cq + h)),
            pl.BlockSpec((seq, hd), lambda b, h: (b, ck + h)),
            pl.BlockSpec((seq, hd), lambda b, h: (b, cv + h)),
        ],
        out_specs=pl.BlockSpec((seq, hd), lambda b, h: (b, h)),
        out_shape=jax.ShapeDtypeStruct((batch * seq, DIL_HEADS * hd), BF16),
        scratch_shapes=([pltpu.VMEM((seq, hd), F32)] * (3 * len(DIL_BRANCHES))
                        + [pltpu.VMEM((len(c[6]), c[3], c[4]), F32) for c in _dil_cfgs(seq)]),
        compiler_params=_params(2),
        name="dil",
    )(slopes, proj, proj, proj)


def _conv_kernel(prev_ref, cur_ref, next_ref, w_ref, cb_ref, g_ref, b_ref, o_ref, u_ref, *, halo, taps):
    i = pl.program_id(1)
    c = o_ref.shape[-1]
    tq = o_ref.shape[0]

    def glu(ref):
        x = ref[...]
        return x[:, :c] * jax.nn.sigmoid(x[:, c:])

    u_ref[0:halo, :] = jnp.where(i > 0, glu(prev_ref), 0.0)
    u_ref[halo:halo + tq, :] = glu(cur_ref)
    u_ref[halo + tq:, :] = jnp.where(i < pl.num_programs(1) - 1, glu(next_ref), 0.0)

    pad = taps // 2
    y = jnp.zeros((tq, c), F32) + cb_ref[...]
    for t in range(taps):
        y = y + u_ref[pl.ds(halo - pad + t, tq), :] * w_ref[t:t + 1, :]
    y = _layer_norm(y, g_ref[...], b_ref[...])
    o_ref[...] = (y * jax.nn.sigmoid(y)).astype(o_ref.dtype)


def _conv(proj, conv_w, conv_b, ln_g, ln_b, layer, *, batch, seq, offs):
    taps, c = conv_w.shape[1:]
    halo = 16
    assert taps // 2 <= halo
    tq = min(512, seq)
    nt = seq // tq
    hb = tq // halo
    ccol = offs["cv"] // (2 * c)
    kern = functools.partial(_conv_kernel, halo=halo, taps=taps)
    return pl.pallas_call(
        kern,
        grid=(batch, nt),
        in_specs=[
            pl.BlockSpec((halo, 2 * c), lambda b, i: (jnp.maximum((b * nt + i) * hb - 1, 0), ccol)),
            pl.BlockSpec((tq, 2 * c), lambda b, i: (b * nt + i, ccol)),
            pl.BlockSpec((halo, 2 * c),
                         lambda b, i: (jnp.minimum((b * nt + i + 1) * hb, batch * nt * hb - 1), ccol)),
            _layer_spec((taps, c), layer),
            _layer_spec((1, c), layer),
            _layer_spec((1, c), layer),
            _layer_spec((1, c), layer),
        ],
        out_specs=pl.BlockSpec((tq, c), lambda b, i: (b * nt + i, 0)),
        out_shape=jax.ShapeDtypeStruct((batch * seq, c), BF16),
        scratch_shapes=[pltpu.VMEM((tq + 2 * halo, c), F32)],
        compiler_params=_params(2),
        name="conv",
    )(proj, proj, proj, conv_w, conv_b, ln_g, ln_b)


def _out_ln_kernel(x_ref, a_ref, d_ref, c_ref, w_ref, g_ref, b_ref, o_ref, *, alpha):
    na, nd = a_ref.shape[1], d_ref.shape[1]
    mix = jnp.dot(a_ref[...], w_ref[0:na, :], preferred_element_type=F32)
    mix += jnp.dot(d_ref[...], w_ref[na:na + nd, :], preferred_element_type=F32)
    mix += jnp.dot(c_ref[...], w_ref[na + nd:, :], preferred_element_type=F32)
    o_ref[...] = _layer_norm(alpha * x_ref[...] + mix, g_ref[...], b_ref[...])


def _out_ln(x, gla_o, dil_o, conv_o, w, g, b, layer, *, alpha, tm):
    m, d = x.shape
    row = lambda i: (i, 0)
    return pl.pallas_call(
        functools.partial(_out_ln_kernel, alpha=alpha),
        grid=(m // tm,),
        in_specs=[
            pl.BlockSpec((tm, d), row),
            pl.BlockSpec((tm, gla_o.shape[1]), row),
            pl.BlockSpec((tm, dil_o.shape[1]), row),
            pl.BlockSpec((tm, conv_o.shape[1]), row),
            _layer_spec(w.shape[1:], layer),
            _layer_spec((1, d), layer),
            _layer_spec((1, d), layer),
        ],
        out_specs=pl.BlockSpec((tm, d), row),
        out_shape=jax.ShapeDtypeStruct((m, d), F32),
        compiler_params=_params(1),
        name="out_ln",
    )(x, gla_o, dil_o, conv_o, w, g, b)


def _pad_heads(w, heads, width):
    lead = w.shape[:-1]
    hd = w.shape[-1] // heads
    w = w.reshape(lead + (heads, hd))
    w = jnp.pad(w, [(0, 0)] * len(lead) + [(0, 0), (0, width - hd)])
    return w.reshape(lead + (heads * width,))


def kernel(x, ffn1_w_gate, ffn1_w_up, ffn1_w_down, ln1_g, ln1_b, w_in, gla_decay_w_fwd, gla_decay_b_fwd, gla_decay_w_bwd, gla_decay_b_bwd, gla_norm_g, conv_w, conv_b, conv_ln_g, conv_ln_b, w_out, ln2_g, ln2_b, ffn2_w_gate, ffn2_w_up, ffn2_w_down, ln3_g, ln3_b):
    batch, seq, d_model = x.shape
    depth = ffn1_w_gate.shape[0]
    rank, gla_dk = gla_decay_w_fwd.shape[1:]
    gla_dv = gla_norm_g.shape[1]
    conv_c = conv_w.shape[2]
    dil_dim = w_out.shape[1] - gla_dv - conv_c
    dil_hd = dil_dim // DIL_HEADS
    hdk, hdv = gla_dk // GLA_HEADS, gla_dv // GLA_HEADS
    hk, hv = _round_up(hdk, LANES), _round_up(hdv, LANES)
    assert dil_hd == LANES and 2 * rank <= LANES and seq % GLA_CHUNK == 0

    alpha = (2.0 * depth) ** 0.25
    m = batch * seq
    tm = min(512, m)
    tn = 2 * MXU_WIDTH

    widths = (("gq", GLA_HEADS * hk), ("gk", GLA_HEADS * hk), ("gv", GLA_HEADS * hv), ("gg", GLA_HEADS * hv),
              ("cv", 2 * conv_c), ("dq", dil_dim), ("dk", dil_dim), ("dv", dil_dim), ("r", LANES))
    offs, n_proj = {}, 0
    for name, wdt in widths:
        offs[name] = n_proj
        n_proj += wdt
    assert offs["cv"] % (2 * conv_c) == 0
    n_proj_pad = _round_up(n_proj, tn)

    in_sizes = (gla_dk, gla_dk, gla_dv, rank, rank, gla_dv, dil_dim, dil_dim, dil_dim, 2 * conv_c)
    cuts = [int(c) for c in np.cumsum(in_sizes)[:-1]]
    gq, gk, gv, rf, rb, gg, dq, dk, dv, cv = jnp.split(w_in, cuts, axis=2)
    w_in_p = jnp.concatenate(
        [_pad_heads(gq, GLA_HEADS, hk), _pad_heads(gk, GLA_HEADS, hk), _pad_heads(gv, GLA_HEADS, hv),
         _pad_heads(gg, GLA_HEADS, hv), cv, dq, dk, dv, rf, rb,
         jnp.zeros((depth, d_model, n_proj_pad - n_proj + LANES - 2 * rank), F32)], axis=2).astype(BF16)

    def head_mats(wdec, row0):
        w4 = _pad_heads(wdec, GLA_HEADS, hk).reshape(depth, rank, GLA_HEADS, hk).transpose(0, 2, 1, 3)
        return jnp.pad(w4, ((0, 0), (0, 0), (row0, LANES - rank - row0), (0, 0))).astype(BF16)

    wdec_f, wdec_b = head_mats(gla_decay_w_fwd, 0), head_mats(gla_decay_w_bwd, rank)
    bdec_f = _pad_heads(gla_decay_b_fwd, GLA_HEADS, hk).reshape(depth, GLA_HEADS, 1, hk)
    bdec_b = _pad_heads(gla_decay_b_bwd, GLA_HEADS, hk).reshape(depth, GLA_HEADS, 1, hk)
    norm_g = _pad_heads(gla_norm_g, GLA_HEADS, hv).reshape(depth, GLA_HEADS, 1, hv)

    w_gla = jnp.pad(w_out[:, :gla_dv].reshape(depth, GLA_HEADS, hdv, d_model), ((0, 0), (0, 0), (0, hv - hdv), (0, 0)))
    w_out_p = jnp.concatenate([w_gla.reshape(depth, GLA_HEADS * hv, d_model), w_out[:, gla_dv:]], axis=1).astype(BF16)

    ffn1 = [w.astype(BF16) for w in (ffn1_w_gate, ffn1_w_up, ffn1_w_down)]
    ffn2 = [w.astype(BF16) for w in (ffn2_w_gate, ffn2_w_up, ffn2_w_down)]
    vec = lambda p: p[:, None, :]
    slopes = jnp.asarray(2.0 ** (-8.0 * np.arange(1, DIL_HEADS + 1) / DIL_HEADS), F32)

    xf = x.reshape(m, d_model)
    for l in range(depth):
        xf = _ffn_ln(xf, *ffn1, vec(ln1_g), vec(ln1_b), l, alpha=alpha, tm=tm)
        proj = _in_proj(xf, w_in_p, l, tm=min(2 * tm, m), tn=tn)
        gla_o = _gla(proj, wdec_f, wdec_b, bdec_f, bdec_b, norm_g, l, batch=batch, seq=seq, offs=offs,
                     hk=hk, hv=hv, q_scale=hdk ** -0.5, dv_head=hdv)
        dil_o = _dil(proj, slopes, batch=batch, seq=seq, offs=offs, hd=dil_hd)
        conv_o = _conv(proj, conv_w, vec(conv_b), vec(conv_ln_g), vec(conv_ln_b), l, batch=batch, seq=seq, offs=offs)
        xf = _out_ln(xf, gla_o, dil_o, conv_o, w_out_p, vec(ln2_g), vec(ln2_b), l, alpha=alpha, tm=tm)
        xf = _ffn_ln(xf, *ffn2, vec(ln3_g), vec(ln3_b), l, alpha=alpha, tm=tm)
    return xf.reshape(batch, seq, d_model)
```

```python
import functools

import numpy as np
import jax
import jax.numpy as jnp
from jax import lax
from jax.experimental import pallas as pl
from jax.experimental.pallas import tpu as pltpu

GLA_HEADS = 4
GLA_TAU = 16.0
GLA_CHUNK = 64
DIL_HEADS = 6
DIL_BRANCHES = ((128, 1), (512, 4), (2048, 16))
LN_EPS = 1e-5
MASK_VALUE = -1e30

LANES = 128
SUBLANES = 8
MXU_WIDTH = 256
VMEM_LIMIT_BYTES = 56 * 1024 * 1024

F32 = jnp.float32
BF16 = jnp.bfloat16
_NT_DIMS = (((1,), (1,)), ((), ()))
_TN_DIMS = (((0,), (0,)), ((), ()))


def _round_up(n, m):
    return (n + m - 1) // m * m


def _layer_norm(y, g, b):
    mu = jnp.mean(y, axis=-1, keepdims=True)
    yc = y - mu
    var = jnp.mean(yc * yc, axis=-1, keepdims=True)
    return yc * lax.rsqrt(var + LN_EPS) * g + b


def _params(n_grid_axes):
    return pltpu.CompilerParams(dimension_semantics=("arbitrary",) * n_grid_axes,
                                vmem_limit_bytes=VMEM_LIMIT_BYTES)


def _layer_spec(shape, layer):
    zeros = (0,) * len(shape)
    return pl.BlockSpec((None,) + tuple(shape), lambda *_: (layer,) + zeros)


def _swiglu_part(xb, wg, wu, wd):
    gate = jnp.dot(xb, wg, preferred_element_type=F32)
    up = jnp.dot(xb, wu, preferred_element_type=F32)
    h = (gate * jax.nn.sigmoid(gate) * up).astype(BF16)
    return jnp.dot(h, wd, preferred_element_type=F32)


def _ffn_ln_kernel(x_ref, wg_ref, wu_ref, wd_ref, wgt_ref, wut_ref, wdt_ref, g_ref, b_ref, o_ref, xb_ref, *, alpha):
    f = pl.program_id(1)

    @pl.when(f == 0)
    def _():
        xb = x_ref[...].astype(BF16)
        xb_ref[...] = xb
        o_ref[...] = _swiglu_part(xb, wgt_ref[...], wut_ref[...], wdt_ref[...])

    o_ref[...] += _swiglu_part(xb_ref[...], wg_ref[...], wu_ref[...], wd_ref[...])

    @pl.when(f == pl.num_programs(1) - 1)
    def _():
        y = alpha * x_ref[...] + 0.5 * o_ref[...]
        o_ref[...] = _layer_norm(y, g_ref[...], b_ref[...])


def _ffn_tiles(d_ff):
    for tf in (3 * MXU_WIDTH, 2 * MXU_WIDTH, MXU_WIDTH):
        tail = d_ff % tf
        if tail and tail % LANES == 0 and (d_ff - tail) % tail == 0:
            return tf, tail
    raise NotImplementedError(f"no FFN tiling for d_ff={d_ff}")


def _ffn_ln(x, wg, wu, wd, g, b, layer, *, alpha, tm):
    m, d = x.shape
    d_ff = wg.shape[2]
    tf, tail = _ffn_tiles(d_ff)
    n_f = d_ff // tf
    tail_blk = (d_ff - tail) // tail
    return pl.pallas_call(
        functools.partial(_ffn_ln_kernel, alpha=alpha),
        grid=(m // tm, n_f),
        in_specs=[
            pl.BlockSpec((tm, d), lambda i, f: (i, 0)),
            pl.BlockSpec((None, d, tf), lambda i, f: (layer, 0, f)),
            pl.BlockSpec((None, d, tf), lambda i, f: (layer, 0, f)),
            pl.BlockSpec((None, tf, d), lambda i, f: (layer, f, 0)),
            pl.BlockSpec((None, d, tail), lambda i, f: (layer, 0, tail_blk)),
            pl.BlockSpec((None, d, tail), lambda i, f: (layer, 0, tail_blk)),
            pl.BlockSpec((None, tail, d), lambda i, f: (layer, tail_blk, 0)),
            _layer_spec((1, d), layer),
            _layer_spec((1, d), layer),
        ],
        out_specs=pl.BlockSpec((tm, d), lambda i, f: (i, 0)),
        out_shape=jax.ShapeDtypeStruct((m, d), F32),
        scratch_shapes=[pltpu.VMEM((tm, d), BF16)],
        compiler_params=_params(2),
        name="ffn_ln",
    )(x, wg, wu, wd, wg, wu, wd, g, b)


def _in_proj_kernel(x_ref, w_ref, o_ref, xb_ref):
    @pl.when(pl.program_id(1) == 0)
    def _():
        xb_ref[...] = x_ref[...].astype(BF16)

    o_ref[...] = jnp.dot(xb_ref[...], w_ref[...], preferred_element_type=F32)


def _in_proj(x, w, layer, *, tm, tn):
    m, d = x.shape
    n = w.shape[2]
    return pl.pallas_call(
        _in_proj_kernel,
        grid=(m // tm, n // tn),
        in_specs=[
            pl.BlockSpec((tm, d), lambda i, j: (i, 0)),
            pl.BlockSpec((None, d, tn), lambda i, j: (layer, 0, j)),
        ],
        out_specs=pl.BlockSpec((tm, tn), lambda i, j: (i, j)),
        out_shape=jax.ShapeDtypeStruct((m, n), F32),
        scratch_shapes=[pltpu.VMEM((tm, d), BF16)],
        compiler_params=_params(2),
        name="in_proj",
    )(x, w)


def _log_sigmoid(z):
    return jnp.minimum(z, 0.0) - jnp.log(1.0 + jnp.exp(-jnp.abs(z)))


def _gla_kernel(q_ref, k_ref, v_ref, gg_ref, r_ref, wf_ref, wb_ref, bf_ref, bb_ref, ng_ref,
                o_ref, of_ref, ob_ref, stf_ref, stb_ref, *, seq, tile, chunk, q_scale, inv_dv):
    n_tiles = seq // tile
    n_chunks = tile // chunk
    hk = q_ref.shape[-1]
    shift = chunk.bit_length() - 1
    row = lax.broadcasted_iota(jnp.int32, (tile, tile), 0)
    col = lax.broadcasted_iota(jnp.int32, (tile, tile), 1)
    same_chunk = lax.shift_right_logical(row, shift) == lax.shift_right_logical(col, shift)

    tris = (same_chunk & (col <= row), same_chunk & (col >= row))
    st_refs, oacc_refs = (stf_ref, stb_ref), (of_ref, ob_ref)
    ws, biases = (wf_ref[0], wb_ref[0]), (bf_ref[0], bb_ref[0])
    both = (0, 1)

    def pair_step(i):
        rows = [pl.ds(pl.multiple_of(ti * tile, tile), tile) for ti in (i, n_tiles - 1 - i)]
        z = [jnp.dot(r_ref[rows[j], :].astype(BF16), ws[j], preferred_element_type=F32) + biases[j]
             for j in both]
        la = [_log_sigmoid(z[j]) * (1.0 / GLA_TAU) for j in both]
        la_hi = [la[j].astype(BF16) for j in both]
        la_lo = [(la[j] - la_hi[j].astype(F32)).astype(BF16) for j in both]
        cum = [jnp.dot(tris[j].astype(BF16), jnp.concatenate([la_hi[j], la_lo[j]], axis=1),
                       preferred_element_type=F32) for j in both]
        bcum = [cum[j][:, :hk] + cum[j][:, hk:] for j in both]

        vb = [v_ref[rows[j], :].astype(BF16) for j in both]
        q_dec = [(q_ref[rows[j], :] * q_scale * jnp.exp(bcum[j])).astype(BF16) for j in both]
        k_inv = [k_ref[rows[j], :] * jnp.exp(-bcum[j]) for j in both]
        att = [lax.dot_general(q_dec[j], k_inv[j].astype(BF16), _NT_DIMS, preferred_element_type=F32)
               for j in both]
        att = [jnp.where(tris[j], att[j], 0.0).astype(BF16) for j in both]
        o_intra = [jnp.dot(att[j], vb[j], preferred_element_type=F32) for j in both]

        o_inter = [[None] * n_chunks for _ in both]
        for s in range(n_chunks):
            for j in both:
                c = s if j == 0 else n_chunks - 1 - s
                lo = c * chunk
                end_row = lo + chunk - 1 if j == 0 else lo
                e_end = jnp.exp(bcum[j][end_row:end_row + 1, :])
                st = st_refs[j][...]
                o_inter[j][c] = lax.dot_general(q_dec[j][lo:lo + chunk], st.astype(BF16), _NT_DIMS,
                                                preferred_element_type=F32)
                k_end = (k_inv[j][lo:lo + chunk] * e_end).astype(BF16)
                upd = lax.dot_general(vb[j][lo:lo + chunk], k_end, _TN_DIMS, preferred_element_type=F32)
                st_refs[j][...] = st * e_end + upd
        for j in both:
            oacc_refs[j][rows[j], :] = o_intra[j] + jnp.concatenate(o_inter[j], axis=0)

    stf_ref[...] = jnp.zeros_like(stf_ref)
    stb_ref[...] = jnp.zeros_like(stb_ref)

    def body(i, carry):
        pair_step(i)
        return carry

    lax.fori_loop(0, n_tiles, body, 0)

    def finish(i, carry):
        rows = pl.ds(pl.multiple_of(i * tile, tile), tile)
        o = of_ref[rows, :] + ob_ref[rows, :]
        ms = jnp.sum(o * o, axis=-1, keepdims=True) * inv_dv
        gate = gg_ref[rows, :]
        o = o * lax.rsqrt(ms + LN_EPS) * ng_ref[0] * (gate * jax.nn.sigmoid(gate))
        o_ref[rows, :] = o.astype(o_ref.dtype)
        return carry

    lax.fori_loop(0, n_tiles, finish, 0)


def _gla(proj, wdec_f, wdec_b, bdec_f, bdec_b, norm_g, layer, *, batch, seq, offs, hk, hv, q_scale, dv_head):
    heads = wdec_f.shape[1]
    tile = min(256, seq)
    kern = functools.partial(_gla_kernel, seq=seq, tile=tile, chunk=GLA_CHUNK, q_scale=q_scale,
                             inv_dv=1.0 / dv_head)
    cq, ck, cv, cg, cr = (offs["gq"] // hk, offs["gk"] // hk, offs["gv"] // hv, offs["gg"] // hv,
                          offs["r"] // LANES)
    per_head = lambda rows, cols: pl.BlockSpec((None, 1, rows, cols), lambda b, h: (layer, h, 0, 0))
    return pl.pallas_call(
        kern,
        grid=(batch, heads),
        in_specs=[
            pl.BlockSpec((seq, hk), lambda b, h: (b, cq + h)),
            pl.BlockSpec((seq, hk), lambda b, h: (b, ck + h)),
            pl.BlockSpec((seq, hv), lambda b, h: (b, cv + h)),
            pl.BlockSpec((seq, hv), lambda b, h: (b, cg + h)),
            pl.BlockSpec((seq, LANES), lambda b, h: (b, cr)),
            per_head(LANES, hk), per_head(LANES, hk),
            per_head(1, hk), per_head(1, hk), per_head(1, hv),
        ],
        out_specs=pl.BlockSpec((seq, hv), lambda b, h: (b, h)),
        out_shape=jax.ShapeDtypeStruct((batch * seq, heads * hv), BF16),
        scratch_shapes=[pltpu.VMEM((seq, hv), F32), pltpu.VMEM((seq, hv), F32),
                        pltpu.VMEM((hv, hk), F32), pltpu.VMEM((hv, hk), F32)],
        compiler_params=_params(2),
        name="gla",
    )(proj, proj, proj, proj, proj, wdec_f, wdec_b, bdec_f, bdec_b, norm_g)


def _dil_cfgs(seq):
    cfgs = []
    for w, d in DIL_BRANCHES:
        reach = (w // 2) // d
        length = seq // d
        qb = min(128, length)
        kb = min(length, qb + 2 * reach)
        n_qb = length // qb
        rel_starts = sorted({min(max(jb * qb - reach, 0), length - kb) - jb * qb for jb in range(n_qb)},
                            reverse=True)
        cfgs.append((d, reach, length, qb, kb, n_qb, rel_starts))
    return cfgs


def _dil_kernel(slopes_ref, q_ref, k_ref, v_ref, o_ref, *stats, seq, scale):
    n_br = len(DIL_BRANCHES)
    m_refs, l_refs, acc_refs, bias_refs = (stats[i * n_br:(i + 1) * n_br] for i in range(4))
    slope = slopes_ref[pl.program_id(1)]
    hd = q_ref.shape[-1]

    cfgs = _dil_cfgs(seq)
    for n, (d, reach, length, qb, kb, n_qb, rel_starts) in enumerate(cfgs):
        rel = lax.broadcasted_iota(jnp.int32, (qb, kb), 1) - lax.broadcasted_iota(jnp.int32, (qb, kb), 0)
        for vi, rs in enumerate(rel_starts):
            dist = jnp.abs(rel + rs)
            bias_refs[n][vi] = jnp.where(dist <= reach, -(slope * float(d)) * dist.astype(F32), MASK_VALUE)

    def blocks(jobs):
        place = []
        for n, idx in jobs:
            d, reach, length, qb, kb, n_qb, rel_starts = cfgs[n]
            if d == 1:
                r, jb = 0, idx
            else:
                r, jb = lax.div(idx, jnp.int32(n_qb)), lax.rem(idx, jnp.int32(n_qb))
            i0 = jb * qb
            ks = jnp.clip(i0 - reach, 0, length - kb)
            variant = sum(((ks - i0) <= rs).astype(jnp.int32) for rs in rel_starts[1:])
            if d == 1:
                qrows = pl.ds(pl.multiple_of(i0, qb), qb)
                krows = pl.ds(pl.multiple_of(ks, 8), kb)
            else:
                qrows = pl.ds(r + d * i0, qb, stride=d)
                krows = pl.ds(r + d * ks, kb, stride=d)
            place.append((n, qrows, krows, variant, qb, kb))
        s = []
        for n, qrows, krows, variant, qb, kb in place:
            q = (q_ref[qrows, :] * scale).astype(BF16)
            k = k_ref[krows, :].astype(BF16)
            s.append(lax.dot_general(q, k, _NT_DIMS, preferred_element_type=F32) + bias_refs[n][variant])
        m = [jnp.max(sj, axis=-1, keepdims=True) for sj in s]
        p = [jnp.exp(sj - mj).astype(BF16) for sj, mj in zip(s, m)]
        for (n, qrows, krows, variant, qb, kb), mj, pj in zip(place, m, p):
            v_ones = jnp.concatenate([v_ref[krows, :].astype(BF16), jnp.ones((kb, hd), BF16)], axis=1)
            acc_l = jnp.dot(pj, v_ones, preferred_element_type=F32)
            acc_refs[n][qrows, :] = acc_l[:, :hd]
            l_refs[n][qrows, :] = acc_l[:, hd:]
            m_refs[n][qrows, :] = jnp.broadcast_to(mj, (qb, hd))

    counts = [seq // min(128, seq // d) for _, d in DIL_BRANCHES]
    group = 4
    if len(set(counts)) == 1 and counts[0] % group == 0:
        def body(i, carry):
            blocks([(n, group * i + u) for u in range(group) for n in range(n_br)])
            return carry
        lax.fori_loop(0, counts[0] // group, body, 0)
    else:
        for n in range(n_br):
            def body(idx, carry, n=n):
                blocks([(n, idx)])
                return carry
            lax.fori_loop(0, counts[n], body, 0)

    rt = min(256, seq)

    def finish(i, carry):
        rows = pl.ds(pl.multiple_of(i * rt, rt), rt)
        ms = [m_ref[rows, :] for m_ref in m_refs]
        m_all = functools.reduce(jnp.maximum, ms)
        num = jnp.zeros((rt, hd), F32)
        den = jnp.zeros((rt, hd), F32)
        for n in range(n_br):
            wgt = jnp.exp(ms[n] - m_all)
            num += wgt * acc_refs[n][rows, :]
            den += wgt * l_refs[n][rows, :]
        o_ref[rows, :] = (num / den).astype(o_ref.dtype)
        return carry

    lax.fori_loop(0, seq // rt, finish, 0)


def _dil(proj, slopes, *, batch, seq, offs, hd):
    cq, ck, cv = offs["dq"] // hd, offs["dk"] // hd, offs["dv"] // hd
    kern = functools.partial(_dil_kernel, seq=seq, scale=hd ** -0.5)
    return pl.pallas_call(
        kern,
        grid=(batch, DIL_HEADS),
        in_specs=[
            pl.BlockSpec(memory_space=pltpu.SMEM),
            pl.BlockSpec((seq, hd), lambda b, h: (b, cq + h)),
            pl.BlockSpec((seq, hd), lambda b, h: (b, ck + h)),
            pl.BlockSpec((seq, hd), lambda b, h: (b, cv + h)),
        ],
        out_specs=pl.BlockSpec((seq, hd), lambda b, h: (b, h)),
        out_shape=jax.ShapeDtypeStruct((batch * seq, DIL_HEADS * hd), BF16),
        scratch_shapes=([pltpu.VMEM((seq, hd), F32)] * (3 * len(DIL_BRANCHES))
                        + [pltpu.VMEM((len(c[6]), c[3], c[4]), F32) for c in _dil_cfgs(seq)]),
        compiler_params=_params(2),
        name="dil",
    )(slopes, proj, proj, proj)


def _conv_kernel(prev_ref, cur_ref, next_ref, w_ref, cb_ref, g_ref, b_ref, o_ref, u_ref, *, halo, taps):
    i = pl.program_id(1)
    c = o_ref.shape[-1]
    tq = o_ref.shape[0]

    def glu(ref):
        x = ref[...]
        return x[:, :c] * jax.nn.sigmoid(x[:, c:])

    ext = tq + 2 * halo
    u_ref[0, 0:halo, :] = jnp.where(i > 0, glu(prev_ref), 0.0)
    u_ref[0, halo:halo + tq, :] = glu(cur_ref)
    u_ref[0, halo + tq:, :] = jnp.where(i < pl.num_programs(1) - 1, glu(next_ref), 0.0)
    for s in range(1, SUBLANES):
        u_ref[s, 0:ext - SUBLANES, :] = u_ref[0, pl.ds(s, ext - SUBLANES), :]

    pad = taps // 2
    rc = min(32, tq)
    first = halo - pad
    span = (first + taps - 1) // SUBLANES * SUBLANES

    def chunk(j, carry):
        r0 = pl.multiple_of(j * rc, rc)
        acc = jnp.zeros((rc, c), F32) + cb_ref[...]
        for s in range(SUBLANES):
            win = u_ref[s, pl.ds(r0, rc + span), :]
            for t in range(taps):
                a, ts = divmod(first + t, SUBLANES)
                if ts == s:
                    acc = acc + win[a * SUBLANES:a * SUBLANES + rc] * w_ref[t:t + 1, :]
        y = _layer_norm(acc, g_ref[...], b_ref[...])
        o_ref[pl.ds(r0, rc), :] = (y * jax.nn.sigmoid(y)).astype(o_ref.dtype)
        return carry

    lax.fori_loop(0, tq // rc, chunk, 0, unroll=2)


def _conv(proj, conv_w, conv_b, ln_g, ln_b, layer, *, batch, seq, offs):
    taps, c = conv_w.shape[1:]
    halo = 16
    assert taps // 2 <= halo and halo % SUBLANES == 0
    tq = min(512, seq)
    nt = seq // tq
    hb = tq // halo
    ccol = offs["cv"] // (2 * c)
    kern = functools.partial(_conv_kernel, halo=halo, taps=taps)
    return pl.pallas_call(
        kern,
        grid=(batch, nt),
        in_specs=[
            pl.BlockSpec((halo, 2 * c), lambda b, i: (jnp.maximum((b * nt + i) * hb - 1, 0), ccol)),
            pl.BlockSpec((tq, 2 * c), lambda b, i: (b * nt + i, ccol)),
            pl.BlockSpec((halo, 2 * c),
                         lambda b, i: (jnp.minimum((b * nt + i + 1) * hb, batch * nt * hb - 1), ccol)),
            _layer_spec((taps, c), layer),
            _layer_spec((1, c), layer),
            _layer_spec((1, c), layer),
            _layer_spec((1, c), layer),
        ],
        out_specs=pl.BlockSpec((tq, c), lambda b, i: (b * nt + i, 0)),
        out_shape=jax.ShapeDtypeStruct((batch * seq, c), BF16),
        scratch_shapes=[pltpu.VMEM((SUBLANES, tq + 2 * halo, c), F32)],
        compiler_params=_params(2),
        name="conv",
    )(proj, proj, proj, conv_w, conv_b, ln_g, ln_b)


def _out_ln_kernel(x_ref, a_ref, d_ref, c_ref, w_ref, g_ref, b_ref, o_ref, *, alpha):
    na, nd = a_ref.shape[1], d_ref.shape[1]
    mix = jnp.dot(a_ref[...], w_ref[0:na, :], preferred_element_type=F32)
    mix += jnp.dot(d_ref[...], w_ref[na:na + nd, :], preferred_element_type=F32)
    mix += jnp.dot(c_ref[...], w_ref[na + nd:, :], preferred_element_type=F32)
    o_ref[...] = _layer_norm(alpha * x_ref[...] + mix, g_ref[...], b_ref[...])


def _out_ln(x, gla_o, dil_o, conv_o, w, g, b, layer, *, alpha, tm):
    m, d = x.shape
    row = lambda i: (i, 0)
    return pl.pallas_call(
        functools.partial(_out_ln_kernel, alpha=alpha),
        grid=(m // tm,),
        in_specs=[
            pl.BlockSpec((tm, d), row),
            pl.BlockSpec((tm, gla_o.shape[1]), row),
            pl.BlockSpec((tm, dil_o.shape[1]), row),
            pl.BlockSpec((tm, conv_o.shape[1]), row),
            _layer_spec(w.shape[1:], layer),
            _layer_spec((1, d), layer),
            _layer_spec((1, d), layer),
        ],
        out_specs=pl.BlockSpec((tm, d), row),
        out_shape=jax.ShapeDtypeStruct((m, d), F32),
        compiler_params=_params(1),
        name="out_ln",
    )(x, gla_o, dil_o, conv_o, w, g, b)


def _pad_heads(w, heads, width):
    lead = w.shape[:-1]
    hd = w.shape[-1] // heads
    w = w.reshape(lead + (heads, hd))
    w = jnp.pad(w, [(0, 0)] * len(lead) + [(0, 0), (0, width - hd)])
    return w.reshape(lead + (heads * width,))


def kernel(x, ffn1_w_gate, ffn1_w_up, ffn1_w_down, ln1_g, ln1_b, w_in, gla_decay_w_fwd, gla_decay_b_fwd, gla_decay_w_bwd, gla_decay_b_bwd, gla_norm_g, conv_w, conv_b, conv_ln_g, conv_ln_b, w_out, ln2_g, ln2_b, ffn2_w_gate, ffn2_w_up, ffn2_w_down, ln3_g, ln3_b):
    batch, seq, d_model = x.shape
    depth = ffn1_w_gate.shape[0]
    rank, gla_dk = gla_decay_w_fwd.shape[1:]
    gla_dv = gla_norm_g.shape[1]
    conv_c = conv_w.shape[2]
    dil_dim = w_out.shape[1] - gla_dv - conv_c
    dil_hd = dil_dim // DIL_HEADS
    hdk, hdv = gla_dk // GLA_HEADS, gla_dv // GLA_HEADS
    hk, hv = _round_up(hdk, LANES), _round_up(hdv, LANES)
    assert dil_hd == LANES and 2 * rank <= LANES and seq % GLA_CHUNK == 0

    alpha = (2.0 * depth) ** 0.25
    m = batch * seq
    tm = min(512, m)

    widths = (("gq", GLA_HEADS * hk), ("gk", GLA_HEADS * hk), ("gv", GLA_HEADS * hv), ("gg", GLA_HEADS * hv),
              ("cv", 2 * conv_c), ("dq", dil_dim), ("dk", dil_dim), ("dv", dil_dim), ("r", LANES))
    offs, n_proj = {}, 0
    for name, wdt in widths:
        offs[name] = n_proj
        n_proj += wdt
    assert offs["cv"] % (2 * conv_c) == 0
    n_proj_pad = _round_up(n_proj, 2 * MXU_WIDTH)
    tn = n_proj_pad // 2

    in_sizes = (gla_dk, gla_dk, gla_dv, rank, rank, gla_dv, dil_dim, dil_dim, dil_dim, 2 * conv_c)
    cuts = [int(c) for c in np.cumsum(in_sizes)[:-1]]
    gq, gk, gv, rf, rb, gg, dq, dk, dv, cv = jnp.split(w_in.astype(BF16), cuts, axis=2)
    w_in_p = jnp.concatenate(
        [_pad_heads(gq, GLA_HEADS, hk), _pad_heads(gk, GLA_HEADS, hk), _pad_heads(gv, GLA_HEADS, hv),
         _pad_heads(gg, GLA_HEADS, hv), cv, dq, dk, dv, rf, rb,
         jnp.zeros((depth, d_model, n_proj_pad - n_proj + LANES - 2 * rank), BF16)], axis=2)

    def head_mats(wdec, row0):
        w4 = _pad_heads(wdec, GLA_HEADS, hk).reshape(depth, rank, GLA_HEADS, hk).transpose(0, 2, 1, 3)
        return jnp.pad(w4, ((0, 0), (0, 0), (row0, LANES - rank - row0), (0, 0))).astype(BF16)

    wdec_f, wdec_b = head_mats(gla_decay_w_fwd, 0), head_mats(gla_decay_w_bwd, rank)
    bdec_f = _pad_heads(gla_decay_b_fwd, GLA_HEADS, hk).reshape(depth, GLA_HEADS, 1, hk)
    bdec_b = _pad_heads(gla_decay_b_bwd, GLA_HEADS, hk).reshape(depth, GLA_HEADS, 1, hk)
    norm_g = _pad_heads(gla_norm_g, GLA_HEADS, hv).reshape(depth, GLA_HEADS, 1, hv)

    w_out_b = w_out.astype(BF16)
    w_gla = jnp.pad(w_out_b[:, :gla_dv].reshape(depth, GLA_HEADS, hdv, d_model), ((0, 0), (0, 0), (0, hv - hdv), (0, 0)))
    w_out_p = jnp.concatenate([w_gla.reshape(depth, GLA_HEADS * hv, d_model), w_out_b[:, gla_dv:]], axis=1)

    ffn1 = [w.astype(BF16) for w in (ffn1_w_gate, ffn1_w_up, ffn1_w_down)]
    ffn2 = [w.astype(BF16) for w in (ffn2_w_gate, ffn2_w_up, ffn2_w_down)]
    vec = lambda p: p[:, None, :]
    slopes = jnp.asarray(2.0 ** (-8.0 * np.arange(1, DIL_HEADS + 1) / DIL_HEADS), F32)

    xf = x.reshape(m, d_model)
    for l in range(depth):
        xf = _ffn_ln(xf, *ffn1, vec(ln1_g), vec(ln1_b), l, alpha=alpha, tm=tm)
        proj = _in_proj(xf, w_in_p, l, tm=tm, tn=tn)
        gla_o = _gla(proj, wdec_f, wdec_b, bdec_f, bdec_b, norm_g, l, batch=batch, seq=seq, offs=offs,
                     hk=hk, hv=hv, q_scale=hdk ** -0.5, dv_head=hdv)
        dil_o = _dil(proj, slopes, batch=batch, seq=seq, offs=offs, hd=dil_hd)
        conv_o = _conv(proj, conv_w, vec(conv_b), vec(conv_ln_g), vec(conv_ln_b), l, batch=batch, seq=seq, offs=offs)
        xf = _out_ln(xf, gla_o, dil_o, conv_o, w_out_p, vec(ln2_g), vec(ln2_b), l, alpha=alpha, tm=tm)
        xf = _ffn_ln(xf, *ffn2, vec(ln3_g), vec(ln3_b), l, alpha=alpha, tm=tm)
    return xf.reshape(batch, seq, d_model)
```

```python
import functools

import numpy as np
import jax
import jax.numpy as jnp
from jax import lax
from jax.experimental import pallas as pl
from jax.experimental.pallas import tpu as pltpu

GLA_HEADS = 4
GLA_TAU = 16.0
GLA_CHUNK = 64
DIL_HEADS = 6
DIL_BRANCHES = ((128, 1), (512, 4), (2048, 16))
LN_EPS = 1e-5
MASK_VALUE = -1e30

LANES = 128
SUBLANES = 8
MXU_WIDTH = 256
VMEM_LIMIT_BYTES = 56 * 1024 * 1024

F32 = jnp.float32
BF16 = jnp.bfloat16
_NT_DIMS = (((1,), (1,)), ((), ()))
_TN_DIMS = (((0,), (0,)), ((), ()))


def _round_up(n, m):
    return (n + m - 1) // m * m


def _layer_norm(y, g, b):
    mu = jnp.mean(y, axis=-1, keepdims=True)
    yc = y - mu
    var = jnp.mean(yc * yc, axis=-1, keepdims=True)
    return yc * lax.rsqrt(var + LN_EPS) * g + b


def _params(n_grid_axes):
    return pltpu.CompilerParams(dimension_semantics=("arbitrary",) * n_grid_axes,
                                vmem_limit_bytes=VMEM_LIMIT_BYTES)


def _layer_spec(shape, layer):
    zeros = (0,) * len(shape)
    return pl.BlockSpec((None,) + tuple(shape), lambda *_: (layer,) + zeros)


def _swiglu_part(xb, wg, wu, wd):
    gate = jnp.dot(xb, wg, preferred_element_type=F32)
    up = jnp.dot(xb, wu, preferred_element_type=F32)
    h = (gate * jax.nn.sigmoid(gate) * up).astype(BF16)
    return jnp.dot(h, wd, preferred_element_type=F32)


def _ffn_ln_kernel(x_ref, wg_ref, wu_ref, wd_ref, wgt_ref, wut_ref, wdt_ref, g_ref, b_ref, o_ref, xb_ref, *, alpha):
    f = pl.program_id(1)

    @pl.when(f == 0)
    def _():
        xb = x_ref[...].astype(BF16)
        xb_ref[...] = xb
        o_ref[...] = _swiglu_part(xb, wgt_ref[...], wut_ref[...], wdt_ref[...])

    o_ref[...] += _swiglu_part(xb_ref[...], wg_ref[...], wu_ref[...], wd_ref[...])

    @pl.when(f == pl.num_programs(1) - 1)
    def _():
        y = alpha * x_ref[...] + 0.5 * o_ref[...]
        o_ref[...] = _layer_norm(y, g_ref[...], b_ref[...])


def _ffn_tiles(d_ff):
    for tf in (3 * MXU_WIDTH, 2 * MXU_WIDTH, MXU_WIDTH):
        tail = d_ff % tf
        if tail and tail % LANES == 0 and (d_ff - tail) % tail == 0:
            return tf, tail
    raise NotImplementedError(f"no FFN tiling for d_ff={d_ff}")


def _ffn_ln(x, wg, wu, wd, g, b, layer, *, alpha, tm):
    m, d = x.shape
    d_ff = wg.shape[2]
    tf, tail = _ffn_tiles(d_ff)
    n_f = d_ff // tf
    tail_blk = (d_ff - tail) // tail
    return pl.pallas_call(
        functools.partial(_ffn_ln_kernel, alpha=alpha),
        grid=(m // tm, n_f),
        in_specs=[
            pl.BlockSpec((tm, d), lambda i, f: (i, 0)),
            pl.BlockSpec((None, d, tf), lambda i, f: (layer, 0, f)),
            pl.BlockSpec((None, d, tf), lambda i, f: (layer, 0, f)),
            pl.BlockSpec((None, tf, d), lambda i, f: (layer, f, 0)),
            pl.BlockSpec((None, d, tail), lambda i, f: (layer, 0, tail_blk)),
            pl.BlockSpec((None, d, tail), lambda i, f: (layer, 0, tail_blk)),
            pl.BlockSpec((None, tail, d), lambda i, f: (layer, tail_blk, 0)),
            _layer_spec((1, d), layer),
            _layer_spec((1, d), layer),
        ],
        out_specs=pl.BlockSpec((tm, d), lambda i, f: (i, 0)),
        out_shape=jax.ShapeDtypeStruct((m, d), F32),
        scratch_shapes=[pltpu.VMEM((tm, d), BF16)],
        compiler_params=_params(2),
        name="ffn_ln",
    )(x, wg, wu, wd, wg, wu, wd, g, b)


def _in_proj_kernel(x_ref, w_ref, o_ref):
    o_ref[...] = jnp.dot(x_ref[...].astype(BF16), w_ref[...], preferred_element_type=F32).astype(o_ref.dtype)


def _in_proj(x, w, layer, out_dtype, *, tm, name):
    m, d = x.shape
    n = w.shape[2]
    return pl.pallas_call(
        _in_proj_kernel,
        grid=(m // tm,),
        in_specs=[
            pl.BlockSpec((tm, d), lambda i: (i, 0)),
            pl.BlockSpec((None, d, n), lambda i: (layer, 0, 0), pipeline_mode=pl.Buffered(1)),
        ],
        out_specs=pl.BlockSpec((tm, n), lambda i: (i, 0)),
        out_shape=jax.ShapeDtypeStruct((m, n), out_dtype),
        compiler_params=_params(1),
        name=name,
    )(x, w)


def _log_sigmoid(z):
    return jnp.minimum(z, 0.0) - jnp.log(1.0 + jnp.exp(-jnp.abs(z)))


def _gla_kernel(q_ref, k_ref, v_ref, gg_ref, r_ref, wf_ref, wb_ref, bf_ref, bb_ref, ng_ref,
                o_ref, of_ref, ob_ref, st_ref, *, seq, tile, chunk, hk, hv, q_scale, inv_dv):
    n_tiles = seq // tile
    n_chunks = tile // chunk
    n_heads = q_ref.shape[-1] // hk
    shift = chunk.bit_length() - 1
    row = lax.broadcasted_iota(jnp.int32, (tile, tile), 0)
    col = lax.broadcasted_iota(jnp.int32, (tile, tile), 1)
    same_chunk = lax.shift_right_logical(row, shift) == lax.shift_right_logical(col, shift)
    tris = (same_chunk & (col <= row), same_chunk & (col >= row))
    oacc_refs = (of_ref, ob_ref)
    w_refs, b_refs = (wf_ref, wb_ref), (bf_ref, bb_ref)
    chains = [(h, dn) for h in range(n_heads) for dn in (0, 1)]
    every = range(len(chains))
    kcols = [slice(h * hk, (h + 1) * hk) for h, _ in chains]
    vcols = [slice(h * hv, (h + 1) * hv) for h, _ in chains]

    def step(i):
        rows = [pl.ds(pl.multiple_of((i if dn == 0 else n_tiles - 1 - i) * tile, tile), tile) for _, dn in chains]
        z = [jnp.dot(r_ref[rows[c], :], w_refs[dn][h], preferred_element_type=F32) + b_refs[dn][h]
             for c, (h, dn) in enumerate(chains)]
        la = [_log_sigmoid(zc) * (1.0 / GLA_TAU) for zc in z]
        la_hi = [x.astype(BF16) for x in la]
        la_lo = [(x - hi.astype(F32)).astype(BF16) for x, hi in zip(la, la_hi)]
        cum = [jnp.dot(tris[dn].astype(BF16), jnp.concatenate([la_hi[c], la_lo[c]], axis=1),
                       preferred_element_type=F32) for c, (_, dn) in enumerate(chains)]
        bcum = [x[:, :hk] + x[:, hk:] for x in cum]

        vb = [v_ref[rows[c], vcols[c]] for c in every]
        q_dec = [(q_ref[rows[c], kcols[c]].astype(F32) * q_scale * jnp.exp(bcum[c])).astype(BF16) for c in every]
        k_inv = [k_ref[rows[c], kcols[c]].astype(F32) * jnp.exp(-bcum[c]) for c in every]
        att = [lax.dot_general(q_dec[c], k_inv[c].astype(BF16), _NT_DIMS, preferred_element_type=F32) for c in every]
        att = [jnp.where(tris[dn], att[c], 0.0).astype(BF16) for c, (_, dn) in enumerate(chains)]
        o_intra = [jnp.dot(att[c], vb[c], preferred_element_type=F32) for c in every]

        o_inter = [[None] * n_chunks for _ in every]
        for s in range(n_chunks):
            for c, (_, dn) in enumerate(chains):
                ck = s if dn == 0 else n_chunks - 1 - s
                lo = ck * chunk
                end_row = lo + chunk - 1 if dn == 0 else lo
                e_end = jnp.exp(bcum[c][end_row:end_row + 1, :])
                st = st_ref[c]
                o_inter[c][ck] = lax.dot_general(q_dec[c][lo:lo + chunk], st.astype(BF16), _NT_DIMS,
                                                 preferred_element_type=F32)
                k_end = (k_inv[c][lo:lo + chunk] * e_end).astype(BF16)
                upd = lax.dot_general(vb[c][lo:lo + chunk], k_end, _TN_DIMS, preferred_element_type=F32)
                st_ref[c] = st * e_end + upd
        for c, (_, dn) in enumerate(chains):
            oacc_refs[dn][rows[c], vcols[c]] = o_intra[c] + jnp.concatenate(o_inter[c], axis=0)

    st_ref[...] = jnp.zeros_like(st_ref)

    def body(i, carry):
        step(i)
        return carry

    lax.fori_loop(0, n_tiles, body, 0)

    def finish(i, carry):
        rows = pl.ds(pl.multiple_of(i * tile, tile), tile)
        for h in range(n_heads):
            cols = slice(h * hv, (h + 1) * hv)
            o = of_ref[rows, cols] + ob_ref[rows, cols]
            ms = jnp.sum(o * o, axis=-1, keepdims=True) * inv_dv
            gate = gg_ref[rows, cols].astype(F32)
            o = o * lax.rsqrt(ms + LN_EPS) * ng_ref[h] * (gate * jax.nn.sigmoid(gate))
            o_ref[rows, cols] = o.astype(o_ref.dtype)
        return carry

    lax.fori_loop(0, n_tiles, finish, 0)


def _gla(proj, wdec_f, wdec_b, bdec_f, bdec_b, norm_g, layer, *, batch, seq, offs, hk, hv, q_scale, dv_head):
    heads = wdec_f.shape[1]
    hps = 2 if heads % 2 == 0 else 1
    tile = min(256, seq)
    kern = functools.partial(_gla_kernel, seq=seq, tile=tile, chunk=GLA_CHUNK, hk=hk, hv=hv, q_scale=q_scale,
                             inv_dv=1.0 / dv_head)
    wk, wv = hps * hk, hps * hv
    assert all(offs[n] % w == 0 for n, w in (("gq", wk), ("gk", wk), ("gv", wv), ("gg", wv)))
    cq, ck, cv, cg, cr = offs["gq"] // wk, offs["gk"] // wk, offs["gv"] // wv, offs["gg"] // wv, offs["r"] // LANES
    per_head = lambda rows, cols: pl.BlockSpec((None, hps, rows, cols), lambda b, g: (layer, g, 0, 0))
    return pl.pallas_call(
        kern,
        grid=(batch, heads // hps),
        in_specs=[
            pl.BlockSpec((seq, wk), lambda b, g: (b, cq + g)),
            pl.BlockSpec((seq, wk), lambda b, g: (b, ck + g)),
            pl.BlockSpec((seq, wv), lambda b, g: (b, cv + g)),
            pl.BlockSpec((seq, wv), lambda b, g: (b, cg + g)),
            pl.BlockSpec((seq, LANES), lambda b, g: (b, cr)),
            per_head(LANES, hk), per_head(LANES, hk),
            per_head(1, hk), per_head(1, hk), per_head(1, hv),
        ],
        out_specs=pl.BlockSpec((seq, wv), lambda b, g: (b, g)),
        out_shape=jax.ShapeDtypeStruct((batch * seq, heads * hv), BF16),
        scratch_shapes=[pltpu.VMEM((seq, wv), F32), pltpu.VMEM((seq, wv), F32),
                        pltpu.VMEM((2 * hps, hv, hk), F32)],
        compiler_params=_params(2),
        name="gla",
    )(proj, proj, proj, proj, proj, wdec_f, wdec_b, bdec_f, bdec_b, norm_g)


def _dil_cfgs(seq):
    cfgs = []
    for w, d in DIL_BRANCHES:
        reach = (w // 2) // d
        length = seq // d
        qb = min(128, length)
        kb = min(length, qb + 2 * reach)
        n_qb = length // qb
        rel_starts = sorted({min(max(jb * qb - reach, 0), length - kb) - jb * qb for jb in range(n_qb)},
                            reverse=True)
        cfgs.append((d, reach, length, qb, kb, n_qb, rel_starts))
    return cfgs


def _dil_kernel(slopes_ref, q_ref, k_ref, v_ref, o_ref, *stats, seq, scale):
    n_br = len(DIL_BRANCHES)
    m_refs, l_refs, acc_refs, bias_refs = (stats[i * n_br:(i + 1) * n_br] for i in range(4))
    qkv16 = [stats[4 * n_br + 3 * n:4 * n_br + 3 * n + 3] for n in range(n_br)]
    qkv32 = [None, *[stats[7 * n_br + 3 * (n - 1):7 * n_br + 3 * n] for n in range(1, n_br - 1)]]
    slope = slopes_ref[pl.program_id(1)]
    hd = q_ref.shape[-1]
    dils = [d for _, d in DIL_BRANCHES]

    cfgs = _dil_cfgs(seq)
    for n, (d, reach, length, qb, kb, n_qb, rel_starts) in enumerate(cfgs):
        rel = lax.broadcasted_iota(jnp.int32, (qb, kb), 1) - lax.broadcasted_iota(jnp.int32, (qb, kb), 0)
        for vi, rs in enumerate(rel_starts):
            dist = jnp.abs(rel + rs)
            bias_refs[n][vi] = jnp.where(dist <= reach, -(slope * float(d)) * dist.astype(F32), MASK_VALUE)

    rt = min(256, seq)
    srcs = (q_ref, k_ref, v_ref)
    for a in range(3):
        for t0 in range(0, seq, rt):
            x = srcs[a][t0:t0 + rt, :]
            qkv16[0][a][t0:t0 + rt, :] = (x * scale if a == 0 else x).astype(BF16)
    for n in range(1, n_br):
        assert dils[n] % dils[n - 1] == 0
        step, l_prev, l_cur = dils[n] // dils[n - 1], seq // dils[n - 1], seq // dils[n]
        for a in range(3):
            src = srcs[a] if n == 1 else qkv32[n - 1][a]
            for bp in range(dils[n - 1]):
                for r2 in range(step):
                    for t0 in range(0, l_cur, rt):
                        rows = min(rt, l_cur - t0)
                        x = src[pl.ds(bp * l_prev + r2 + step * t0, rows, stride=step), :]
                        dst = (bp * step + r2) * l_cur + t0
                        if n < n_br - 1:
                            qkv32[n][a][dst:dst + rows, :] = x
                        qkv16[n][a][dst:dst + rows, :] = (x * scale if a == 0 else x).astype(BF16)

    def _residue(n, b):
        if n == 0:
            return 0
        step = dils[n] // dils[n - 1]
        return lax.rem(b, jnp.int32(step)) * dils[n - 1] + _residue(n - 1, lax.div(b, jnp.int32(step)))

    def blocks(jobs):
        place = []
        for n, idx in jobs:
            d, reach, length, qb, kb, n_qb, rel_starts = cfgs[n]
            if d == 1:
                blk, jb = 0, idx
            else:
                blk, jb = lax.div(idx, jnp.int32(n_qb)), lax.rem(idx, jnp.int32(n_qb))
            i0 = jb * qb
            ks = jnp.clip(i0 - reach, 0, length - kb)
            variant = sum(((ks - i0) <= rs).astype(jnp.int32) for rs in rel_starts[1:])
            base = blk * length
            qrows = pl.ds(pl.multiple_of(base + i0, qb), qb)
            krows = pl.ds(pl.multiple_of(base + ks, 16), kb)
            orows = qrows if d == 1 else pl.ds(_residue(n, blk) + d * i0, qb, stride=d)
            place.append((n, qrows, krows, orows, variant, qb, kb))
        s = []
        for n, qrows, krows, orows, variant, qb, kb in place:
            s.append(lax.dot_general(qkv16[n][0][qrows, :], qkv16[n][1][krows, :], _NT_DIMS,
                                     preferred_element_type=F32) + bias_refs[n][variant])
        m = [jnp.max(sj, axis=-1, keepdims=True) for sj in s]
        p = [jnp.exp(sj - mj).astype(BF16) for sj, mj in zip(s, m)]
        for (n, qrows, krows, orows, variant, qb, kb), mj, pj in zip(place, m, p):
            v_ones = jnp.concatenate([qkv16[n][2][krows, :], jnp.ones((kb, hd), BF16)], axis=1)
            acc_l = jnp.dot(pj, v_ones, preferred_element_type=F32)
            acc_refs[n][orows, :] = acc_l[:, :hd]
            l_refs[n][orows, :] = acc_l[:, hd:]
            m_refs[n][orows, :] = jnp.broadcast_to(mj, (qb, hd))

    counts = [seq // min(128, seq // d) for _, d in DIL_BRANCHES]
    group = 4
    if len(set(counts)) == 1 and counts[0] % group == 0:
        def body(i, carry):
            blocks([(n, group * i + u) for u in range(group) for n in range(n_br)])
            return carry
        lax.fori_loop(0, counts[0] // group, body, 0)
    else:
        for n in range(n_br):
            def body(idx, carry, n=n):
                blocks([(n, idx)])
                return carry
            lax.fori_loop(0, counts[n], body, 0)

    rt = min(256, seq)

    def finish(i, carry):
        rows = pl.ds(pl.multiple_of(i * rt, rt), rt)
        ms = [m_ref[rows, :] for m_ref in m_refs]
        m_all = functools.reduce(jnp.maximum, ms)
        num = jnp.zeros((rt, hd), F32)
        den = jnp.zeros((rt, hd), F32)
        for n in range(n_br):
            wgt = jnp.exp(ms[n] - m_all)
            num += wgt * acc_refs[n][rows, :]
            den += wgt * l_refs[n][rows, :]
        o_ref[rows, :] = (num / den).astype(o_ref.dtype)
        return carry

    lax.fori_loop(0, seq // rt, finish, 0)


def _dil(proj, slopes, *, batch, seq, offs, hd):
    cq, ck, cv = offs["dq"] // hd, offs["dk"] // hd, offs["dv"] // hd
    kern = functools.partial(_dil_kernel, seq=seq, scale=hd ** -0.5)
    return pl.pallas_call(
        kern,
        grid=(batch, DIL_HEADS),
        in_specs=[
            pl.BlockSpec(memory_space=pltpu.SMEM),
            pl.BlockSpec((seq, hd), lambda b, h: (b, cq + h)),
            pl.BlockSpec((seq, hd), lambda b, h: (b, ck + h)),
            pl.BlockSpec((seq, hd), lambda b, h: (b, cv + h)),
        ],
        out_specs=pl.BlockSpec((seq, hd), lambda b, h: (b, h)),
        out_shape=jax.ShapeDtypeStruct((batch * seq, DIL_HEADS * hd), BF16),
        scratch_shapes=([pltpu.VMEM((seq, hd), F32)] * (3 * len(DIL_BRANCHES))
                        + [pltpu.VMEM((len(c[6]), c[3], c[4]), F32) for c in _dil_cfgs(seq)]
                        + [pltpu.VMEM((seq, hd), BF16)] * (3 * len(DIL_BRANCHES))
                        + [pltpu.VMEM((seq, hd), F32)] * (3 * (len(DIL_BRANCHES) - 2))),
        compiler_params=_params(2),
        name="dil",
    )(slopes, proj, proj, proj)


def _conv_kernel(prev_ref, cur_ref, next_ref, w_ref, cb_ref, g_ref, b_ref, o_ref, u_ref, *, halo, taps):
    i = pl.program_id(1)
    c = o_ref.shape[-1]
    tq = o_ref.shape[0]

    def glu(ref):
        x = ref[...]
        return x[:, :c] * jax.nn.sigmoid(x[:, c:])

    ext = tq + 2 * halo
    u_ref[0, 0:halo, :] = jnp.where(i > 0, glu(prev_ref), 0.0)
    u_ref[0, halo:halo + tq, :] = glu(cur_ref)
    u_ref[0, halo + tq:, :] = jnp.where(i < pl.num_programs(1) - 1, glu(next_ref), 0.0)
    for s in range(1, SUBLANES):
        u_ref[s, 0:ext - SUBLANES, :] = u_ref[0, pl.ds(s, ext - SUBLANES), :]

    pad = taps // 2
    rc = min(32, tq)
    first = halo - pad
    span = (first + taps - 1) // SUBLANES * SUBLANES

    def chunk(j, carry):
        r0 = pl.multiple_of(j * rc, rc)
        acc = jnp.zeros((rc, c), F32) + cb_ref[...]
        for s in range(SUBLANES):
            win = u_ref[s, pl.ds(r0, rc + span), :]
            for t in range(taps):
                a, ts = divmod(first + t, SUBLANES)
                if ts == s:
                    acc = acc + win[a * SUBLANES:a * SUBLANES + rc] * w_ref[t:t + 1, :]
        y = _layer_norm(acc, g_ref[...], b_ref[...])
        o_ref[pl.ds(r0, rc), :] = (y * jax.nn.sigmoid(y)).astype(o_ref.dtype)
        return carry

    lax.fori_loop(0, tq // rc, chunk, 0, unroll=2)


def _conv(proj, conv_w, conv_b, ln_g, ln_b, layer, *, batch, seq, offs):
    taps, c = conv_w.shape[1:]
    halo = 16
    assert taps // 2 <= halo and halo % SUBLANES == 0
    tq = min(512, seq)
    nt = seq // tq
    hb = tq // halo
    ccol = offs["cv"] // (2 * c)
    kern = functools.partial(_conv_kernel, halo=halo, taps=taps)
    return pl.pallas_call(
        kern,
        grid=(batch, nt),
        in_specs=[
            pl.BlockSpec((halo, 2 * c), lambda b, i: (jnp.maximum((b * nt + i) * hb - 1, 0), ccol)),
            pl.BlockSpec((tq, 2 * c), lambda b, i: (b * nt + i, ccol)),
            pl.BlockSpec((halo, 2 * c),
                         lambda b, i: (jnp.minimum((b * nt + i + 1) * hb, batch * nt * hb - 1), ccol)),
            _layer_spec((taps, c), layer),
            _layer_spec((1, c), layer),
            _layer_spec((1, c), layer),
            _layer_spec((1, c), layer),
        ],
        out_specs=pl.BlockSpec((tq, c), lambda b, i: (b * nt + i, 0)),
        out_shape=jax.ShapeDtypeStruct((batch * seq, c), BF16),
        scratch_shapes=[pltpu.VMEM((SUBLANES, tq + 2 * halo, c), F32)],
        compiler_params=_params(2),
        name="conv",
    )(proj, proj, proj, conv_w, conv_b, ln_g, ln_b)


def _out_ln_kernel(x_ref, a_ref, d_ref, c_ref, w_ref, g_ref, b_ref, o_ref, *, alpha):
    na, nd = a_ref.shape[1], d_ref.shape[1]
    mix = jnp.dot(a_ref[...], w_ref[0:na, :], preferred_element_type=F32)
    mix += jnp.dot(d_ref[...], w_ref[na:na + nd, :], preferred_element_type=F32)
    mix += jnp.dot(c_ref[...], w_ref[na + nd:, :], preferred_element_type=F32)
    o_ref[...] = _layer_norm(alpha * x_ref[...] + mix, g_ref[...], b_ref[...])


def _out_ln(x, gla_o, dil_o, conv_o, w, g, b, layer, *, alpha, tm):
    m, d = x.shape
    row = lambda i: (i, 0)
    return pl.pallas_call(
        functools.partial(_out_ln_kernel, alpha=alpha),
        grid=(m // tm,),
        in_specs=[
            pl.BlockSpec((tm, d), row),
            pl.BlockSpec((tm, gla_o.shape[1]), row),
            pl.BlockSpec((tm, dil_o.shape[1]), row),
            pl.BlockSpec((tm, conv_o.shape[1]), row),
            _layer_spec(w.shape[1:], layer),
            _layer_spec((1, d), layer),
            _layer_spec((1, d), layer),
        ],
        out_specs=pl.BlockSpec((tm, d), row),
        out_shape=jax.ShapeDtypeStruct((m, d), F32),
        compiler_params=_params(1),
        name="out_ln",
    )(x, gla_o, dil_o, conv_o, w, g, b)


def _pad_heads(w, heads, width):
    lead = w.shape[:-1]
    hd = w.shape[-1] // heads
    w = w.reshape(lead + (heads, hd))
    w = jnp.pad(w, [(0, 0)] * len(lead) + [(0, 0), (0, width - hd)])
    return w.reshape(lead + (heads * width,))


def kernel(x, ffn1_w_gate, ffn1_w_up, ffn1_w_down, ln1_g, ln1_b, w_in, gla_decay_w_fwd, gla_decay_b_fwd, gla_decay_w_bwd, gla_decay_b_bwd, gla_norm_g, conv_w, conv_b, conv_ln_g, conv_ln_b, w_out, ln2_g, ln2_b, ffn2_w_gate, ffn2_w_up, ffn2_w_down, ln3_g, ln3_b):
    batch, seq, d_model = x.shape
    depth = ffn1_w_gate.shape[0]
    rank, gla_dk = gla_decay_w_fwd.shape[1:]
    gla_dv = gla_norm_g.shape[1]
    conv_c = conv_w.shape[2]
    dil_dim = w_out.shape[1] - gla_dv - conv_c
    dil_hd = dil_dim // DIL_HEADS
    hdk, hdv = gla_dk // GLA_HEADS, gla_dv // GLA_HEADS
    hk, hv = _round_up(hdk, LANES), _round_up(hdv, LANES)
    assert dil_hd == LANES and 2 * rank <= LANES and seq % GLA_CHUNK == 0

    alpha = (2.0 * depth) ** 0.25
    m = batch * seq
    tm = min(512, m)

    def layout(widths):
        offs, total = {}, 0
        for name, wdt in widths:
            offs[name] = total
            total += wdt
        return offs

    offs_a = layout((("gq", GLA_HEADS * hk), ("gk", GLA_HEADS * hk), ("gv", GLA_HEADS * hv),
                     ("gg", GLA_HEADS * hv), ("r", LANES)))
    offs_b = layout((("cv", 2 * conv_c), ("dq", dil_dim), ("dk", dil_dim), ("dv", dil_dim)))

    in_sizes = (gla_dk, gla_dk, gla_dv, rank, rank, gla_dv, dil_dim, dil_dim, dil_dim, 2 * conv_c)
    cuts = [int(c) for c in np.cumsum(in_sizes)[:-1]]
    gq, gk, gv, rf, rb, gg, dq, dk, dv, cv = jnp.split(w_in.astype(BF16), cuts, axis=2)
    w_in_a = jnp.concatenate(
        [_pad_heads(gq, GLA_HEADS, hk), _pad_heads(gk, GLA_HEADS, hk), _pad_heads(gv, GLA_HEADS, hv),
         _pad_heads(gg, GLA_HEADS, hv), rf, rb, jnp.zeros((depth, d_model, LANES - 2 * rank), BF16)], axis=2)
    w_in_b = jnp.concatenate([cv, dq, dk, dv], axis=2)

    def head_mats(wdec, row0):
        w4 = _pad_heads(wdec, GLA_HEADS, hk).reshape(depth, rank, GLA_HEADS, hk).transpose(0, 2, 1, 3)
        return jnp.pad(w4, ((0, 0), (0, 0), (row0, LANES - rank - row0), (0, 0))).astype(BF16)

    wdec_f, wdec_b = head_mats(gla_decay_w_fwd, 0), head_mats(gla_decay_w_bwd, rank)
    bdec_f = _pad_heads(gla_decay_b_fwd, GLA_HEADS, hk).reshape(depth, GLA_HEADS, 1, hk)
    bdec_b = _pad_heads(gla_decay_b_bwd, GLA_HEADS, hk).reshape(depth, GLA_HEADS, 1, hk)
    norm_g = _pad_heads(gla_norm_g, GLA_HEADS, hv).reshape(depth, GLA_HEADS, 1, hv)

    w_out_b = w_out.astype(BF16)
    w_gla = jnp.pad(w_out_b[:, :gla_dv].reshape(depth, GLA_HEADS, hdv, d_model), ((0, 0), (0, 0), (0, hv - hdv), (0, 0)))
    w_out_p = jnp.concatenate([w_gla.reshape(depth, GLA_HEADS * hv, d_model), w_out_b[:, gla_dv:]], axis=1)

    ffn1 = [w.astype(BF16) for w in (ffn1_w_gate, ffn1_w_up, ffn1_w_down)]
    ffn2 = [w.astype(BF16) for w in (ffn2_w_gate, ffn2_w_up, ffn2_w_down)]
    vec = lambda p: p[:, None, :]
    slopes = jnp.asarray(2.0 ** (-8.0 * np.arange(1, DIL_HEADS + 1) / DIL_HEADS), F32)

    xf = x.reshape(m, d_model)
    for l in range(depth):
        xf = _ffn_ln(xf, *ffn1, vec(ln1_g), vec(ln1_b), l, alpha=alpha, tm=tm)
        proj_a = _in_proj(xf, w_in_a, l, BF16, tm=tm, name="in_proj_gla")
        proj_b = _in_proj(xf, w_in_b, l, F32, tm=tm, name="in_proj_mix")
        gla_o = _gla(proj_a, wdec_f, wdec_b, bdec_f, bdec_b, norm_g, l, batch=batch, seq=seq, offs=offs_a,
                     hk=hk, hv=hv, q_scale=hdk ** -0.5, dv_head=hdv)
        dil_o = _dil(proj_b, slopes, batch=batch, seq=seq, offs=offs_b, hd=dil_hd)
        conv_o = _conv(proj_b, conv_w, vec(conv_b), vec(conv_ln_g), vec(conv_ln_b), l, batch=batch, seq=seq,
                       offs=offs_b)
        xf = _out_ln(xf, gla_o, dil_o, conv_o, w_out_p, vec(ln2_g), vec(ln2_b), l, alpha=alpha, tm=tm)
        xf = _ffn_ln(xf, *ffn2, vec(ln3_g), vec(ln3_b), l, alpha=alpha, tm=tm)
    return xf.reshape(batch, seq, d_model)
```

```python
import functools

import numpy as np
import jax
import jax.numpy as jnp
from jax import lax
from jax.experimental import pallas as pl
from jax.experimental.pallas import tpu as pltpu

GLA_HEADS = 4
GLA_TAU = 16.0
GLA_CHUNK = 64
DIL_HEADS = 6
DIL_BRANCHES = ((128, 1), (512, 4), (2048, 16))
LN_EPS = 1e-5
MASK_VALUE = -1e30

LANES = 128
SUBLANES = 8
BF16_ROWS = 16
MXU_WIDTH = 256
VMEM_LIMIT_BYTES = 56 * 1024 * 1024

F32 = jnp.float32
BF16 = jnp.bfloat16
_NT_DIMS = (((1,), (1,)), ((), ()))
_TN_DIMS = (((0,), (0,)), ((), ()))


def _round_up(n, m):
    return (n + m - 1) // m * m


def _layer_norm(y, g, b):
    mu = jnp.mean(y, axis=-1, keepdims=True)
    yc = y - mu
    var = jnp.mean(yc * yc, axis=-1, keepdims=True)
    return yc * lax.rsqrt(var + LN_EPS) * g + b


def _params(n_grid_axes):
    return pltpu.CompilerParams(dimension_semantics=("arbitrary",) * n_grid_axes,
                                vmem_limit_bytes=VMEM_LIMIT_BYTES)


def _layer_spec(shape, layer):
    zeros = (0,) * len(shape)
    return pl.BlockSpec((None,) + tuple(shape), lambda *_: (layer,) + zeros)


def _swiglu_part(xb, wg, wu, wd):
    gate = jnp.dot(xb, wg, preferred_element_type=F32)
    up = jnp.dot(xb, wu, preferred_element_type=F32)
    h = (gate * jax.nn.sigmoid(gate) * up).astype(BF16)
    return jnp.dot(h, wd, preferred_element_type=F32)


def _ffn_ln_kernel(x_ref, wg_ref, wu_ref, wd_ref, wgt_ref, wut_ref, wdt_ref, g_ref, b_ref, *rest, alpha, n_cast):
    src_refs, o_ref, dst_refs, xb_ref = rest[:n_cast], rest[n_cast], rest[n_cast + 1:-1], rest[-1]
    f = pl.program_id(1)

    @pl.when(f == 0)
    def _():
        xb = x_ref[...].astype(BF16)
        xb_ref[...] = xb
        o_ref[...] = _swiglu_part(xb, wgt_ref[...], wut_ref[...], wdt_ref[...])

    o_ref[...] += _swiglu_part(xb_ref[...], wg_ref[...], wu_ref[...], wd_ref[...])
    for src, dst in zip(src_refs, dst_refs):
        dst[...] = src[...].astype(BF16)

    @pl.when(f == pl.num_programs(1) - 1)
    def _():
        y = alpha * x_ref[...] + 0.5 * o_ref[...]
        o_ref[...] = _layer_norm(y, g_ref[...], b_ref[...])


def _ffn_tiles(d_ff):
    for tf in (3 * MXU_WIDTH, 2 * MXU_WIDTH, MXU_WIDTH):
        tail = d_ff % tf
        if tail and tail % LANES == 0 and (d_ff - tail) % tail == 0:
            return tf, tail
    raise NotImplementedError(f"no FFN tiling for d_ff={d_ff}")


def _slab_rows(rows, n_steps):
    for r in range(BF16_ROWS, rows + 1, BF16_ROWS):
        if rows % r == 0 and rows // r <= n_steps:
            return r
    raise NotImplementedError(f"cannot spread {rows} rows over {n_steps} steps")


def _ffn_ln(x, wg, wu, wd, g, b, layer, cast_next=None, *, alpha, tm):
    m, d = x.shape
    d_ff = wg.shape[1]
    tf, tail = _ffn_tiles(d_ff)
    n_f = d_ff // tf
    tail_blk = (d_ff - tail) // tail
    in_specs = [
        pl.BlockSpec((tm, d), lambda i, f: (i, 0)),
        pl.BlockSpec((d, tf), lambda i, f: (0, f)),
        pl.BlockSpec((d, tf), lambda i, f: (0, f)),
        pl.BlockSpec((tf, d), lambda i, f: (f, 0)),
        pl.BlockSpec((d, tail), lambda i, f: (0, tail_blk)),
        pl.BlockSpec((d, tail), lambda i, f: (0, tail_blk)),
        pl.BlockSpec((tail, d), lambda i, f: (tail_blk, 0)),
        _layer_spec((1, d), layer),
        _layer_spec((1, d), layer),
    ]
    out_specs = [pl.BlockSpec((tm, d), lambda i, f: (i, 0))]
    out_shape = [jax.ShapeDtypeStruct((m, d), F32)]
    args = [x, wg, wu, wd, wg, wu, wd, g, b]
    if cast_next is not None:
        srcs, nxt = cast_next
        n_steps = (m // tm) * n_f
        for w in srcs:
            rows, cols = w.shape[1:]
            r = _slab_rows(rows, n_steps)
            slab = lambda i, f, last=rows // r - 1: jnp.minimum(i * n_f + f, last)
            in_specs.append(pl.BlockSpec((None, r, cols), lambda i, f, slab=slab: (nxt, slab(i, f), 0)))
            out_specs.append(pl.BlockSpec((r, cols), lambda i, f, slab=slab: (slab(i, f), 0)))
            out_shape.append(jax.ShapeDtypeStruct((rows, cols), BF16))
            args.append(w)
    outs = pl.pallas_call(
        functools.partial(_ffn_ln_kernel, alpha=alpha, n_cast=len(args) - 9),
        grid=(m // tm, n_f),
        in_specs=in_specs,
        out_specs=out_specs,
        out_shape=out_shape,
        scratch_shapes=[pltpu.VMEM((tm, d), BF16)],
        compiler_params=_params(2),
        name="ffn_ln",
    )(*args)
    return outs[0], tuple(outs[1:])


def _in_proj_kernel(x_ref, w_ref, o_ref):
    o_ref[...] = jnp.dot(x_ref[...].astype(BF16), w_ref[...], preferred_element_type=F32).astype(o_ref.dtype)


def _in_proj(x, w, layer, out_dtype, *, tm, name):
    m, d = x.shape
    n = w.shape[2]
    return pl.pallas_call(
        _in_proj_kernel,
        grid=(m // tm,),
        in_specs=[
            pl.BlockSpec((tm, d), lambda i: (i, 0)),
            pl.BlockSpec((None, d, n), lambda i: (layer, 0, 0), pipeline_mode=pl.Buffered(1)),
        ],
        out_specs=pl.BlockSpec((tm, n), lambda i: (i, 0)),
        out_shape=jax.ShapeDtypeStruct((m, n), out_dtype),
        compiler_params=_params(1),
        name=name,
    )(x, w)


def _log_sigmoid(z):
    return jnp.minimum(z, 0.0) - jnp.log(1.0 + jnp.exp(-jnp.abs(z)))


def _gla_kernel(q_ref, k_ref, v_ref, gg_ref, r_ref, wf_ref, wb_ref, bf_ref, bb_ref, ng_ref,
                o_ref, of_ref, ob_ref, st_ref, *, seq, tile, chunk, hk, hv, q_scale, inv_dv):
    n_tiles = seq // tile
    n_chunks = tile // chunk
    n_heads = q_ref.shape[-1] // hk
    shift = chunk.bit_length() - 1
    row = lax.broadcasted_iota(jnp.int32, (tile, tile), 0)
    col = lax.broadcasted_iota(jnp.int32, (tile, tile), 1)
    same_chunk = lax.shift_right_logical(row, shift) == lax.shift_right_logical(col, shift)
    tris = (same_chunk & (col <= row), same_chunk & (col >= row))
    oacc_refs = (of_ref, ob_ref)
    w_refs, b_refs = (wf_ref, wb_ref), (bf_ref, bb_ref)
    chains = [(h, dn) for h in range(n_heads) for dn in (0, 1)]
    every = range(len(chains))
    kcols = [slice(h * hk, (h + 1) * hk) for h, _ in chains]
    vcols = [slice(h * hv, (h + 1) * hv) for h, _ in chains]

    def step(i):
        rows = [pl.ds(pl.multiple_of((i if dn == 0 else n_tiles - 1 - i) * tile, tile), tile) for _, dn in chains]
        z = [jnp.dot(r_ref[rows[c], :], w_refs[dn][h], preferred_element_type=F32) + b_refs[dn][h]
             for c, (h, dn) in enumerate(chains)]
        la = [_log_sigmoid(zc) * (1.0 / GLA_TAU) for zc in z]
        la_hi = [x.astype(BF16) for x in la]
        la_lo = [(x - hi.astype(F32)).astype(BF16) for x, hi in zip(la, la_hi)]
        cum = [jnp.dot(tris[dn].astype(BF16), jnp.concatenate([la_hi[c], la_lo[c]], axis=1),
                       preferred_element_type=F32) for c, (_, dn) in enumerate(chains)]
        bcum = [x[:, :hk] + x[:, hk:] for x in cum]

        vb = [v_ref[rows[c], vcols[c]] for c in every]
        q_dec = [(q_ref[rows[c], kcols[c]].astype(F32) * q_scale * jnp.exp(bcum[c])).astype(BF16) for c in every]
        k_inv = [k_ref[rows[c], kcols[c]].astype(F32) * jnp.exp(-bcum[c]) for c in every]
        att = [lax.dot_general(q_dec[c], k_inv[c].astype(BF16), _NT_DIMS, preferred_element_type=F32) for c in every]
        att = [jnp.where(tris[dn], att[c], 0.0).astype(BF16) for c, (_, dn) in enumerate(chains)]
        o_intra = [jnp.dot(att[c], vb[c], preferred_element_type=F32) for c in every]

        o_inter = [[None] * n_chunks for _ in every]
        for s in range(n_chunks):
            for c, (_, dn) in enumerate(chains):
                ck = s if dn == 0 else n_chunks - 1 - s
                lo = ck * chunk
                end_row = lo + chunk - 1 if dn == 0 else lo
                e_end = jnp.exp(bcum[c][end_row:end_row + 1, :])
                st = st_ref[c]
                o_inter[c][ck] = lax.dot_general(q_dec[c][lo:lo + chunk], st.astype(BF16), _NT_DIMS,
                                                 preferred_element_type=F32)
                k_end = (k_inv[c][lo:lo + chunk] * e_end).astype(BF16)
                upd = lax.dot_general(vb[c][lo:lo + chunk], k_end, _TN_DIMS, preferred_element_type=F32)
                st_ref[c] = st * e_end + upd
        for c, (_, dn) in enumerate(chains):
            oacc_refs[dn][rows[c], vcols[c]] = o_intra[c] + jnp.concatenate(o_inter[c], axis=0)

    st_ref[...] = jnp.zeros_like(st_ref)

    def body(i, carry):
        step(i)
        return carry

    lax.fori_loop(0, n_tiles, body, 0)

    def finish(i, carry):
        rows = pl.ds(pl.multiple_of(i * tile, tile), tile)
        for h in range(n_heads):
            cols = slice(h * hv, (h + 1) * hv)
            o = of_ref[rows, cols] + ob_ref[rows, cols]
            ms = jnp.sum(o * o, axis=-1, keepdims=True) * inv_dv
            gate = gg_ref[rows, cols].astype(F32)
            o = o * lax.rsqrt(ms + LN_EPS) * ng_ref[h] * (gate * jax.nn.sigmoid(gate))
            o_ref[rows, cols] = o.astype(o_ref.dtype)
        return carry

    lax.fori_loop(0, n_tiles, finish, 0)


def _gla(proj, wdec_f, wdec_b, bdec_f, bdec_b, norm_g, layer, *, batch, seq, offs, hk, hv, q_scale, dv_head):
    heads = wdec_f.shape[1]
    hps = 2 if heads % 2 == 0 else 1
    tile = min(256, seq)
    kern = functools.partial(_gla_kernel, seq=seq, tile=tile, chunk=GLA_CHUNK, hk=hk, hv=hv, q_scale=q_scale,
                             inv_dv=1.0 / dv_head)
    wk, wv = hps * hk, hps * hv
    assert all(offs[n] % w == 0 for n, w in (("gq", wk), ("gk", wk), ("gv", wv), ("gg", wv)))
    cq, ck, cv, cg, cr = offs["gq"] // wk, offs["gk"] // wk, offs["gv"] // wv, offs["gg"] // wv, offs["r"] // LANES
    per_head = lambda rows, cols: pl.BlockSpec((None, hps, rows, cols), lambda b, g: (layer, g, 0, 0))
    return pl.pallas_call(
        kern,
        grid=(batch, heads // hps),
        in_specs=[
            pl.BlockSpec((seq, wk), lambda b, g: (b, cq + g)),
            pl.BlockSpec((seq, wk), lambda b, g: (b, ck + g)),
            pl.BlockSpec((seq, wv), lambda b, g: (b, cv + g)),
            pl.BlockSpec((seq, wv), lambda b, g: (b, cg + g)),
            pl.BlockSpec((seq, LANES), lambda b, g: (b, cr)),
            per_head(LANES, hk), per_head(LANES, hk),
            per_head(1, hk), per_head(1, hk), per_head(1, hv),
        ],
        out_specs=pl.BlockSpec((seq, wv), lambda b, g: (b, g)),
        out_shape=jax.ShapeDtypeStruct((batch * seq, heads * hv), BF16),
        scratch_shapes=[pltpu.VMEM((seq, wv), F32), pltpu.VMEM((seq, wv), F32),
                        pltpu.VMEM((2 * hps, hv, hk), F32)],
        compiler_params=_params(2),
        name="gla",
    )(proj, proj, proj, proj, proj, wdec_f, wdec_b, bdec_f, bdec_b, norm_g)


def _dil_cfgs(seq):
    cfgs = []
    for w, d in DIL_BRANCHES:
        reach = (w // 2) // d
        length = seq // d
        qb = min(128, length)
        kb = min(length, qb + 2 * reach)
        n_qb = length // qb
        rel_starts = sorted({min(max(jb * qb - reach, 0), length - kb) - jb * qb for jb in range(n_qb)},
                            reverse=True)
        cfgs.append((d, reach, length, qb, kb, n_qb, rel_starts))
    return cfgs


def _dil_kernel(slopes_ref, q_ref, k_ref, v_ref, o_ref, *stats, seq, scale):
    n_br = len(DIL_BRANCHES)
    m_refs, l_refs, acc_refs, bias_refs = (stats[i * n_br:(i + 1) * n_br] for i in range(4))
    qkv16 = [stats[4 * n_br + 3 * n:4 * n_br + 3 * n + 3] for n in range(n_br)]
    qkv32 = [None, *[stats[7 * n_br + 3 * (n - 1):7 * n_br + 3 * n] for n in range(1, n_br - 1)]]
    slope = slopes_ref[pl.program_id(1)]
    hd = q_ref.shape[-1]
    dils = [d for _, d in DIL_BRANCHES]

    cfgs = _dil_cfgs(seq)
    for n, (d, reach, length, qb, kb, n_qb, rel_starts) in enumerate(cfgs):
        rel = lax.broadcasted_iota(jnp.int32, (qb, kb), 1) - lax.broadcasted_iota(jnp.int32, (qb, kb), 0)
        for vi, rs in enumerate(rel_starts):
            dist = jnp.abs(rel + rs)
            bias_refs[n][vi] = jnp.where(dist <= reach, -(slope * float(d)) * dist.astype(F32), MASK_VALUE)

    rt = min(256, seq)
    srcs = (q_ref, k_ref, v_ref)
    for a in range(3):
        for t0 in range(0, seq, rt):
            x = srcs[a][t0:t0 + rt, :]
            qkv16[0][a][t0:t0 + rt, :] = (x * scale if a == 0 else x).astype(BF16)
    for n in range(1, n_br):
        assert dils[n] % dils[n - 1] == 0
        step, l_prev, l_cur = dils[n] // dils[n - 1], seq // dils[n - 1], seq // dils[n]
        for a in range(3):
            src = srcs[a] if n == 1 else qkv32[n - 1][a]
            for bp in range(dils[n - 1]):
                for r2 in range(step):
                    for t0 in range(0, l_cur, rt):
                        rows = min(rt, l_cur - t0)
                        x = src[pl.ds(bp * l_prev + r2 + step * t0, rows, stride=step), :]
                        dst = (bp * step + r2) * l_cur + t0
                        if n < n_br - 1:
                            qkv32[n][a][dst:dst + rows, :] = x
                        qkv16[n][a][dst:dst + rows, :] = (x * scale if a == 0 else x).astype(BF16)

    def _residue(n, b):
        if n == 0:
            return 0
        step = dils[n] // dils[n - 1]
        return lax.rem(b, jnp.int32(step)) * dils[n - 1] + _residue(n - 1, lax.div(b, jnp.int32(step)))

    def blocks(jobs):
        place = []
        for n, idx in jobs:
            d, reach, length, qb, kb, n_qb, rel_starts = cfgs[n]
            if d == 1:
                blk, jb = 0, idx
            else:
                blk, jb = lax.div(idx, jnp.int32(n_qb)), lax.rem(idx, jnp.int32(n_qb))
            i0 = jb * qb
            ks = jnp.clip(i0 - reach, 0, length - kb)
            variant = sum(((ks - i0) <= rs).astype(jnp.int32) for rs in rel_starts[1:])
            base = blk * length
            qrows = pl.ds(pl.multiple_of(base + i0, qb), qb)
            krows = pl.ds(pl.multiple_of(base + ks, 16), kb)
            orows = qrows if d == 1 else pl.ds(_residue(n, blk) + d * i0, qb, stride=d)
            place.append((n, qrows, krows, orows, variant, qb, kb))
        s = []
        for n, qrows, krows, orows, variant, qb, kb in place:
            s.append(lax.dot_general(qkv16[n][0][qrows, :], qkv16[n][1][krows, :], _NT_DIMS,
                                     preferred_element_type=F32) + bias_refs[n][variant])
        m = [jnp.max(sj, axis=-1, keepdims=True) for sj in s]
        p = [jnp.exp(sj - mj).astype(BF16) for sj, mj in zip(s, m)]
        for (n, qrows, krows, orows, variant, qb, kb), mj, pj in zip(place, m, p):
            v_ones = jnp.concatenate([qkv16[n][2][krows, :], jnp.ones((kb, hd), BF16)], axis=1)
            acc_l = jnp.dot(pj, v_ones, preferred_element_type=F32)
            acc_refs[n][orows, :] = acc_l[:, :hd]
            l_refs[n][orows, :] = acc_l[:, hd:]
            m_refs[n][orows, :] = jnp.broadcast_to(mj, (qb, hd))

    counts = [seq // min(128, seq // d) for _, d in DIL_BRANCHES]
    group = 4
    if len(set(counts)) == 1 and counts[0] % group == 0:
        def body(i, carry):
            blocks([(n, group * i + u) for u in range(group) for n in range(n_br)])
            return carry
        lax.fori_loop(0, counts[0] // group, body, 0)
    else:
        for n in range(n_br):
            def body(idx, carry, n=n):
                blocks([(n, idx)])
                return carry
            lax.fori_loop(0, counts[n], body, 0)

    rt = min(256, seq)

    def finish(i, carry):
        rows = pl.ds(pl.multiple_of(i * rt, rt), rt)
        ms = [m_ref[rows, :] for m_ref in m_refs]
        m_all = functools.reduce(jnp.maximum, ms)
        num = jnp.zeros((rt, hd), F32)
        den = jnp.zeros((rt, hd), F32)
        for n in range(n_br):
            wgt = jnp.exp(ms[n] - m_all)
            num += wgt * acc_refs[n][rows, :]
            den += wgt * l_refs[n][rows, :]
        o_ref[rows, :] = (num / den).astype(o_ref.dtype)
        return carry

    lax.fori_loop(0, seq // rt, finish, 0)


def _dil(proj, slopes, *, batch, seq, offs, hd):
    cq, ck, cv = offs["dq"] // hd, offs["dk"] // hd, offs["dv"] // hd
    kern = functools.partial(_dil_kernel, seq=seq, scale=hd ** -0.5)
    return pl.pallas_call(
        kern,
        grid=(batch, DIL_HEADS),
        in_specs=[
            pl.BlockSpec(memory_space=pltpu.SMEM),
            pl.BlockSpec((seq, hd), lambda b, h: (b, cq + h)),
            pl.BlockSpec((seq, hd), lambda b, h: (b, ck + h)),
            pl.BlockSpec((seq, hd), lambda b, h: (b, cv + h)),
        ],
        out_specs=pl.BlockSpec((seq, hd), lambda b, h: (b, h)),
        out_shape=jax.ShapeDtypeStruct((batch * seq, DIL_HEADS * hd), BF16),
        scratch_shapes=([pltpu.VMEM((seq, hd), F32)] * (3 * len(DIL_BRANCHES))
                        + [pltpu.VMEM((len(c[6]), c[3], c[4]), F32) for c in _dil_cfgs(seq)]
                        + [pltpu.VMEM((seq, hd), BF16)] * (3 * len(DIL_BRANCHES))
                        + [pltpu.VMEM((seq, hd), F32)] * (3 * (len(DIL_BRANCHES) - 2))),
        compiler_params=_params(2),
        name="dil",
    )(slopes, proj, proj, proj)


def _conv_kernel(prev_ref, cur_ref, next_ref, w_ref, cb_ref, g_ref, b_ref, o_ref, u_ref, *, halo, taps):
    i = pl.program_id(1)
    c = o_ref.shape[-1]
    tq = o_ref.shape[0]

    def glu(ref):
        x = ref[...]
        return x[:, :c] * jax.nn.sigmoid(x[:, c:])

    ext = tq + 2 * halo
    u_ref[0, 0:halo, :] = jnp.where(i > 0, glu(prev_ref), 0.0)
    u_ref[0, halo:halo + tq, :] = glu(cur_ref)
    u_ref[0, halo + tq:, :] = jnp.where(i < pl.num_programs(1) - 1, glu(next_ref), 0.0)
    for s in range(1, SUBLANES):
        u_ref[s, 0:ext - SUBLANES, :] = u_ref[0, pl.ds(s, ext - SUBLANES), :]

    pad = taps // 2
    rc = min(32, tq)
    first = halo - pad
    span = (first + taps - 1) // SUBLANES * SUBLANES

    def chunk(j, carry):
        r0 = pl.multiple_of(j * rc, rc)
        acc = jnp.zeros((rc, c), F32) + cb_ref[...]
        for s in range(SUBLANES):
            win = u_ref[s, pl.ds(r0, rc + span), :]
            for t in range(taps):
                a, ts = divmod(first + t, SUBLANES)
                if ts == s:
                    acc = acc + win[a * SUBLANES:a * SUBLANES + rc] * w_ref[t:t + 1, :]
        y = _layer_norm(acc, g_ref[...], b_ref[...])
        o_ref[pl.ds(r0, rc), :] = (y * jax.nn.sigmoid(y)).astype(o_ref.dtype)
        return carry

    lax.fori_loop(0, tq // rc, chunk, 0, unroll=2)


def _conv(proj, conv_w, conv_b, ln_g, ln_b, layer, *, batch, seq, offs):
    taps, c = conv_w.shape[1:]
    halo = 16
    assert taps // 2 <= halo and halo % SUBLANES == 0
    tq = min(512, seq)
    nt = seq // tq
    hb = tq // halo
    ccol = offs["cv"] // (2 * c)
    kern = functools.partial(_conv_kernel, halo=halo, taps=taps)
    return pl.pallas_call(
        kern,
        grid=(batch, nt),
        in_specs=[
            pl.BlockSpec((halo, 2 * c), lambda b, i: (jnp.maximum((b * nt + i) * hb - 1, 0), ccol)),
            pl.BlockSpec((tq, 2 * c), lambda b, i: (b * nt + i, ccol)),
            pl.BlockSpec((halo, 2 * c),
                         lambda b, i: (jnp.minimum((b * nt + i + 1) * hb, batch * nt * hb - 1), ccol)),
            _layer_spec((taps, c), layer),
            _layer_spec((1, c), layer),
            _layer_spec((1, c), layer),
            _layer_spec((1, c), layer),
        ],
        out_specs=pl.BlockSpec((tq, c), lambda b, i: (b * nt + i, 0)),
        out_shape=jax.ShapeDtypeStruct((batch * seq, c), BF16),
        scratch_shapes=[pltpu.VMEM((SUBLANES, tq + 2 * halo, c), F32)],
        compiler_params=_params(2),
        name="conv",
    )(proj, proj, proj, conv_w, conv_b, ln_g, ln_b)


def _out_ln_kernel(x_ref, a_ref, d_ref, c_ref, w_ref, g_ref, b_ref, o_ref, *, alpha):
    na, nd = a_ref.shape[1], d_ref.shape[1]
    mix = jnp.dot(a_ref[...], w_ref[0:na, :], preferred_element_type=F32)
    mix += jnp.dot(d_ref[...], w_ref[na:na + nd, :], preferred_element_type=F32)
    mix += jnp.dot(c_ref[...], w_ref[na + nd:, :], preferred_element_type=F32)
    o_ref[...] = _layer_norm(alpha * x_ref[...] + mix, g_ref[...], b_ref[...])


def _out_ln(x, gla_o, dil_o, conv_o, w, g, b, layer, *, alpha, tm):
    m, d = x.shape
    row = lambda i: (i, 0)
    return pl.pallas_call(
        functools.partial(_out_ln_kernel, alpha=alpha),
        grid=(m // tm,),
        in_specs=[
            pl.BlockSpec((tm, d), row),
            pl.BlockSpec((tm, gla_o.shape[1]), row),
            pl.BlockSpec((tm, dil_o.shape[1]), row),
            pl.BlockSpec((tm, conv_o.shape[1]), row),
            _layer_spec(w.shape[1:], layer),
            _layer_spec((1, d), layer),
            _layer_spec((1, d), layer),
        ],
        out_specs=pl.BlockSpec((tm, d), row),
        out_shape=jax.ShapeDtypeStruct((m, d), F32),
        compiler_params=_params(1),
        name="out_ln",
    )(x, gla_o, dil_o, conv_o, w, g, b)


def _pad_heads(w, heads, width):
    lead = w.shape[:-1]
    hd = w.shape[-1] // heads
    w = w.reshape(lead + (heads, hd))
    w = jnp.pad(w, [(0, 0)] * len(lead) + [(0, 0), (0, width - hd)])
    return w.reshape(lead + (heads * width,))


def kernel(x, ffn1_w_gate, ffn1_w_up, ffn1_w_down, ln1_g, ln1_b, w_in, gla_decay_w_fwd, gla_decay_b_fwd, gla_decay_w_bwd, gla_decay_b_bwd, gla_norm_g, conv_w, conv_b, conv_ln_g, conv_ln_b, w_out, ln2_g, ln2_b, ffn2_w_gate, ffn2_w_up, ffn2_w_down, ln3_g, ln3_b):
    batch, seq, d_model = x.shape
    depth = ffn1_w_gate.shape[0]
    rank, gla_dk = gla_decay_w_fwd.shape[1:]
    gla_dv = gla_norm_g.shape[1]
    conv_c = conv_w.shape[2]
    dil_dim = w_out.shape[1] - gla_dv - conv_c
    dil_hd = dil_dim // DIL_HEADS
    hdk, hdv = gla_dk // GLA_HEADS, gla_dv // GLA_HEADS
    hk, hv = _round_up(hdk, LANES), _round_up(hdv, LANES)
    assert dil_hd == LANES and 2 * rank <= LANES and seq % GLA_CHUNK == 0

    alpha = (2.0 * depth) ** 0.25
    m = batch * seq
    tm = min(512, m)

    def layout(widths):
        offs, total = {}, 0
        for name, wdt in widths:
            offs[name] = total
            total += wdt
        return offs

    offs_a = layout((("gq", GLA_HEADS * hk), ("gk", GLA_HEADS * hk), ("gv", GLA_HEADS * hv),
                     ("gg", GLA_HEADS * hv), ("r", LANES)))
    offs_b = layout((("cv", 2 * conv_c), ("dq", dil_dim), ("dk", dil_dim), ("dv", dil_dim)))

    in_sizes = (gla_dk, gla_dk, gla_dv, rank, rank, gla_dv, dil_dim, dil_dim, dil_dim, 2 * conv_c)
    cuts = [int(c) for c in np.cumsum(in_sizes)[:-1]]
    gq, gk, gv, rf, rb, gg, dq, dk, dv, cv = jnp.split(w_in.astype(BF16), cuts, axis=2)
    w_in_a = jnp.concatenate(
        [_pad_heads(gq, GLA_HEADS, hk), _pad_heads(gk, GLA_HEADS, hk), _pad_heads(gv, GLA_HEADS, hv),
         _pad_heads(gg, GLA_HEADS, hv), rf, rb, jnp.zeros((depth, d_model, LANES - 2 * rank), BF16)], axis=2)
    w_in_b = jnp.concatenate([cv, dq, dk, dv], axis=2)

    def head_mats(wdec, row0):
        w4 = _pad_heads(wdec, GLA_HEADS, hk).reshape(depth, rank, GLA_HEADS, hk).transpose(0, 2, 1, 3)
        return jnp.pad(w4, ((0, 0), (0, 0), (row0, LANES - rank - row0), (0, 0))).astype(BF16)

    wdec_f, wdec_b = head_mats(gla_decay_w_fwd, 0), head_mats(gla_decay_w_bwd, rank)
    bdec_f = _pad_heads(gla_decay_b_fwd, GLA_HEADS, hk).reshape(depth, GLA_HEADS, 1, hk)
    bdec_b = _pad_heads(gla_decay_b_bwd, GLA_HEADS, hk).reshape(depth, GLA_HEADS, 1, hk)
    norm_g = _pad_heads(gla_norm_g, GLA_HEADS, hv).reshape(depth, GLA_HEADS, 1, hv)

    w_out_b = w_out.astype(BF16)
    w_gla = jnp.pad(w_out_b[:, :gla_dv].reshape(depth, GLA_HEADS, hdv, d_model), ((0, 0), (0, 0), (0, hv - hdv), (0, 0)))
    w_out_p = jnp.concatenate([w_gla.reshape(depth, GLA_HEADS * hv, d_model), w_out_b[:, gla_dv:]], axis=1)

    ffn_src = ((ffn1_w_gate, ffn1_w_up, ffn1_w_down), (ffn2_w_gate, ffn2_w_up, ffn2_w_down))
    ffn_w = tuple(w[0].astype(BF16) for w in ffn_src[0])
    vec = lambda p: p[:, None, :]
    slopes = jnp.asarray(2.0 ** (-8.0 * np.arange(1, DIL_HEADS + 1) / DIL_HEADS), F32)

    xf = x.reshape(m, d_model)
    for l in range(depth):
        xf, ffn_w = _ffn_ln(xf, *ffn_w, vec(ln1_g), vec(ln1_b), l, (ffn_src[1], l), alpha=alpha, tm=tm)
        proj_a = _in_proj(xf, w_in_a, l, BF16, tm=tm, name="in_proj_gla")
        proj_b = _in_proj(xf, w_in_b, l, F32, tm=tm, name="in_proj_mix")
        gla_o = _gla(proj_a, wdec_f, wdec_b, bdec_f, bdec_b, norm_g, l, batch=batch, seq=seq, offs=offs_a,
                     hk=hk, hv=hv, q_scale=hdk ** -0.5, dv_head=hdv)
        dil_o = _dil(proj_b, slopes, batch=batch, seq=seq, offs=offs_b, hd=dil_hd)
        conv_o = _conv(proj_b, conv_w, vec(conv_b), vec(conv_ln_g), vec(conv_ln_b), l, batch=batch, seq=seq,
                       offs=offs_b)
        xf = _out_ln(xf, gla_o, dil_o, conv_o, w_out_p, vec(ln2_g), vec(ln2_b), l, alpha=alpha, tm=tm)
        nxt = (ffn_src[0], l + 1) if l + 1 < depth else None
        xf, ffn_w = _ffn_ln(xf, *ffn_w, vec(ln3_g), vec(ln3_b), l, nxt, alpha=alpha, tm=tm)
    return xf.reshape(batch, seq, d_model)
```

```python
import functools

import numpy as np
import jax
import jax.numpy as jnp
from jax import lax
from jax.experimental import pallas as pl
from jax.experimental.pallas import tpu as pltpu

GLA_HEADS = 4
GLA_TAU = 16.0
GLA_CHUNK = 64
DIL_HEADS = 6
DIL_BRANCHES = ((128, 1), (512, 4), (2048, 16))
LN_EPS = 1e-5
MASK_VALUE = -1e30

LANES = 128
SUBLANES = 8
BF16_ROWS = 16
MXU_WIDTH = 256
VMEM_LIMIT_BYTES = 56 * 1024 * 1024

F32 = jnp.float32
BF16 = jnp.bfloat16
_NT_DIMS = (((1,), (1,)), ((), ()))
_TN_DIMS = (((0,), (0,)), ((), ()))


def _round_up(n, m):
    return (n + m - 1) // m * m


def _layer_norm(y, g, b):
    mu = jnp.mean(y, axis=-1, keepdims=True)
    yc = y - mu
    var = jnp.mean(yc * yc, axis=-1, keepdims=True)
    return yc * lax.rsqrt(var + LN_EPS) * g + b


def _params(n_grid_axes):
    return pltpu.CompilerParams(dimension_semantics=("arbitrary",) * n_grid_axes,
                                vmem_limit_bytes=VMEM_LIMIT_BYTES)


def _layer_spec(shape, layer):
    zeros = (0,) * len(shape)
    return pl.BlockSpec((None,) + tuple(shape), lambda *_: (layer,) + zeros)


def _swiglu_part(xb, wg, wu, wd):
    gate = jnp.dot(xb, wg, preferred_element_type=F32)
    up = jnp.dot(xb, wu, preferred_element_type=F32)
    h = (gate * jax.nn.sigmoid(gate) * up).astype(BF16)
    return jnp.dot(h, wd, preferred_element_type=F32)


def _ffn_ln_kernel(x_ref, wg_ref, wu_ref, wd_ref, wgt_ref, wut_ref, wdt_ref, g_ref, b_ref, *rest, alpha, n_cast):
    src_refs, o_ref, dst_refs, xb_ref = rest[:n_cast], rest[n_cast], rest[n_cast + 1:-1], rest[-1]
    f = pl.program_id(1)

    @pl.when(f == 0)
    def _():
        xb = x_ref[...].astype(BF16)
        xb_ref[...] = xb
        o_ref[...] = _swiglu_part(xb, wgt_ref[...], wut_ref[...], wdt_ref[...])

    o_ref[...] += _swiglu_part(xb_ref[...], wg_ref[...], wu_ref[...], wd_ref[...])
    for src, dst in zip(src_refs, dst_refs):
        dst[...] = src[...].astype(BF16)

    @pl.when(f == pl.num_programs(1) - 1)
    def _():
        y = alpha * x_ref[...] + 0.5 * o_ref[...]
        o_ref[...] = _layer_norm(y, g_ref[...], b_ref[...])


def _ffn_tiles(d_ff):
    for tf in (3 * MXU_WIDTH, 2 * MXU_WIDTH, MXU_WIDTH):
        tail = d_ff % tf
        if tail and tail % LANES == 0 and (d_ff - tail) % tail == 0:
            return tf, tail
    raise NotImplementedError(f"no FFN tiling for d_ff={d_ff}")


def _slab_rows(rows, n_steps):
    for r in range(BF16_ROWS, rows + 1, BF16_ROWS):
        if rows % r == 0 and rows // r <= n_steps:
            return r
    raise NotImplementedError(f"cannot spread {rows} rows over {n_steps} steps")


def _ffn_ln(x, wg, wu, wd, g, b, layer, cast_next=None, *, alpha, tm):
    m, d = x.shape
    d_ff = wg.shape[1]
    tf, tail = _ffn_tiles(d_ff)
    n_f = d_ff // tf
    tail_blk = (d_ff - tail) // tail
    in_specs = [
        pl.BlockSpec((tm, d), lambda i, f: (i, 0)),
        pl.BlockSpec((d, tf), lambda i, f: (0, f)),
        pl.BlockSpec((d, tf), lambda i, f: (0, f)),
        pl.BlockSpec((tf, d), lambda i, f: (f, 0)),
        pl.BlockSpec((d, tail), lambda i, f: (0, tail_blk)),
        pl.BlockSpec((d, tail), lambda i, f: (0, tail_blk)),
        pl.BlockSpec((tail, d), lambda i, f: (tail_blk, 0)),
        _layer_spec((1, d), layer),
        _layer_spec((1, d), layer),
    ]
    out_specs = [pl.BlockSpec((tm, d), lambda i, f: (i, 0))]
    out_shape = [jax.ShapeDtypeStruct((m, d), F32)]
    args = [x, wg, wu, wd, wg, wu, wd, g, b]
    if cast_next is not None:
        srcs, nxt = cast_next
        n_steps = (m // tm) * n_f
        for w in srcs:
            rows, cols = w.shape[1:]
            r = _slab_rows(rows, n_steps)
            slab = lambda i, f, last=rows // r - 1: jnp.minimum(i * n_f + f, last)
            in_specs.append(pl.BlockSpec((None, r, cols), lambda i, f, slab=slab: (nxt, slab(i, f), 0)))
            out_specs.append(pl.BlockSpec((r, cols), lambda i, f, slab=slab: (slab(i, f), 0)))
            out_shape.append(jax.ShapeDtypeStruct((rows, cols), BF16))
            args.append(w)
    outs = pl.pallas_call(
        functools.partial(_ffn_ln_kernel, alpha=alpha, n_cast=len(args) - 9),
        grid=(m // tm, n_f),
        in_specs=in_specs,
        out_specs=out_specs,
        out_shape=out_shape,
        scratch_shapes=[pltpu.VMEM((tm, d), BF16)],
        compiler_params=_params(2),
        name="ffn_ln",
    )(*args)
    return outs[0], tuple(outs[1:])


def _in_proj_kernel(x_ref, *refs):
    w_refs, o_ref = refs[:-1], refs[-1]
    xb = x_ref[...].astype(BF16)
    col = 0
    for w_ref in w_refs:
        n = w_ref.shape[1]
        o_ref[:, col:col + n] = jnp.dot(xb, w_ref[...], preferred_element_type=F32).astype(o_ref.dtype)
        col += n


def _in_proj(x, ws, layer, out_dtype, *, tm, name):
    m, d = x.shape
    n = sum(w.shape[2] for w in ws)
    return pl.pallas_call(
        _in_proj_kernel,
        grid=(m // tm,),
        in_specs=[pl.BlockSpec((tm, d), lambda i: (i, 0))] + [
            pl.BlockSpec((None, d, w.shape[2]), lambda i: (layer, 0, 0), pipeline_mode=pl.Buffered(1)) for w in ws],
        out_specs=pl.BlockSpec((tm, n), lambda i: (i, 0)),
        out_shape=jax.ShapeDtypeStruct((m, n), out_dtype),
        compiler_params=_params(1),
        name=name,
    )(x, *ws)


def _regroup_kernel(w_ref, o_ref, *, runs):
    src = w_ref[...].astype(BF16)
    for blk, (start, width) in enumerate(runs):
        w0 = start // LANES * LANES
        k = 2 * LANES if (start + width - 1) // LANES > start // LANES else LANES
        rowi = lax.broadcasted_iota(jnp.int32, (k, LANES), 0)
        colj = lax.broadcasted_iota(jnp.int32, (k, LANES), 1)
        sel = ((rowi - colj == start - w0) & (colj < width)).astype(BF16)
        o_ref[:, blk * LANES:(blk + 1) * LANES] = jnp.dot(
            src[:, w0:w0 + k], sel, preferred_element_type=F32).astype(o_ref.dtype)


def _regroup(w, runs, *, rows):
    depth, d, _ = w.shape
    n_src = _round_up(max(s + wd for s, wd in runs), LANES)
    n_out = LANES * len(runs)
    return pl.pallas_call(
        functools.partial(_regroup_kernel, runs=tuple(runs)),
        grid=(depth, d // rows),
        in_specs=[pl.BlockSpec((None, rows, n_src), lambda l, i: (l, i, 0))],
        out_specs=pl.BlockSpec((None, rows, n_out), lambda l, i: (l, i, 0)),
        out_shape=jax.ShapeDtypeStruct((depth, d, n_out), BF16),
        compiler_params=_params(2),
        name="regroup_w_in",
    )(w)


def _log_sigmoid(z):
    return jnp.minimum(z, 0.0) - jnp.log(1.0 + jnp.exp(-jnp.abs(z)))


def _gla_kernel(q_ref, k_ref, v_ref, gg_ref, r_ref, wf_ref, wb_ref, bf_ref, bb_ref, ng_ref,
                o_ref, of_ref, ob_ref, st_ref, *, seq, tile, chunk, hk, hv, q_scale, inv_dv):
    n_tiles = seq // tile
    n_chunks = tile // chunk
    n_heads = q_ref.shape[-1] // hk
    shift = chunk.bit_length() - 1
    row = lax.broadcasted_iota(jnp.int32, (tile, tile), 0)
    col = lax.broadcasted_iota(jnp.int32, (tile, tile), 1)
    same_chunk = lax.shift_right_logical(row, shift) == lax.shift_right_logical(col, shift)
    tris = (same_chunk & (col <= row), same_chunk & (col >= row))
    oacc_refs = (of_ref, ob_ref)
    w_refs, b_refs = (wf_ref, wb_ref), (bf_ref, bb_ref)
    chains = [(h, dn) for h in range(n_heads) for dn in (0, 1)]
    every = range(len(chains))
    kcols = [slice(h * hk, (h + 1) * hk) for h, _ in chains]
    vcols = [slice(h * hv, (h + 1) * hv) for h, _ in chains]

    def step(i):
        rows = [pl.ds(pl.multiple_of((i if dn == 0 else n_tiles - 1 - i) * tile, tile), tile) for _, dn in chains]
        z = [jnp.dot(r_ref[rows[c], :], w_refs[dn][h], preferred_element_type=F32) + b_refs[dn][h]
             for c, (h, dn) in enumerate(chains)]
        la = [_log_sigmoid(zc) * (1.0 / GLA_TAU) for zc in z]
        la_hi = [x.astype(BF16) for x in la]
        la_lo = [(x - hi.astype(F32)).astype(BF16) for x, hi in zip(la, la_hi)]
        cum = [jnp.dot(tris[dn].astype(BF16), jnp.concatenate([la_hi[c], la_lo[c]], axis=1),
                       preferred_element_type=F32) for c, (_, dn) in enumerate(chains)]
        bcum = [x[:, :hk] + x[:, hk:] for x in cum]

        vb = [v_ref[rows[c], vcols[c]] for c in every]
        q_dec = [(q_ref[rows[c], kcols[c]].astype(F32) * q_scale * jnp.exp(bcum[c])).astype(BF16) for c in every]
        k_inv = [k_ref[rows[c], kcols[c]].astype(F32) * jnp.exp(-bcum[c]) for c in every]
        att = [lax.dot_general(q_dec[c], k_inv[c].astype(BF16), _NT_DIMS, preferred_element_type=F32) for c in every]
        att = [jnp.where(tris[dn], att[c], 0.0).astype(BF16) for c, (_, dn) in enumerate(chains)]
        o_intra = [jnp.dot(att[c], vb[c], preferred_element_type=F32) for c in every]

        o_inter = [[None] * n_chunks for _ in every]
        for s in range(n_chunks):
            for c, (_, dn) in enumerate(chains):
                ck = s if dn == 0 else n_chunks - 1 - s
                lo = ck * chunk
                end_row = lo + chunk - 1 if dn == 0 else lo
                e_end = jnp.exp(bcum[c][end_row:end_row + 1, :])
                st = st_ref[c]
                o_inter[c][ck] = lax.dot_general(q_dec[c][lo:lo + chunk], st.astype(BF16), _NT_DIMS,
                                                 preferred_element_type=F32)
                k_end = (k_inv[c][lo:lo + chunk] * e_end).astype(BF16)
                upd = lax.dot_general(vb[c][lo:lo + chunk], k_end, _TN_DIMS, preferred_element_type=F32)
                st_ref[c] = st * e_end + upd
        for c, (_, dn) in enumerate(chains):
            oacc_refs[dn][rows[c], vcols[c]] = o_intra[c] + jnp.concatenate(o_inter[c], axis=0)

    st_ref[...] = jnp.zeros_like(st_ref)

    def body(i, carry):
        step(i)
        return carry

    lax.fori_loop(0, n_tiles, body, 0)

    def finish(i, carry):
        rows = pl.ds(pl.multiple_of(i * tile, tile), tile)
        for h in range(n_heads):
            cols = slice(h * hv, (h + 1) * hv)
            o = of_ref[rows, cols] + ob_ref[rows, cols]
            ms = jnp.sum(o * o, axis=-1, keepdims=True) * inv_dv
            gate = gg_ref[rows, cols].astype(F32)
            o = o * lax.rsqrt(ms + LN_EPS) * ng_ref[h] * (gate * jax.nn.sigmoid(gate))
            o_ref[rows, cols] = o.astype(o_ref.dtype)
        return carry

    lax.fori_loop(0, n_tiles, finish, 0)


def _gla(proj, wdec_f, wdec_b, bdec_f, bdec_b, norm_g, layer, *, batch, seq, offs, hk, hv, q_scale, dv_head):
    heads = wdec_f.shape[1]
    hps = 2 if heads % 2 == 0 else 1
    tile = min(256, seq)
    kern = functools.partial(_gla_kernel, seq=seq, tile=tile, chunk=GLA_CHUNK, hk=hk, hv=hv, q_scale=q_scale,
                             inv_dv=1.0 / dv_head)
    wk, wv = hps * hk, hps * hv
    assert all(offs[n] % w == 0 for n, w in (("gq", wk), ("gk", wk), ("gv", wv), ("gg", wv)))
    cq, ck, cv, cg, cr = offs["gq"] // wk, offs["gk"] // wk, offs["gv"] // wv, offs["gg"] // wv, offs["r"] // LANES
    per_head = lambda rows, cols: pl.BlockSpec((None, hps, rows, cols), lambda b, g: (layer, g, 0, 0))
    return pl.pallas_call(
        kern,
        grid=(batch, heads // hps),
        in_specs=[
            pl.BlockSpec((seq, wk), lambda b, g: (b, cq + g)),
            pl.BlockSpec((seq, wk), lambda b, g: (b, ck + g)),
            pl.BlockSpec((seq, wv), lambda b, g: (b, cv + g)),
            pl.BlockSpec((seq, wv), lambda b, g: (b, cg + g)),
            pl.BlockSpec((seq, LANES), lambda b, g: (b, cr)),
            per_head(LANES, hk), per_head(LANES, hk),
            per_head(1, hk), per_head(1, hk), per_head(1, hv),
        ],
        out_specs=pl.BlockSpec((seq, wv), lambda b, g: (b, g)),
        out_shape=jax.ShapeDtypeStruct((batch * seq, heads * hv), BF16),
        scratch_shapes=[pltpu.VMEM((seq, wv), F32), pltpu.VMEM((seq, wv), F32),
                        pltpu.VMEM((2 * hps, hv, hk), F32)],
        compiler_params=_params(2),
        name="gla",
    )(proj, proj, proj, proj, proj, wdec_f, wdec_b, bdec_f, bdec_b, norm_g)


def _dil_cfgs(seq):
    cfgs = []
    for w, d in DIL_BRANCHES:
        reach = (w // 2) // d
        length = seq // d
        qb = min(128, length)
        kb = min(length, qb + 2 * reach)
        n_qb = length // qb
        rel_starts = sorted({min(max(jb * qb - reach, 0), length - kb) - jb * qb for jb in range(n_qb)},
                            reverse=True)
        cfgs.append((d, reach, length, qb, kb, n_qb, rel_starts))
    return cfgs


def _dil_kernel(slopes_ref, q_ref, k_ref, v_ref, o_ref, *stats, seq, scale):
    n_br = len(DIL_BRANCHES)
    m_refs, l_refs, acc_refs, bias_refs = (stats[i * n_br:(i + 1) * n_br] for i in range(4))
    qkv16 = [stats[4 * n_br + 3 * n:4 * n_br + 3 * n + 3] for n in range(n_br)]
    qkv32 = [None, *[stats[7 * n_br + 3 * (n - 1):7 * n_br + 3 * n] for n in range(1, n_br - 1)]]
    slope = slopes_ref[pl.program_id(1)]
    hd = q_ref.shape[-1]
    dils = [d for _, d in DIL_BRANCHES]

    cfgs = _dil_cfgs(seq)
    for n, (d, reach, length, qb, kb, n_qb, rel_starts) in enumerate(cfgs):
        rel = lax.broadcasted_iota(jnp.int32, (qb, kb), 1) - lax.broadcasted_iota(jnp.int32, (qb, kb), 0)
        for vi, rs in enumerate(rel_starts):
            dist = jnp.abs(rel + rs)
            bias_refs[n][vi] = jnp.where(dist <= reach, -(slope * float(d)) * dist.astype(F32), MASK_VALUE)

    rt = min(256, seq)
    srcs = (q_ref, k_ref, v_ref)
    for a in range(3):
        for t0 in range(0, seq, rt):
            x = srcs[a][t0:t0 + rt, :]
            qkv16[0][a][t0:t0 + rt, :] = (x * scale if a == 0 else x).astype(BF16)
    for n in range(1, n_br):
        assert dils[n] % dils[n - 1] == 0
        step, l_prev, l_cur = dils[n] // dils[n - 1], seq // dils[n - 1], seq // dils[n]
        for a in range(3):
            src = srcs[a] if n == 1 else qkv32[n - 1][a]
            for bp in range(dils[n - 1]):
                for r2 in range(step):
                    for t0 in range(0, l_cur, rt):
                        rows = min(rt, l_cur - t0)
                        x = src[pl.ds(bp * l_prev + r2 + step * t0, rows, stride=step), :]
                        dst = (bp * step + r2) * l_cur + t0
                        if n < n_br - 1:
                            qkv32[n][a][dst:dst + rows, :] = x
                        qkv16[n][a][dst:dst + rows, :] = (x * scale if a == 0 else x).astype(BF16)

    def _residue(n, b):
        if n == 0:
            return 0
        step = dils[n] // dils[n - 1]
        return lax.rem(b, jnp.int32(step)) * dils[n - 1] + _residue(n - 1, lax.div(b, jnp.int32(step)))

    def blocks(jobs):
        place = []
        for n, idx in jobs:
            d, reach, length, qb, kb, n_qb, rel_starts = cfgs[n]
            if d == 1:
                blk, jb = 0, idx
            else:
                blk, jb = lax.div(idx, jnp.int32(n_qb)), lax.rem(idx, jnp.int32(n_qb))
            i0 = jb * qb
            ks = jnp.clip(i0 - reach, 0, length - kb)
            variant = sum(((ks - i0) <= rs).astype(jnp.int32) for rs in rel_starts[1:])
            base = blk * length
            qrows = pl.ds(pl.multiple_of(base + i0, qb), qb)
            krows = pl.ds(pl.multiple_of(base + ks, 16), kb)
            orows = qrows if d == 1 else pl.ds(_residue(n, blk) + d * i0, qb, stride=d)
            place.append((n, qrows, krows, orows, variant, qb, kb))
        s = []
        for n, qrows, krows, orows, variant, qb, kb in place:
            s.append(lax.dot_general(qkv16[n][0][qrows, :], qkv16[n][1][krows, :], _NT_DIMS,
                                     preferred_element_type=F32) + bias_refs[n][variant])
        m = [jnp.max(sj, axis=-1, keepdims=True) for sj in s]
        p = [jnp.exp(sj - mj).astype(BF16) for sj, mj in zip(s, m)]
        for (n, qrows, krows, orows, variant, qb, kb), mj, pj in zip(place, m, p):
            v_ones = jnp.concatenate([qkv16[n][2][krows, :], jnp.ones((kb, hd), BF16)], axis=1)
            acc_l = jnp.dot(pj, v_ones, preferred_element_type=F32)
            acc_refs[n][orows, :] = acc_l[:, :hd]
            l_refs[n][orows, :] = acc_l[:, hd:]
            m_refs[n][orows, :] = jnp.broadcast_to(mj, (qb, hd))

    counts = [seq // min(128, seq // d) for _, d in DIL_BRANCHES]
    group = 4
    if len(set(counts)) == 1 and counts[0] % group == 0:
        def body(i, carry):
            blocks([(n, group * i + u) for u in range(group) for n in range(n_br)])
            return carry
        lax.fori_loop(0, counts[0] // group, body, 0)
    else:
        for n in range(n_br):
            def body(idx, carry, n=n):
                blocks([(n, idx)])
                return carry
            lax.fori_loop(0, counts[n], body, 0)

    rt = min(256, seq)

    def finish(i, carry):
        rows = pl.ds(pl.multiple_of(i * rt, rt), rt)
        ms = [m_ref[rows, :] for m_ref in m_refs]
        m_all = functools.reduce(jnp.maximum, ms)
        num = jnp.zeros((rt, hd), F32)
        den = jnp.zeros((rt, hd), F32)
        for n in range(n_br):
            wgt = jnp.exp(ms[n] - m_all)
            num += wgt * acc_refs[n][rows, :]
            den += wgt * l_refs[n][rows, :]
        o_ref[rows, :] = (num / den).astype(o_ref.dtype)
        return carry

    lax.fori_loop(0, seq // rt, finish, 0)


def _dil(proj, slopes, *, batch, seq, offs, hd):
    cq, ck, cv = offs["dq"] // hd, offs["dk"] // hd, offs["dv"] // hd
    kern = functools.partial(_dil_kernel, seq=seq, scale=hd ** -0.5)
    return pl.pallas_call(
        kern,
        grid=(batch, DIL_HEADS),
        in_specs=[
            pl.BlockSpec(memory_space=pltpu.SMEM),
            pl.BlockSpec((seq, hd), lambda b, h: (b, cq + h)),
            pl.BlockSpec((seq, hd), lambda b, h: (b, ck + h)),
            pl.BlockSpec((seq, hd), lambda b, h: (b, cv + h)),
        ],
        out_specs=pl.BlockSpec((seq, hd), lambda b, h: (b, h)),
        out_shape=jax.ShapeDtypeStruct((batch * seq, DIL_HEADS * hd), BF16),
        scratch_shapes=([pltpu.VMEM((seq, hd), F32)] * (3 * len(DIL_BRANCHES))
                        + [pltpu.VMEM((len(c[6]), c[3], c[4]), F32) for c in _dil_cfgs(seq)]
                        + [pltpu.VMEM((seq, hd), BF16)] * (3 * len(DIL_BRANCHES))
                        + [pltpu.VMEM((seq, hd), F32)] * (3 * (len(DIL_BRANCHES) - 2))),
        compiler_params=_params(2),
        name="dil",
    )(slopes, proj, proj, proj)


def _conv_kernel(prev_ref, cur_ref, next_ref, w_ref, cb_ref, g_ref, b_ref, o_ref, u_ref, *, halo, taps):
    i = pl.program_id(1)
    c = o_ref.shape[-1]
    tq = o_ref.shape[0]

    def glu(ref):
        x = ref[...]
        return x[:, :c] * jax.nn.sigmoid(x[:, c:])

    ext = tq + 2 * halo
    u_ref[0, 0:halo, :] = jnp.where(i > 0, glu(prev_ref), 0.0)
    u_ref[0, halo:halo + tq, :] = glu(cur_ref)
    u_ref[0, halo + tq:, :] = jnp.where(i < pl.num_programs(1) - 1, glu(next_ref), 0.0)
    for s in range(1, SUBLANES):
        u_ref[s, 0:ext - SUBLANES, :] = u_ref[0, pl.ds(s, ext - SUBLANES), :]

    pad = taps // 2
    rc = min(32, tq)
    first = halo - pad
    span = (first + taps - 1) // SUBLANES * SUBLANES

    def chunk(j, carry):
        r0 = pl.multiple_of(j * rc, rc)
        acc = jnp.zeros((rc, c), F32) + cb_ref[...]
        for s in range(SUBLANES):
            win = u_ref[s, pl.ds(r0, rc + span), :]
            for t in range(taps):
                a, ts = divmod(first + t, SUBLANES)
                if ts == s:
                    acc = acc + win[a * SUBLANES:a * SUBLANES + rc] * w_ref[t:t + 1, :]
        y = _layer_norm(acc, g_ref[...], b_ref[...])
        o_ref[pl.ds(r0, rc), :] = (y * jax.nn.sigmoid(y)).astype(o_ref.dtype)
        return carry

    lax.fori_loop(0, tq // rc, chunk, 0, unroll=2)


def _conv(proj, conv_w, conv_b, ln_g, ln_b, layer, *, batch, seq, offs):
    taps, c = conv_w.shape[1:]
    halo = 16
    assert taps // 2 <= halo and halo % SUBLANES == 0
    tq = min(512, seq)
    nt = seq // tq
    hb = tq // halo
    ccol = offs["cv"] // (2 * c)
    kern = functools.partial(_conv_kernel, halo=halo, taps=taps)
    return pl.pallas_call(
        kern,
        grid=(batch, nt),
        in_specs=[
            pl.BlockSpec((halo, 2 * c), lambda b, i: (jnp.maximum((b * nt + i) * hb - 1, 0), ccol)),
            pl.BlockSpec((tq, 2 * c), lambda b, i: (b * nt + i, ccol)),
            pl.BlockSpec((halo, 2 * c),
                         lambda b, i: (jnp.minimum((b * nt + i + 1) * hb, batch * nt * hb - 1), ccol)),
            _layer_spec((taps, c), layer),
            _layer_spec((1, c), layer),
            _layer_spec((1, c), layer),
            _layer_spec((1, c), layer),
        ],
        out_specs=pl.BlockSpec((tq, c), lambda b, i: (b * nt + i, 0)),
        out_shape=jax.ShapeDtypeStruct((batch * seq, c), BF16),
        scratch_shapes=[pltpu.VMEM((SUBLANES, tq + 2 * halo, c), F32)],
        compiler_params=_params(2),
        name="conv",
    )(proj, proj, proj, conv_w, conv_b, ln_g, ln_b)


def _out_ln_kernel(x_ref, a_ref, d_ref, c_ref, w_ref, g_ref, b_ref, o_ref, *, alpha, heads, hdv):
    hv, nd = a_ref.shape[1] // heads, d_ref.shape[1]
    na = heads * hdv
    mix = jnp.dot(d_ref[...], w_ref[na:na + nd, :], preferred_element_type=F32)
    mix += jnp.dot(c_ref[...], w_ref[na + nd:, :], preferred_element_type=F32)
    for h in range(heads):
        mix += jnp.dot(a_ref[:, h * hv:(h + 1) * hv], w_ref[h * hdv:h * hdv + hv, :], preferred_element_type=F32)
    o_ref[...] = _layer_norm(alpha * x_ref[...] + mix, g_ref[...], b_ref[...])


def _out_ln(x, gla_o, dil_o, conv_o, w, g, b, layer, *, alpha, tm, heads, hdv):
    m, d = x.shape
    row = lambda i: (i, 0)
    assert (heads - 1) * hdv + gla_o.shape[1] // heads <= w.shape[1] and hdv % BF16_ROWS == 0
    return pl.pallas_call(
        functools.partial(_out_ln_kernel, alpha=alpha, heads=heads, hdv=hdv),
        grid=(m // tm,),
        in_specs=[
            pl.BlockSpec((tm, d), row),
            pl.BlockSpec((tm, gla_o.shape[1]), row),
            pl.BlockSpec((tm, dil_o.shape[1]), row),
            pl.BlockSpec((tm, conv_o.shape[1]), row),
            _layer_spec(w.shape[1:], layer),
            _layer_spec((1, d), layer),
            _layer_spec((1, d), layer),
        ],
        out_specs=pl.BlockSpec((tm, d), row),
        out_shape=jax.ShapeDtypeStruct((m, d), F32),
        compiler_params=_params(1),
        name="out_ln",
    )(x, gla_o, dil_o, conv_o, w, g, b)


def _pad_heads(w, heads, width):
    lead = w.shape[:-1]
    hd = w.shape[-1] // heads
    w = w.reshape(lead + (heads, hd))
    w = jnp.pad(w, [(0, 0)] * len(lead) + [(0, 0), (0, width - hd)])
    return w.reshape(lead + (heads * width,))


def kernel(x, ffn1_w_gate, ffn1_w_up, ffn1_w_down, ln1_g, ln1_b, w_in, gla_decay_w_fwd, gla_decay_b_fwd, gla_decay_w_bwd, gla_decay_b_bwd, gla_norm_g, conv_w, conv_b, conv_ln_g, conv_ln_b, w_out, ln2_g, ln2_b, ffn2_w_gate, ffn2_w_up, ffn2_w_down, ln3_g, ln3_b):
    batch, seq, d_model = x.shape
    depth = ffn1_w_gate.shape[0]
    rank, gla_dk = gla_decay_w_fwd.shape[1:]
    gla_dv = gla_norm_g.shape[1]
    conv_c = conv_w.shape[2]
    dil_dim = w_out.shape[1] - gla_dv - conv_c
    dil_hd = dil_dim // DIL_HEADS
    hdk, hdv = gla_dk // GLA_HEADS, gla_dv // GLA_HEADS
    hk, hv = _round_up(hdk, LANES), _round_up(hdv, LANES)
    assert dil_hd == LANES and 2 * rank <= LANES and seq % GLA_CHUNK == 0

    alpha = (2.0 * depth) ** 0.25
    m = batch * seq
    tm = min(512, m)

    def layout(widths):
        offs, total = {}, 0
        for name, wdt in widths:
            offs[name] = total
            total += wdt
        return offs

    offs_a = layout((("gq", GLA_HEADS * hk), ("gk", GLA_HEADS * hk), ("gv", GLA_HEADS * hv),
                     ("gg", GLA_HEADS * hv), ("r", LANES)))
    offs_b = layout((("cv", 2 * conv_c), ("dq", dil_dim), ("dk", dil_dim), ("dv", dil_dim)))

    src = layout((("gq", gla_dk), ("gk", gla_dk), ("gv", gla_dv), ("r", 2 * rank), ("gg", gla_dv),
                  ("dil", 3 * dil_dim), ("cv", 2 * conv_c)))
    runs = []
    for name, hd in (("gq", hdk), ("gk", hdk), ("gv", hdv), ("gg", hdv)):
        for h in range(GLA_HEADS):
            runs += [(src[name] + h * hd + c0, min(LANES, hd - c0)) for c0 in range(0, _round_up(hd, LANES), LANES)]
    runs.append((src["r"], 2 * rank))
    assert len(runs) * LANES == offs_a["r"] + LANES
    w_in_a = (_regroup(w_in, runs, rows=min(256, d_model)),)
    w_in_b = (w_in[:, :, src["cv"]:].astype(BF16), w_in[:, :, src["dil"]:src["cv"]].astype(BF16))

    def head_mats(wdec, row0):
        w4 = _pad_heads(wdec, GLA_HEADS, hk).reshape(depth, rank, GLA_HEADS, hk).transpose(0, 2, 1, 3)
        return jnp.pad(w4, ((0, 0), (0, 0), (row0, LANES - rank - row0), (0, 0))).astype(BF16)

    wdec_f, wdec_b = head_mats(gla_decay_w_fwd, 0), head_mats(gla_decay_w_bwd, rank)
    bdec_f = _pad_heads(gla_decay_b_fwd, GLA_HEADS, hk).reshape(depth, GLA_HEADS, 1, hk)
    bdec_b = _pad_heads(gla_decay_b_bwd, GLA_HEADS, hk).reshape(depth, GLA_HEADS, 1, hk)
    norm_g = _pad_heads(gla_norm_g, GLA_HEADS, hv).reshape(depth, GLA_HEADS, 1, hv)

    w_out_b = w_out.astype(BF16)

    ffn_src = ((ffn1_w_gate, ffn1_w_up, ffn1_w_down), (ffn2_w_gate, ffn2_w_up, ffn2_w_down))
    ffn_w = tuple(w[0].astype(BF16) for w in ffn_src[0])
    vec = lambda p: p[:, None, :]
    slopes = jnp.asarray(2.0 ** (-8.0 * np.arange(1, DIL_HEADS + 1) / DIL_HEADS), F32)

    xf = x.reshape(m, d_model)
    for l in range(depth):
        xf, ffn_w = _ffn_ln(xf, *ffn_w, vec(ln1_g), vec(ln1_b), l, (ffn_src[1], l), alpha=alpha, tm=tm)
        proj_a = _in_proj(xf, w_in_a, l, BF16, tm=tm, name="in_proj_gla")
        proj_b = _in_proj(xf, w_in_b, l, F32, tm=tm, name="in_proj_mix")
        gla_o = _gla(proj_a, wdec_f, wdec_b, bdec_f, bdec_b, norm_g, l, batch=batch, seq=seq, offs=offs_a,
                     hk=hk, hv=hv, q_scale=hdk ** -0.5, dv_head=hdv)
        dil_o = _dil(proj_b, slopes, batch=batch, seq=seq, offs=offs_b, hd=dil_hd)
        conv_o = _conv(proj_b, conv_w, vec(conv_b), vec(conv_ln_g), vec(conv_ln_b), l, batch=batch, seq=seq,
                       offs=offs_b)
        xf = _out_ln(xf, gla_o, dil_o, conv_o, w_out_b, vec(ln2_g), vec(ln2_b), l, alpha=alpha, tm=tm,
                     heads=GLA_HEADS, hdv=hdv)
        nxt = (ffn_src[0], l + 1) if l + 1 < depth else None
        xf, ffn_w = _ffn_ln(xf, *ffn_w, vec(ln3_g), vec(ln3_b), l, nxt, alpha=alpha, tm=tm)
    return xf.reshape(batch, seq, d_model)
```

```python
import functools

import numpy as np
import jax
import jax.numpy as jnp
from jax import lax
from jax.experimental import pallas as pl
from jax.experimental.pallas import tpu as pltpu

GLA_HEADS = 4
GLA_TAU = 16.0
GLA_CHUNK = 64
DIL_HEADS = 6
DIL_BRANCHES = ((128, 1), (512, 4), (2048, 16))
LN_EPS = 1e-5
MASK_VALUE = -1e30

LANES = 128
SUBLANES = 8
BF16_ROWS = 16
MXU_WIDTH = 256
VMEM_LIMIT_BYTES = 56 * 1024 * 1024

F32 = jnp.float32
BF16 = jnp.bfloat16
_NT_DIMS = (((1,), (1,)), ((), ()))
_TN_DIMS = (((0,), (0,)), ((), ()))


def _round_up(n, m):
    return (n + m - 1) // m * m


def _layer_norm(y, g, b):
    mu = jnp.mean(y, axis=-1, keepdims=True)
    yc = y - mu
    var = jnp.mean(yc * yc, axis=-1, keepdims=True)
    return yc * lax.rsqrt(var + LN_EPS) * g + b


def _params(n_grid_axes):
    return pltpu.CompilerParams(dimension_semantics=("arbitrary",) * n_grid_axes,
                                vmem_limit_bytes=VMEM_LIMIT_BYTES)


def _layer_spec(shape, layer):
    zeros = (0,) * len(shape)
    return pl.BlockSpec((None,) + tuple(shape), lambda *_: (layer,) + zeros)


def _swiglu_part(xb, wg, wu, wd):
    gate = jnp.dot(xb, wg, preferred_element_type=F32)
    up = jnp.dot(xb, wu, preferred_element_type=F32)
    h = (gate * jax.nn.sigmoid(gate) * up).astype(BF16)
    return jnp.dot(h, wd, preferred_element_type=F32)


def _ffn_ln_kernel(x_ref, wg_ref, wu_ref, wd_ref, wgt_ref, wut_ref, wdt_ref, g_ref, b_ref, *rest, alpha, n_cast):
    src_refs, o_ref, dst_refs, xb_ref = rest[:n_cast], rest[n_cast], rest[n_cast + 1:-1], rest[-1]
    f = pl.program_id(1)

    @pl.when(f == 0)
    def _():
        xb = x_ref[...].astype(BF16)
        xb_ref[...] = xb
        o_ref[...] = _swiglu_part(xb, wgt_ref[...], wut_ref[...], wdt_ref[...])

    o_ref[...] += _swiglu_part(xb_ref[...], wg_ref[...], wu_ref[...], wd_ref[...])
    for src, dst in zip(src_refs, dst_refs):
        dst[...] = src[...].astype(BF16)

    @pl.when(f == pl.num_programs(1) - 1)
    def _():
        y = alpha * x_ref[...] + 0.5 * o_ref[...]
        o_ref[...] = _layer_norm(y, g_ref[...], b_ref[...])


def _ffn_tiles(d_ff):
    for tf in (3 * MXU_WIDTH, 2 * MXU_WIDTH, MXU_WIDTH):
        tail = d_ff % tf
        if tail and tail % LANES == 0 and (d_ff - tail) % tail == 0:
            return tf, tail
    raise NotImplementedError(f"no FFN tiling for d_ff={d_ff}")


def _slab_rows(rows, n_steps):
    for r in range(BF16_ROWS, rows + 1, BF16_ROWS):
        if rows % r == 0 and rows // r <= n_steps:
            return r
    raise NotImplementedError(f"cannot spread {rows} rows over {n_steps} steps")


def _ffn_ln(x, wg, wu, wd, g, b, layer, cast_next=None, *, alpha, tm):
    m, d = x.shape
    d_ff = wg.shape[1]
    tf, tail = _ffn_tiles(d_ff)
    n_f = d_ff // tf
    tail_blk = (d_ff - tail) // tail
    in_specs = [
        pl.BlockSpec((tm, d), lambda i, f: (i, 0)),
        pl.BlockSpec((d, tf), lambda i, f: (0, f)),
        pl.BlockSpec((d, tf), lambda i, f: (0, f)),
        pl.BlockSpec((tf, d), lambda i, f: (f, 0)),
        pl.BlockSpec((d, tail), lambda i, f: (0, tail_blk)),
        pl.BlockSpec((d, tail), lambda i, f: (0, tail_blk)),
        pl.BlockSpec((tail, d), lambda i, f: (tail_blk, 0)),
        _layer_spec((1, d), layer),
        _layer_spec((1, d), layer),
    ]
    out_specs = [pl.BlockSpec((tm, d), lambda i, f: (i, 0))]
    out_shape = [jax.ShapeDtypeStruct((m, d), F32)]
    args = [x, wg, wu, wd, wg, wu, wd, g, b]
    if cast_next is not None:
        srcs, nxt = cast_next
        n_steps = (m // tm) * n_f
        for w in srcs:
            rows, cols = w.shape[1:]
            r = _slab_rows(rows, n_steps)
            slab = lambda i, f, last=rows // r - 1: jnp.minimum(i * n_f + f, last)
            in_specs.append(pl.BlockSpec((None, r, cols), lambda i, f, slab=slab: (nxt, slab(i, f), 0)))
            out_specs.append(pl.BlockSpec((r, cols), lambda i, f, slab=slab: (slab(i, f), 0)))
            out_shape.append(jax.ShapeDtypeStruct((rows, cols), BF16))
            args.append(w)
    outs = pl.pallas_call(
        functools.partial(_ffn_ln_kernel, alpha=alpha, n_cast=len(args) - 9),
        grid=(m // tm, n_f),
        in_specs=in_specs,
        out_specs=out_specs,
        out_shape=out_shape,
        scratch_shapes=[pltpu.VMEM((tm, d), BF16)],
        compiler_params=_params(2),
        name="ffn_ln",
    )(*args)
    return outs[0], tuple(outs[1:])


def _in_proj_kernel(x_ref, *refs):
    w_refs, o_ref = refs[:-1], refs[-1]
    xb = x_ref[...].astype(BF16)
    col = 0
    for w_ref in w_refs:
        n = w_ref.shape[1]
        o_ref[:, col:col + n] = jnp.dot(xb, w_ref[...], preferred_element_type=F32).astype(o_ref.dtype)
        col += n


def _in_proj(x, ws, layer, out_dtype, *, tm, name):
    m, d = x.shape
    n = sum(w.shape[2] for w in ws)
    return pl.pallas_call(
        _in_proj_kernel,
        grid=(m // tm,),
        in_specs=[pl.BlockSpec((tm, d), lambda i: (i, 0))] + [
            pl.BlockSpec((None, d, w.shape[2]), lambda i: (layer, 0, 0), pipeline_mode=pl.Buffered(1)) for w in ws],
        out_specs=pl.BlockSpec((tm, n), lambda i: (i, 0)),
        out_shape=jax.ShapeDtypeStruct((m, n), out_dtype),
        compiler_params=_params(1),
        name=name,
    )(x, *ws)


def _regroup_kernel(w_ref, tail_ref, *o_refs, runs):
    n_main = w_ref.shape[1] // LANES
    tiles = {}

    def tile(j):
        if j not in tiles:
            src = tail_ref[...] if j == n_main else w_ref[:, j * LANES:(j + 1) * LANES]
            tiles[j] = src.astype(BF16)
        return tiles[j]

    for o_ref, o_runs in zip(o_refs, runs):
        for blk, (start, width) in enumerate(o_runs):
            j0, j1 = start // LANES, (start + width - 1) // LANES
            window = tile(j0) if j1 == j0 else jnp.concatenate([tile(j0), tile(j1)], axis=1)
            k = window.shape[1]
            rowi = lax.broadcasted_iota(jnp.int32, (k, LANES), 0)
            colj = lax.broadcasted_iota(jnp.int32, (k, LANES), 1)
            sel = ((rowi - colj == start - j0 * LANES) & (colj < width)).astype(BF16)
            o_ref[:, blk * LANES:(blk + 1) * LANES] = jnp.dot(
                window, sel, preferred_element_type=F32).astype(o_ref.dtype)


def _regroup(w, runs, *, rows):
    depth, d, n = w.shape
    n_main = n // LANES * LANES
    assert all(s + wd <= n for o_runs in runs for s, wd in o_runs) and n - n_main > 0
    tail = jnp.pad(w[:, :, n_main:], ((0, 0), (0, 0), (0, LANES - (n - n_main))))
    return pl.pallas_call(
        functools.partial(_regroup_kernel, runs=tuple(tuple(r) for r in runs)),
        grid=(depth, d // rows),
        in_specs=[pl.BlockSpec((None, rows, n_main), lambda l, i: (l, i, 0)),
                  pl.BlockSpec((None, rows, LANES), lambda l, i: (l, i, 0))],
        out_specs=[pl.BlockSpec((None, rows, LANES * len(r)), lambda l, i: (l, i, 0)) for r in runs],
        out_shape=[jax.ShapeDtypeStruct((depth, d, LANES * len(r)), BF16) for r in runs],
        compiler_params=_params(2),
        name="regroup_w_in",
    )(w, tail)


def _log_sigmoid(z):
    return jnp.minimum(z, 0.0) - jnp.log(1.0 + jnp.exp(-jnp.abs(z)))


def _gla_kernel(q_ref, k_ref, v_ref, gg_ref, r_ref, wf_ref, wb_ref, bf_ref, bb_ref, ng_ref,
                o_ref, of_ref, ob_ref, st_ref, *, seq, tile, chunk, hk, hv, q_scale, inv_dv):
    n_tiles = seq // tile
    n_chunks = tile // chunk
    n_heads = q_ref.shape[-1] // hk
    shift = chunk.bit_length() - 1
    row = lax.broadcasted_iota(jnp.int32, (tile, tile), 0)
    col = lax.broadcasted_iota(jnp.int32, (tile, tile), 1)
    same_chunk = lax.shift_right_logical(row, shift) == lax.shift_right_logical(col, shift)
    tris = (same_chunk & (col <= row), same_chunk & (col >= row))
    oacc_refs = (of_ref, ob_ref)
    w_refs, b_refs = (wf_ref, wb_ref), (bf_ref, bb_ref)
    chains = [(h, dn) for h in range(n_heads) for dn in (0, 1)]
    every = range(len(chains))
    kcols = [slice(h * hk, (h + 1) * hk) for h, _ in chains]
    vcols = [slice(h * hv, (h + 1) * hv) for h, _ in chains]

    def step(i):
        rows = [pl.ds(pl.multiple_of((i if dn == 0 else n_tiles - 1 - i) * tile, tile), tile) for _, dn in chains]
        z = [jnp.dot(r_ref[rows[c], :], w_refs[dn][h], preferred_element_type=F32) + b_refs[dn][h]
             for c, (h, dn) in enumerate(chains)]
        la = [_log_sigmoid(zc) * (1.0 / GLA_TAU) for zc in z]
        la_hi = [x.astype(BF16) for x in la]
        la_lo = [(x - hi.astype(F32)).astype(BF16) for x, hi in zip(la, la_hi)]
        cum = [jnp.dot(tris[dn].astype(BF16), jnp.concatenate([la_hi[c], la_lo[c]], axis=1),
                       preferred_element_type=F32) for c, (_, dn) in enumerate(chains)]
        bcum = [x[:, :hk] + x[:, hk:] for x in cum]

        vb = [v_ref[rows[c], vcols[c]] for c in every]
        q_dec = [(q_ref[rows[c], kcols[c]].astype(F32) * q_scale * jnp.exp(bcum[c])).astype(BF16) for c in every]
        k_inv = [k_ref[rows[c], kcols[c]].astype(F32) * jnp.exp(-bcum[c]) for c in every]
        att = [lax.dot_general(q_dec[c], k_inv[c].astype(BF16), _NT_DIMS, preferred_element_type=F32) for c in every]
        att = [jnp.where(tris[dn], att[c], 0.0).astype(BF16) for c, (_, dn) in enumerate(chains)]
        o_intra = [jnp.dot(att[c], vb[c], preferred_element_type=F32) for c in every]

        o_inter = [[None] * n_chunks for _ in every]
        for s in range(n_chunks):
            for c, (_, dn) in enumerate(chains):
                ck = s if dn == 0 else n_chunks - 1 - s
                lo = ck * chunk
                end_row = lo + chunk - 1 if dn == 0 else lo
                e_end = jnp.exp(bcum[c][end_row:end_row + 1, :])
                st = st_ref[c]
                o_inter[c][ck] = lax.dot_general(q_dec[c][lo:lo + chunk], st.astype(BF16), _NT_DIMS,
                                                 preferred_element_type=F32)
                k_end = (k_inv[c][lo:lo + chunk] * e_end).astype(BF16)
                upd = lax.dot_general(vb[c][lo:lo + chunk], k_end, _TN_DIMS, preferred_element_type=F32)
                st_ref[c] = st * e_end + upd
        for c, (_, dn) in enumerate(chains):
            oacc_refs[dn][rows[c], vcols[c]] = o_intra[c] + jnp.concatenate(o_inter[c], axis=0)

    st_ref[...] = jnp.zeros_like(st_ref)

    def body(i, carry):
        step(i)
        return carry

    lax.fori_loop(0, n_tiles, body, 0)

    def finish(i, carry):
        rows = pl.ds(pl.multiple_of(i * tile, tile), tile)
        for h in range(n_heads):
            cols = slice(h * hv, (h + 1) * hv)
            o = of_ref[rows, cols] + ob_ref[rows, cols]
            ms = jnp.sum(o * o, axis=-1, keepdims=True) * inv_dv
            gate = gg_ref[rows, cols].astype(F32)
            o = o * lax.rsqrt(ms + LN_EPS) * ng_ref[h] * (gate * jax.nn.sigmoid(gate))
            o_ref[rows, cols] = o.astype(o_ref.dtype)
        return carry

    lax.fori_loop(0, n_tiles, finish, 0)


def _gla(proj, wdec_f, wdec_b, bdec_f, bdec_b, norm_g, layer, *, batch, seq, offs, hk, hv, q_scale, dv_head):
    heads = wdec_f.shape[1]
    hps = 2 if heads % 2 == 0 else 1
    tile = min(256, seq)
    kern = functools.partial(_gla_kernel, seq=seq, tile=tile, chunk=GLA_CHUNK, hk=hk, hv=hv, q_scale=q_scale,
                             inv_dv=1.0 / dv_head)
    wk, wv = hps * hk, hps * hv
    assert all(offs[n] % w == 0 for n, w in (("gq", wk), ("gk", wk), ("gv", wv), ("gg", wv)))
    cq, ck, cv, cg, cr = offs["gq"] // wk, offs["gk"] // wk, offs["gv"] // wv, offs["gg"] // wv, offs["r"] // LANES
    per_head = lambda rows, cols: pl.BlockSpec((None, hps, rows, cols), lambda b, g: (layer, g, 0, 0))
    return pl.pallas_call(
        kern,
        grid=(batch, heads // hps),
        in_specs=[
            pl.BlockSpec((seq, wk), lambda b, g: (b, cq + g)),
            pl.BlockSpec((seq, wk), lambda b, g: (b, ck + g)),
            pl.BlockSpec((seq, wv), lambda b, g: (b, cv + g)),
            pl.BlockSpec((seq, wv), lambda b, g: (b, cg + g)),
            pl.BlockSpec((seq, LANES), lambda b, g: (b, cr)),
            per_head(LANES, hk), per_head(LANES, hk),
            per_head(1, hk), per_head(1, hk), per_head(1, hv),
        ],
        out_specs=pl.BlockSpec((seq, wv), lambda b, g: (b, g)),
        out_shape=jax.ShapeDtypeStruct((batch * seq, heads * hv), BF16),
        scratch_shapes=[pltpu.VMEM((seq, wv), F32), pltpu.VMEM((seq, wv), F32),
                        pltpu.VMEM((2 * hps, hv, hk), F32)],
        compiler_params=_params(2),
        name="gla",
    )(proj, proj, proj, proj, proj, wdec_f, wdec_b, bdec_f, bdec_b, norm_g)


def _dil_cfgs(seq):
    cfgs = []
    for w, d in DIL_BRANCHES:
        reach = (w // 2) // d
        length = seq // d
        qb = min(128, length)
        kb = min(length, qb + 2 * reach)
        n_qb = length // qb
        rel_starts = sorted({min(max(jb * qb - reach, 0), length - kb) - jb * qb for jb in range(n_qb)},
                            reverse=True)
        cfgs.append((d, reach, length, qb, kb, n_qb, rel_starts))
    return cfgs


def _dil_kernel(slopes_ref, q_ref, k_ref, v_ref, o_ref, *stats, seq, scale):
    n_br = len(DIL_BRANCHES)
    m_refs, l_refs, acc_refs, bias_refs = (stats[i * n_br:(i + 1) * n_br] for i in range(4))
    qkv16 = [stats[4 * n_br + 3 * n:4 * n_br + 3 * n + 3] for n in range(n_br)]
    qkv32 = [None, *[stats[7 * n_br + 3 * (n - 1):7 * n_br + 3 * n] for n in range(1, n_br - 1)]]
    slope = slopes_ref[pl.program_id(1)]
    hd = q_ref.shape[-1]
    dils = [d for _, d in DIL_BRANCHES]

    cfgs = _dil_cfgs(seq)
    for n, (d, reach, length, qb, kb, n_qb, rel_starts) in enumerate(cfgs):
        rel = lax.broadcasted_iota(jnp.int32, (qb, kb), 1) - lax.broadcasted_iota(jnp.int32, (qb, kb), 0)
        for vi, rs in enumerate(rel_starts):
            dist = jnp.abs(rel + rs)
            bias_refs[n][vi] = jnp.where(dist <= reach, -(slope * float(d)) * dist.astype(F32), MASK_VALUE)

    rt = min(256, seq)
    srcs = (q_ref, k_ref, v_ref)
    for a in range(3):
        for t0 in range(0, seq, rt):
            x = srcs[a][t0:t0 + rt, :]
            qkv16[0][a][t0:t0 + rt, :] = (x * scale if a == 0 else x).astype(BF16)
    for n in range(1, n_br):
        assert dils[n] % dils[n - 1] == 0
        step, l_prev, l_cur = dils[n] // dils[n - 1], seq // dils[n - 1], seq // dils[n]
        for a in range(3):
            src = srcs[a] if n == 1 else qkv32[n - 1][a]
            for bp in range(dils[n - 1]):
                for r2 in range(step):
                    for t0 in range(0, l_cur, rt):
                        rows = min(rt, l_cur - t0)
                        x = src[pl.ds(bp * l_prev + r2 + step * t0, rows, stride=step), :]
                        dst = (bp * step + r2) * l_cur + t0
                        if n < n_br - 1:
                            qkv32[n][a][dst:dst + rows, :] = x
                        qkv16[n][a][dst:dst + rows, :] = (x * scale if a == 0 else x).astype(BF16)

    def _residue(n, b):
        if n == 0:
            return 0
        step = dils[n] // dils[n - 1]
        return lax.rem(b, jnp.int32(step)) * dils[n - 1] + _residue(n - 1, lax.div(b, jnp.int32(step)))

    def blocks(jobs):
        place = []
        for n, idx in jobs:
            d, reach, length, qb, kb, n_qb, rel_starts = cfgs[n]
            if d == 1:
                blk, jb = 0, idx
            else:
                blk, jb = lax.div(idx, jnp.int32(n_qb)), lax.rem(idx, jnp.int32(n_qb))
            i0 = jb * qb
            ks = jnp.clip(i0 - reach, 0, length - kb)
            variant = sum(((ks - i0) <= rs).astype(jnp.int32) for rs in rel_starts[1:])
            base = blk * length
            qrows = pl.ds(pl.multiple_of(base + i0, qb), qb)
            krows = pl.ds(pl.multiple_of(base + ks, 16), kb)
            orows = qrows if d == 1 else pl.ds(_residue(n, blk) + d * i0, qb, stride=d)
            place.append((n, qrows, krows, orows, variant, qb, kb))
        s = []
        for n, qrows, krows, orows, variant, qb, kb in place:
            s.append(lax.dot_general(qkv16[n][0][qrows, :], qkv16[n][1][krows, :], _NT_DIMS,
                                     preferred_element_type=F32) + bias_refs[n][variant])
        m = [jnp.max(sj, axis=-1, keepdims=True) for sj in s]
        p = [jnp.exp(sj - mj).astype(BF16) for sj, mj in zip(s, m)]
        for (n, qrows, krows, orows, variant, qb, kb), mj, pj in zip(place, m, p):
            v_ones = jnp.concatenate([qkv16[n][2][krows, :], jnp.ones((kb, hd), BF16)], axis=1)
            acc_l = jnp.dot(pj, v_ones, preferred_element_type=F32)
            acc_refs[n][orows, :] = acc_l[:, :hd]
            l_refs[n][orows, :] = acc_l[:, hd:]
            m_refs[n][orows, :] = jnp.broadcast_to(mj, (qb, hd))

    counts = [seq // min(128, seq // d) for _, d in DIL_BRANCHES]
    group = 4
    if len(set(counts)) == 1 and counts[0] % group == 0:
        def body(i, carry):
            blocks([(n, group * i + u) for u in range(group) for n in range(n_br)])
            return carry
        lax.fori_loop(0, counts[0] // group, body, 0)
    else:
        for n in range(n_br):
            def body(idx, carry, n=n):
                blocks([(n, idx)])
                return carry
            lax.fori_loop(0, counts[n], body, 0)

    rt = min(256, seq)

    def finish(i, carry):
        rows = pl.ds(pl.multiple_of(i * rt, rt), rt)
        ms = [m_ref[rows, :] for m_ref in m_refs]
        m_all = functools.reduce(jnp.maximum, ms)
        num = jnp.zeros((rt, hd), F32)
        den = jnp.zeros((rt, hd), F32)
        for n in range(n_br):
            wgt = jnp.exp(ms[n] - m_all)
            num += wgt * acc_refs[n][rows, :]
            den += wgt * l_refs[n][rows, :]
        o_ref[rows, :] = (num / den).astype(o_ref.dtype)
        return carry

    lax.fori_loop(0, seq // rt, finish, 0)


def _dil(proj, slopes, *, batch, seq, offs, hd):
    cq, ck, cv = offs["dq"] // hd, offs["dk"] // hd, offs["dv"] // hd
    kern = functools.partial(_dil_kernel, seq=seq, scale=hd ** -0.5)
    return pl.pallas_call(
        kern,
        grid=(batch, DIL_HEADS),
        in_specs=[
            pl.BlockSpec(memory_space=pltpu.SMEM),
            pl.BlockSpec((seq, hd), lambda b, h: (b, cq + h)),
            pl.BlockSpec((seq, hd), lambda b, h: (b, ck + h)),
            pl.BlockSpec((seq, hd), lambda b, h: (b, cv + h)),
        ],
        out_specs=pl.BlockSpec((seq, hd), lambda b, h: (b, h)),
        out_shape=jax.ShapeDtypeStruct((batch * seq, DIL_HEADS * hd), BF16),
        scratch_shapes=([pltpu.VMEM((seq, hd), F32)] * (3 * len(DIL_BRANCHES))
                        + [pltpu.VMEM((len(c[6]), c[3], c[4]), F32) for c in _dil_cfgs(seq)]
                        + [pltpu.VMEM((seq, hd), BF16)] * (3 * len(DIL_BRANCHES))
                        + [pltpu.VMEM((seq, hd), F32)] * (3 * (len(DIL_BRANCHES) - 2))),
        compiler_params=_params(2),
        name="dil",
    )(slopes, proj, proj, proj)


def _conv_kernel(prev_ref, cur_ref, next_ref, w_ref, cb_ref, g_ref, b_ref, o_ref, u_ref, *, halo, taps):
    i = pl.program_id(1)
    c = o_ref.shape[-1]
    tq = o_ref.shape[0]

    def glu(ref):
        x = ref[...]
        return x[:, :c] * jax.nn.sigmoid(x[:, c:])

    ext = tq + 2 * halo
    u_ref[0, 0:halo, :] = jnp.where(i > 0, glu(prev_ref), 0.0)
    u_ref[0, halo:halo + tq, :] = glu(cur_ref)
    u_ref[0, halo + tq:, :] = jnp.where(i < pl.num_programs(1) - 1, glu(next_ref), 0.0)
    for s in range(1, SUBLANES):
        u_ref[s, 0:ext - SUBLANES, :] = u_ref[0, pl.ds(s, ext - SUBLANES), :]

    pad = taps // 2
    rc = min(32, tq)
    first = halo - pad
    span = (first + taps - 1) // SUBLANES * SUBLANES

    def chunk(j, carry):
        r0 = pl.multiple_of(j * rc, rc)
        acc = jnp.zeros((rc, c), F32) + cb_ref[...]
        for s in range(SUBLANES):
            win = u_ref[s, pl.ds(r0, rc + span), :]
            for t in range(taps):
                a, ts = divmod(first + t, SUBLANES)
                if ts == s:
                    acc = acc + win[a * SUBLANES:a * SUBLANES + rc] * w_ref[t:t + 1, :]
        y = _layer_norm(acc, g_ref[...], b_ref[...])
        o_ref[pl.ds(r0, rc), :] = (y * jax.nn.sigmoid(y)).astype(o_ref.dtype)
        return carry

    lax.fori_loop(0, tq // rc, chunk, 0, unroll=2)


def _conv(proj, conv_w, conv_b, ln_g, ln_b, layer, *, batch, seq, offs):
    taps, c = conv_w.shape[1:]
    halo = 16
    assert taps // 2 <= halo and halo % SUBLANES == 0
    tq = min(512, seq)
    nt = seq // tq
    hb = tq // halo
    ccol = offs["cv"] // (2 * c)
    kern = functools.partial(_conv_kernel, halo=halo, taps=taps)
    return pl.pallas_call(
        kern,
        grid=(batch, nt),
        in_specs=[
            pl.BlockSpec((halo, 2 * c), lambda b, i: (jnp.maximum((b * nt + i) * hb - 1, 0), ccol)),
            pl.BlockSpec((tq, 2 * c), lambda b, i: (b * nt + i, ccol)),
            pl.BlockSpec((halo, 2 * c),
                         lambda b, i: (jnp.minimum((b * nt + i + 1) * hb, batch * nt * hb - 1), ccol)),
            _layer_spec((taps, c), layer),
            _layer_spec((1, c), layer),
            _layer_spec((1, c), layer),
            _layer_spec((1, c), layer),
        ],
        out_specs=pl.BlockSpec((tq, c), lambda b, i: (b * nt + i, 0)),
        out_shape=jax.ShapeDtypeStruct((batch * seq, c), BF16),
        scratch_shapes=[pltpu.VMEM((SUBLANES, tq + 2 * halo, c), F32)],
        compiler_params=_params(2),
        name="conv",
    )(proj, proj, proj, conv_w, conv_b, ln_g, ln_b)


def _out_ln_kernel(x_ref, a_ref, d_ref, c_ref, w_ref, g_ref, b_ref, o_ref, *, alpha, heads, hdv):
    hv, nd = a_ref.shape[1] // heads, d_ref.shape[1]
    na = heads * hdv
    mix = jnp.dot(d_ref[...], w_ref[na:na + nd, :], preferred_element_type=F32)
    mix += jnp.dot(c_ref[...], w_ref[na + nd:, :], preferred_element_type=F32)
    for h in range(heads):
        mix += jnp.dot(a_ref[:, h * hv:(h + 1) * hv], w_ref[h * hdv:h * hdv + hv, :], preferred_element_type=F32)
    o_ref[...] = _layer_norm(alpha * x_ref[...] + mix, g_ref[...], b_ref[...])


def _out_ln(x, gla_o, dil_o, conv_o, w, g, b, layer, *, alpha, tm, heads, hdv):
    m, d = x.shape
    row = lambda i: (i, 0)
    assert (heads - 1) * hdv + gla_o.shape[1] // heads <= w.shape[1] and hdv % BF16_ROWS == 0
    return pl.pallas_call(
        functools.partial(_out_ln_kernel, alpha=alpha, heads=heads, hdv=hdv),
        grid=(m // tm,),
        in_specs=[
            pl.BlockSpec((tm, d), row),
            pl.BlockSpec((tm, gla_o.shape[1]), row),
            pl.BlockSpec((tm, dil_o.shape[1]), row),
            pl.BlockSpec((tm, conv_o.shape[1]), row),
            _layer_spec(w.shape[1:], layer),
            _layer_spec((1, d), layer),
            _layer_spec((1, d), layer),
        ],
        out_specs=pl.BlockSpec((tm, d), row),
        out_shape=jax.ShapeDtypeStruct((m, d), F32),
        compiler_params=_params(1),
        name="out_ln",
    )(x, gla_o, dil_o, conv_o, w, g, b)


def _pad_heads(w, heads, width):
    lead = w.shape[:-1]
    hd = w.shape[-1] // heads
    w = w.reshape(lead + (heads, hd))
    w = jnp.pad(w, [(0, 0)] * len(lead) + [(0, 0), (0, width - hd)])
    return w.reshape(lead + (heads * width,))


def kernel(x, ffn1_w_gate, ffn1_w_up, ffn1_w_down, ln1_g, ln1_b, w_in, gla_decay_w_fwd, gla_decay_b_fwd, gla_decay_w_bwd, gla_decay_b_bwd, gla_norm_g, conv_w, conv_b, conv_ln_g, conv_ln_b, w_out, ln2_g, ln2_b, ffn2_w_gate, ffn2_w_up, ffn2_w_down, ln3_g, ln3_b):
    batch, seq, d_model = x.shape
    depth = ffn1_w_gate.shape[0]
    rank, gla_dk = gla_decay_w_fwd.shape[1:]
    gla_dv = gla_norm_g.shape[1]
    conv_c = conv_w.shape[2]
    dil_dim = w_out.shape[1] - gla_dv - conv_c
    dil_hd = dil_dim // DIL_HEADS
    hdk, hdv = gla_dk // GLA_HEADS, gla_dv // GLA_HEADS
    hk, hv = _round_up(hdk, LANES), _round_up(hdv, LANES)
    assert dil_hd == LANES and 2 * rank <= LANES and seq % GLA_CHUNK == 0

    alpha = (2.0 * depth) ** 0.25
    m = batch * seq
    tm = min(512, m)

    def layout(widths):
        offs, total = {}, 0
        for name, wdt in widths:
            offs[name] = total
            total += wdt
        return offs

    offs_a = layout((("gq", GLA_HEADS * hk), ("gk", GLA_HEADS * hk), ("gv", GLA_HEADS * hv),
                     ("gg", GLA_HEADS * hv), ("r", LANES)))
    offs_b = layout((("cv", 2 * conv_c), ("dq", dil_dim), ("dk", dil_dim), ("dv", dil_dim)))

    src = layout((("gq", gla_dk), ("gk", gla_dk), ("gv", gla_dv), ("r", 2 * rank), ("gg", gla_dv),
                  ("dil", 3 * dil_dim), ("cv", 2 * conv_c)))
    runs = []
    for name, hd in (("gq", hdk), ("gk", hdk), ("gv", hdv), ("gg", hdv)):
        for h in range(GLA_HEADS):
            runs += [(src[name] + h * hd + c0, min(LANES, hd - c0)) for c0 in range(0, _round_up(hd, LANES), LANES)]
    runs.append((src["r"], 2 * rank))
    assert len(runs) * LANES == offs_a["r"] + LANES
    runs_b = [(src[name] + c0, LANES) for name, wdt in (("cv", 2 * conv_c), ("dil", 3 * dil_dim))
              for c0 in range(0, wdt, LANES)]
    w_in_a, w_in_b = _regroup(w_in, (runs, runs_b), rows=min(256, d_model))

    def head_mats(wdec, row0):
        w4 = _pad_heads(wdec, GLA_HEADS, hk).reshape(depth, rank, GLA_HEADS, hk).transpose(0, 2, 1, 3)
        return jnp.pad(w4, ((0, 0), (0, 0), (row0, LANES - rank - row0), (0, 0))).astype(BF16)

    wdec_f, wdec_b = head_mats(gla_decay_w_fwd, 0), head_mats(gla_decay_w_bwd, rank)
    bdec_f = _pad_heads(gla_decay_b_fwd, GLA_HEADS, hk).reshape(depth, GLA_HEADS, 1, hk)
    bdec_b = _pad_heads(gla_decay_b_bwd, GLA_HEADS, hk).reshape(depth, GLA_HEADS, 1, hk)
    norm_g = _pad_heads(gla_norm_g, GLA_HEADS, hv).reshape(depth, GLA_HEADS, 1, hv)

    w_out_b = w_out.astype(BF16)

    ffn_src = ((ffn1_w_gate, ffn1_w_up, ffn1_w_down), (ffn2_w_gate, ffn2_w_up, ffn2_w_down))
    ffn_w = tuple(w[0].astype(BF16) for w in ffn_src[0])
    vec = lambda p: p[:, None, :]
    slopes = jnp.asarray(2.0 ** (-8.0 * np.arange(1, DIL_HEADS + 1) / DIL_HEADS), F32)

    xf = x.reshape(m, d_model)
    for l in range(depth):
        xf, ffn_w = _ffn_ln(xf, *ffn_w, vec(ln1_g), vec(ln1_b), l, (ffn_src[1], l), alpha=alpha, tm=tm)
        proj_a = _in_proj(xf, (w_in_a,), l, BF16, tm=tm, name="in_proj_gla")
        proj_b = _in_proj(xf, (w_in_b,), l, F32, tm=tm, name="in_proj_mix")
        gla_o = _gla(proj_a, wdec_f, wdec_b, bdec_f, bdec_b, norm_g, l, batch=batch, seq=seq, offs=offs_a,
                     hk=hk, hv=hv, q_scale=hdk ** -0.5, dv_head=hdv)
        dil_o = _dil(proj_b, slopes, batch=batch, seq=seq, offs=offs_b, hd=dil_hd)
        conv_o = _conv(proj_b, conv_w, vec(conv_b), vec(conv_ln_g), vec(conv_ln_b), l, batch=batch, seq=seq,
                       offs=offs_b)
        xf = _out_ln(xf, gla_o, dil_o, conv_o, w_out_b, vec(ln2_g), vec(ln2_b), l, alpha=alpha, tm=tm,
                     heads=GLA_HEADS, hdv=hdv)
        nxt = (ffn_src[0], l + 1) if l + 1 < depth else None
        xf, ffn_w = _ffn_ln(xf, *ffn_w, vec(ln3_g), vec(ln3_b), l, nxt, alpha=alpha, tm=tm)
    return xf.reshape(batch, seq, d_model)
```

```python
import functools

import numpy as np
import jax
import jax.numpy as jnp
from jax import lax
from jax.experimental import pallas as pl
from jax.experimental.pallas import tpu as pltpu

GLA_HEADS = 4
GLA_TAU = 16.0
GLA_CHUNK = 64
DIL_HEADS = 6
DIL_BRANCHES = ((128, 1), (512, 4), (2048, 16))
LN_EPS = 1e-5
MASK_VALUE = -1e30

LANES = 128
SUBLANES = 8
BF16_ROWS = 16
MXU_WIDTH = 256
VMEM_LIMIT_BYTES = 56 * 1024 * 1024

F32 = jnp.float32
BF16 = jnp.bfloat16
_NT_DIMS = (((1,), (1,)), ((), ()))
_TN_DIMS = (((0,), (0,)), ((), ()))


def _round_up(n, m):
    return (n + m - 1) // m * m


def _layer_norm(y, g, b):
    mu = jnp.mean(y, axis=-1, keepdims=True)
    yc = y - mu
    var = jnp.mean(yc * yc, axis=-1, keepdims=True)
    return yc * lax.rsqrt(var + LN_EPS) * g + b


def _params(n_grid_axes):
    return pltpu.CompilerParams(dimension_semantics=("arbitrary",) * n_grid_axes,
                                vmem_limit_bytes=VMEM_LIMIT_BYTES)


def _layer_spec(shape, layer):
    zeros = (0,) * len(shape)
    return pl.BlockSpec((None,) + tuple(shape), lambda *_: (layer,) + zeros)


def _swiglu_part(xb, wg, wu, wd):
    gate = jnp.dot(xb, wg, preferred_element_type=F32)
    up = jnp.dot(xb, wu, preferred_element_type=F32)
    h = (gate * jax.nn.sigmoid(gate) * up).astype(BF16)
    return jnp.dot(h, wd, preferred_element_type=F32)


def _ffn_ln_kernel(x_ref, wg_ref, wu_ref, wd_ref, wgt_ref, wut_ref, wdt_ref, g_ref, b_ref, *rest, alpha, n_cast):
    src_refs, o_ref, dst_refs, xb_ref = rest[:n_cast], rest[n_cast], rest[n_cast + 1:-1], rest[-1]
    f = pl.program_id(1)

    @pl.when(f == 0)
    def _():
        xb = x_ref[...].astype(BF16)
        xb_ref[...] = xb
        o_ref[...] = _swiglu_part(xb, wgt_ref[...], wut_ref[...], wdt_ref[...])

    o_ref[...] += _swiglu_part(xb_ref[...], wg_ref[...], wu_ref[...], wd_ref[...])
    for src, dst in zip(src_refs, dst_refs):
        dst[...] = src[...].astype(BF16)

    @pl.when(f == pl.num_programs(1) - 1)
    def _():
        y = alpha * x_ref[...] + 0.5 * o_ref[...]
        o_ref[...] = _layer_norm(y, g_ref[...], b_ref[...])


def _ffn_tiles(d_ff):
    for tf in (3 * MXU_WIDTH, 2 * MXU_WIDTH, MXU_WIDTH):
        tail = d_ff % tf
        if tail and tail % LANES == 0 and (d_ff - tail) % tail == 0:
            return tf, tail
    raise NotImplementedError(f"no FFN tiling for d_ff={d_ff}")


def _slab_rows(rows, n_steps):
    for r in range(BF16_ROWS, rows + 1, BF16_ROWS):
        if rows % r == 0 and rows // r <= n_steps:
            return r
    raise NotImplementedError(f"cannot spread {rows} rows over {n_steps} steps")


def _ffn_ln(x, wg, wu, wd, g, b, layer, cast_next=None, *, alpha, tm):
    m, d = x.shape
    d_ff = wg.shape[1]
    tf, tail = _ffn_tiles(d_ff)
    n_f = d_ff // tf
    tail_blk = (d_ff - tail) // tail
    in_specs = [
        pl.BlockSpec((tm, d), lambda i, f: (i, 0)),
        pl.BlockSpec((d, tf), lambda i, f: (0, f)),
        pl.BlockSpec((d, tf), lambda i, f: (0, f)),
        pl.BlockSpec((tf, d), lambda i, f: (f, 0)),
        pl.BlockSpec((d, tail), lambda i, f: (0, tail_blk)),
        pl.BlockSpec((d, tail), lambda i, f: (0, tail_blk)),
        pl.BlockSpec((tail, d), lambda i, f: (tail_blk, 0)),
        _layer_spec((1, d), layer),
        _layer_spec((1, d), layer),
    ]
    out_specs = [pl.BlockSpec((tm, d), lambda i, f: (i, 0))]
    out_shape = [jax.ShapeDtypeStruct((m, d), F32)]
    args = [x, wg, wu, wd, wg, wu, wd, g, b]
    if cast_next is not None:
        srcs, nxt = cast_next
        n_steps = (m // tm) * n_f
        for w in srcs:
            rows, cols = w.shape[1:]
            r = _slab_rows(rows, n_steps)
            slab = lambda i, f, last=rows // r - 1: jnp.minimum(i * n_f + f, last)
            in_specs.append(pl.BlockSpec((None, r, cols), lambda i, f, slab=slab: (nxt, slab(i, f), 0)))
            out_specs.append(pl.BlockSpec((r, cols), lambda i, f, slab=slab: (slab(i, f), 0)))
            out_shape.append(jax.ShapeDtypeStruct((rows, cols), BF16))
            args.append(w)
    outs = pl.pallas_call(
        functools.partial(_ffn_ln_kernel, alpha=alpha, n_cast=len(args) - 9),
        grid=(m // tm, n_f),
        in_specs=in_specs,
        out_specs=out_specs,
        out_shape=out_shape,
        scratch_shapes=[pltpu.VMEM((tm, d), BF16)],
        compiler_params=_params(2),
        name="ffn_ln",
    )(*args)
    return outs[0], tuple(outs[1:])


def _in_proj_kernel(x_ref, *refs):
    w_refs, o_ref = refs[:-1], refs[-1]
    xb = x_ref[...].astype(BF16)
    col = 0
    for w_ref in w_refs:
        n = w_ref.shape[1]
        o_ref[:, col:col + n] = jnp.dot(xb, w_ref[...], preferred_element_type=F32).astype(o_ref.dtype)
        col += n


def _in_proj(x, ws, layer, out_dtype, *, tm, name):
    m, d = x.shape
    n = sum(w.shape[2] for w in ws)
    return pl.pallas_call(
        _in_proj_kernel,
        grid=(m // tm,),
        in_specs=[pl.BlockSpec((tm, d), lambda i: (i, 0))] + [
            pl.BlockSpec((None, d, w.shape[2]), lambda i: (layer, 0, 0), pipeline_mode=pl.Buffered(1)) for w in ws],
        out_specs=pl.BlockSpec((tm, n), lambda i: (i, 0)),
        out_shape=jax.ShapeDtypeStruct((m, n), out_dtype),
        compiler_params=_params(1),
        name=name,
    )(x, *ws)


def _regroup_kernel(wt_ref, *o_refs, runs):
    cb = wt_ref.shape[1]
    eye = (lax.broadcasted_iota(jnp.int32, (cb, cb), 0) == lax.broadcasted_iota(jnp.int32, (cb, cb), 1)).astype(BF16)
    row = lax.broadcasted_iota(jnp.int32, (LANES, cb), 0)
    for o_ref, o_runs in zip(o_refs, runs):
        for blk, (start, width) in enumerate(o_runs):
            rows = wt_ref[start:start + LANES, :]
            if width < LANES:
                rows = jnp.where(row < width, rows, 0.0)
            o_ref[:, blk * LANES:(blk + 1) * LANES] = lax.dot_general(
                eye, rows.astype(BF16), _NT_DIMS, preferred_element_type=F32).astype(o_ref.dtype)


def _regroup(w, runs, *, cb):
    depth, d, n = w.shape
    assert all(s % SUBLANES == 0 and s + LANES <= n for o_runs in runs for s, _ in o_runs)
    return pl.pallas_call(
        functools.partial(_regroup_kernel, runs=tuple(tuple(r) for r in runs)),
        grid=(depth, d // cb),
        in_specs=[pl.BlockSpec((None, n, cb), lambda l, i: (l, 0, i))],
        out_specs=[pl.BlockSpec((None, cb, LANES * len(r)), lambda l, i: (l, i, 0)) for r in runs],
        out_shape=[jax.ShapeDtypeStruct((depth, d, LANES * len(r)), BF16) for r in runs],
        compiler_params=_params(2),
        name="regroup_w_in",
    )(jnp.swapaxes(w, 1, 2))


def _log_sigmoid(z):
    return jnp.minimum(z, 0.0) - jnp.log(1.0 + jnp.exp(-jnp.abs(z)))


def _gla_kernel(q_ref, k_ref, v_ref, gg_ref, r_ref, wf_ref, wb_ref, bf_ref, bb_ref, ng_ref,
                o_ref, of_ref, ob_ref, st_ref, *, seq, tile, chunk, hk, hv, q_scale, inv_dv):
    n_tiles = seq // tile
    n_chunks = tile // chunk
    n_heads = q_ref.shape[-1] // hk
    shift = chunk.bit_length() - 1
    row = lax.broadcasted_iota(jnp.int32, (tile, tile), 0)
    col = lax.broadcasted_iota(jnp.int32, (tile, tile), 1)
    same_chunk = lax.shift_right_logical(row, shift) == lax.shift_right_logical(col, shift)
    tris = (same_chunk & (col <= row), same_chunk & (col >= row))
    oacc_refs = (of_ref, ob_ref)
    w_refs, b_refs = (wf_ref, wb_ref), (bf_ref, bb_ref)
    chains = [(h, dn) for h in range(n_heads) for dn in (0, 1)]
    every = range(len(chains))
    kcols = [slice(h * hk, (h + 1) * hk) for h, _ in chains]
    vcols = [slice(h * hv, (h + 1) * hv) for h, _ in chains]

    def step(i):
        rows = [pl.ds(pl.multiple_of((i if dn == 0 else n_tiles - 1 - i) * tile, tile), tile) for _, dn in chains]
        z = [jnp.dot(r_ref[rows[c], :], w_refs[dn][h], preferred_element_type=F32) + b_refs[dn][h]
             for c, (h, dn) in enumerate(chains)]
        la = [_log_sigmoid(zc) * (1.0 / GLA_TAU) for zc in z]
        la_hi = [x.astype(BF16) for x in la]
        la_lo = [(x - hi.astype(F32)).astype(BF16) for x, hi in zip(la, la_hi)]
        cum = [jnp.dot(tris[dn].astype(BF16), jnp.concatenate([la_hi[c], la_lo[c]], axis=1),
                       preferred_element_type=F32) for c, (_, dn) in enumerate(chains)]
        bcum = [x[:, :hk] + x[:, hk:] for x in cum]

        vb = [v_ref[rows[c], vcols[c]] for c in every]
        q_dec = [(q_ref[rows[c], kcols[c]].astype(F32) * q_scale * jnp.exp(bcum[c])).astype(BF16) for c in every]
        k_inv = [k_ref[rows[c], kcols[c]].astype(F32) * jnp.exp(-bcum[c]) for c in every]
        att = [lax.dot_general(q_dec[c], k_inv[c].astype(BF16), _NT_DIMS, preferred_element_type=F32) for c in every]
        att = [jnp.where(tris[dn], att[c], 0.0).astype(BF16) for c, (_, dn) in enumerate(chains)]
        o_intra = [jnp.dot(att[c], vb[c], preferred_element_type=F32) for c in every]

        o_inter = [[None] * n_chunks for _ in every]
        for s in range(n_chunks):
            for c, (_, dn) in enumerate(chains):
                ck = s if dn == 0 else n_chunks - 1 - s
                lo = ck * chunk
                end_row = lo + chunk - 1 if dn == 0 else lo
                e_end = jnp.exp(bcum[c][end_row:end_row + 1, :])
                st = st_ref[c]
                o_inter[c][ck] = lax.dot_general(q_dec[c][lo:lo + chunk], st.astype(BF16), _NT_DIMS,
                                                 preferred_element_type=F32)
                k_end = (k_inv[c][lo:lo + chunk] * e_end).astype(BF16)
                upd = lax.dot_general(vb[c][lo:lo + chunk], k_end, _TN_DIMS, preferred_element_type=F32)
                st_ref[c] = st * e_end + upd
        for c, (_, dn) in enumerate(chains):
            oacc_refs[dn][rows[c], vcols[c]] = o_intra[c] + jnp.concatenate(o_inter[c], axis=0)

    st_ref[...] = jnp.zeros_like(st_ref)

    def body(i, carry):
        step(i)
        return carry

    lax.fori_loop(0, n_tiles, body, 0)

    def finish(i, carry):
        rows = pl.ds(pl.multiple_of(i * tile, tile), tile)
        for h in range(n_heads):
            cols = slice(h * hv, (h + 1) * hv)
            o = of_ref[rows, cols] + ob_ref[rows, cols]
            ms = jnp.sum(o * o, axis=-1, keepdims=True) * inv_dv
            gate = gg_ref[rows, cols].astype(F32)
            o = o * lax.rsqrt(ms + LN_EPS) * ng_ref[h] * (gate * jax.nn.sigmoid(gate))
            o_ref[rows, cols] = o.astype(o_ref.dtype)
        return carry

    lax.fori_loop(0, n_tiles, finish, 0)


def _gla(proj, wdec_f, wdec_b, bdec_f, bdec_b, norm_g, layer, *, batch, seq, offs, hk, hv, q_scale, dv_head):
    heads = wdec_f.shape[1]
    hps = 2 if heads % 2 == 0 else 1
    tile = min(256, seq)
    kern = functools.partial(_gla_kernel, seq=seq, tile=tile, chunk=GLA_CHUNK, hk=hk, hv=hv, q_scale=q_scale,
                             inv_dv=1.0 / dv_head)
    wk, wv = hps * hk, hps * hv
    assert all(offs[n] % w == 0 for n, w in (("gq", wk), ("gk", wk), ("gv", wv), ("gg", wv)))
    cq, ck, cv, cg, cr = offs["gq"] // wk, offs["gk"] // wk, offs["gv"] // wv, offs["gg"] // wv, offs["r"] // LANES
    per_head = lambda rows, cols: pl.BlockSpec((None, hps, rows, cols), lambda b, g: (layer, g, 0, 0))
    return pl.pallas_call(
        kern,
        grid=(batch, heads // hps),
        in_specs=[
            pl.BlockSpec((seq, wk), lambda b, g: (b, cq + g)),
            pl.BlockSpec((seq, wk), lambda b, g: (b, ck + g)),
            pl.BlockSpec((seq, wv), lambda b, g: (b, cv + g)),
            pl.BlockSpec((seq, wv), lambda b, g: (b, cg + g)),
            pl.BlockSpec((seq, LANES), lambda b, g: (b, cr)),
            per_head(LANES, hk), per_head(LANES, hk),
            per_head(1, hk), per_head(1, hk), per_head(1, hv),
        ],
        out_specs=pl.BlockSpec((seq, wv), lambda b, g: (b, g)),
        out_shape=jax.ShapeDtypeStruct((batch * seq, heads * hv), BF16),
        scratch_shapes=[pltpu.VMEM((seq, wv), F32), pltpu.VMEM((seq, wv), F32),
                        pltpu.VMEM((2 * hps, hv, hk), F32)],
        compiler_params=_params(2),
        name="gla",
    )(proj, proj, proj, proj, proj, wdec_f, wdec_b, bdec_f, bdec_b, norm_g)


def _dil_cfgs(seq):
    cfgs = []
    for w, d in DIL_BRANCHES:
        reach = (w // 2) // d
        length = seq // d
        qb = min(128, length)
        kb = min(length, qb + 2 * reach)
        n_qb = length // qb
        rel_starts = sorted({min(max(jb * qb - reach, 0), length - kb) - jb * qb for jb in range(n_qb)},
                            reverse=True)
        cfgs.append((d, reach, length, qb, kb, n_qb, rel_starts))
    return cfgs


def _dil_kernel(slopes_ref, q_ref, k_ref, v_ref, o_ref, *stats, seq, scale):
    n_br = len(DIL_BRANCHES)
    m_refs, l_refs, acc_refs, bias_refs = (stats[i * n_br:(i + 1) * n_br] for i in range(4))
    qkv16 = [stats[4 * n_br + 3 * n:4 * n_br + 3 * n + 3] for n in range(n_br)]
    qkv32 = [None, *[stats[7 * n_br + 3 * (n - 1):7 * n_br + 3 * n] for n in range(1, n_br - 1)]]
    slope = slopes_ref[pl.program_id(1)]
    hd = q_ref.shape[-1]
    dils = [d for _, d in DIL_BRANCHES]

    cfgs = _dil_cfgs(seq)
    for n, (d, reach, length, qb, kb, n_qb, rel_starts) in enumerate(cfgs):
        rel = lax.broadcasted_iota(jnp.int32, (qb, kb), 1) - lax.broadcasted_iota(jnp.int32, (qb, kb), 0)
        for vi, rs in enumerate(rel_starts):
            dist = jnp.abs(rel + rs)
            bias_refs[n][vi] = jnp.where(dist <= reach, -(slope * float(d)) * dist.astype(F32), MASK_VALUE)

    rt = min(256, seq)
    srcs = (q_ref, k_ref, v_ref)
    for a in range(3):
        for t0 in range(0, seq, rt):
            x = srcs[a][t0:t0 + rt, :]
            qkv16[0][a][t0:t0 + rt, :] = (x * scale if a == 0 else x).astype(BF16)
    for n in range(1, n_br):
        assert dils[n] % dils[n - 1] == 0
        step, l_prev, l_cur = dils[n] // dils[n - 1], seq // dils[n - 1], seq // dils[n]
        for a in range(3):
            src = srcs[a] if n == 1 else qkv32[n - 1][a]
            for bp in range(dils[n - 1]):
                for r2 in range(step):
                    for t0 in range(0, l_cur, rt):
                        rows = min(rt, l_cur - t0)
                        x = src[pl.ds(bp * l_prev + r2 + step * t0, rows, stride=step), :]
                        dst = (bp * step + r2) * l_cur + t0
                        if n < n_br - 1:
                            qkv32[n][a][dst:dst + rows, :] = x
                        qkv16[n][a][dst:dst + rows, :] = (x * scale if a == 0 else x).astype(BF16)

    def _residue(n, b):
        if n == 0:
            return 0
        step = dils[n] // dils[n - 1]
        return lax.rem(b, jnp.int32(step)) * dils[n - 1] + _residue(n - 1, lax.div(b, jnp.int32(step)))

    def blocks(jobs):
        place = []
        for n, idx in jobs:
            d, reach, length, qb, kb, n_qb, rel_starts = cfgs[n]
            if d == 1:
                blk, jb = 0, idx
            else:
                blk, jb = lax.div(idx, jnp.int32(n_qb)), lax.rem(idx, jnp.int32(n_qb))
            i0 = jb * qb
            ks = jnp.clip(i0 - reach, 0, length - kb)
            variant = sum(((ks - i0) <= rs).astype(jnp.int32) for rs in rel_starts[1:])
            base = blk * length
            qrows = pl.ds(pl.multiple_of(base + i0, qb), qb)
            krows = pl.ds(pl.multiple_of(base + ks, 16), kb)
            orows = qrows if d == 1 else pl.ds(_residue(n, blk) + d * i0, qb, stride=d)
            place.append((n, qrows, krows, orows, variant, qb, kb))
        s = []
        for n, qrows, krows, orows, variant, qb, kb in place:
            s.append(lax.dot_general(qkv16[n][0][qrows, :], qkv16[n][1][krows, :], _NT_DIMS,
                                     preferred_element_type=F32) + bias_refs[n][variant])
        m = [jnp.max(sj, axis=-1, keepdims=True) for sj in s]
        p = [jnp.exp(sj - mj).astype(BF16) for sj, mj in zip(s, m)]
        for (n, qrows, krows, orows, variant, qb, kb), mj, pj in zip(place, m, p):
            v_ones = jnp.concatenate([qkv16[n][2][krows, :], jnp.ones((kb, hd), BF16)], axis=1)
            acc_l = jnp.dot(pj, v_ones, preferred_element_type=F32)
            acc_refs[n][orows, :] = acc_l[:, :hd]
            l_refs[n][orows, :] = acc_l[:, hd:]
            m_refs[n][orows, :] = jnp.broadcast_to(mj, (qb, hd))

    counts = [seq // min(128, seq // d) for _, d in DIL_BRANCHES]
    group = 4
    if len(set(counts)) == 1 and counts[0] % group == 0:
        def body(i, carry):
            blocks([(n, group * i + u) for u in range(group) for n in range(n_br)])
            return carry
        lax.fori_loop(0, counts[0] // group, body, 0)
    else:
        for n in range(n_br):
            def body(idx, carry, n=n):
                blocks([(n, idx)])
                return carry
            lax.fori_loop(0, counts[n], body, 0)

    rt = min(256, seq)

    def finish(i, carry):
        rows = pl.ds(pl.multiple_of(i * rt, rt), rt)
        ms = [m_ref[rows, :] for m_ref in m_refs]
        m_all = functools.reduce(jnp.maximum, ms)
        num = jnp.zeros((rt, hd), F32)
        den = jnp.zeros((rt, hd), F32)
        for n in range(n_br):
            wgt = jnp.exp(ms[n] - m_all)
            num += wgt * acc_refs[n][rows, :]
            den += wgt * l_refs[n][rows, :]
        o_ref[rows, :] = (num / den).astype(o_ref.dtype)
        return carry

    lax.fori_loop(0, seq // rt, finish, 0)


def _dil(proj, slopes, *, batch, seq, offs, hd):
    cq, ck, cv = offs["dq"] // hd, offs["dk"] // hd, offs["dv"] // hd
    kern = functools.partial(_dil_kernel, seq=seq, scale=hd ** -0.5)
    return pl.pallas_call(
        kern,
        grid=(batch, DIL_HEADS),
        in_specs=[
            pl.BlockSpec(memory_space=pltpu.SMEM),
            pl.BlockSpec((seq, hd), lambda b, h: (b, cq + h)),
            pl.BlockSpec((seq, hd), lambda b, h: (b, ck + h)),
            pl.BlockSpec((seq, hd), lambda b, h: (b, cv + h)),
        ],
        out_specs=pl.BlockSpec((seq, hd), lambda b, h: (b, h)),
        out_shape=jax.ShapeDtypeStruct((batch * seq, DIL_HEADS * hd), BF16),
        scratch_shapes=([pltpu.VMEM((seq, hd), F32)] * (3 * len(DIL_BRANCHES))
                        + [pltpu.VMEM((len(c[6]), c[3], c[4]), F32) for c in _dil_cfgs(seq)]
                        + [pltpu.VMEM((seq, hd), BF16)] * (3 * len(DIL_BRANCHES))
                        + [pltpu.VMEM((seq, hd), F32)] * (3 * (len(DIL_BRANCHES) - 2))),
        compiler_params=_params(2),
        name="dil",
    )(slopes, proj, proj, proj)


def _conv_kernel(prev_ref, cur_ref, next_ref, w_ref, cb_ref, g_ref, b_ref, o_ref, u_ref, *, halo, taps):
    i = pl.program_id(1)
    c = o_ref.shape[-1]
    tq = o_ref.shape[0]

    def glu(ref):
        x = ref[...]
        return x[:, :c] * jax.nn.sigmoid(x[:, c:])

    ext = tq + 2 * halo
    u_ref[0, 0:halo, :] = jnp.where(i > 0, glu(prev_ref), 0.0)
    u_ref[0, halo:halo + tq, :] = glu(cur_ref)
    u_ref[0, halo + tq:, :] = jnp.where(i < pl.num_programs(1) - 1, glu(next_ref), 0.0)
    for s in range(1, SUBLANES):
        u_ref[s, 0:ext - SUBLANES, :] = u_ref[0, pl.ds(s, ext - SUBLANES), :]

    pad = taps // 2
    rc = min(32, tq)
    first = halo - pad
    span = (first + taps - 1) // SUBLANES * SUBLANES

    def chunk(j, carry):
        r0 = pl.multiple_of(j * rc, rc)
        acc = jnp.zeros((rc, c), F32) + cb_ref[...]
        for s in range(SUBLANES):
            win = u_ref[s, pl.ds(r0, rc + span), :]
            for t in range(taps):
                a, ts = divmod(first + t, SUBLANES)
                if ts == s:
                    acc = acc + win[a * SUBLANES:a * SUBLANES + rc] * w_ref[t:t + 1, :]
        y = _layer_norm(acc, g_ref[...], b_ref[...])
        o_ref[pl.ds(r0, rc), :] = (y * jax.nn.sigmoid(y)).astype(o_ref.dtype)
        return carry

    lax.fori_loop(0, tq // rc, chunk, 0, unroll=2)


def _conv(proj, conv_w, conv_b, ln_g, ln_b, layer, *, batch, seq, offs):
    taps, c = conv_w.shape[1:]
    halo = 16
    assert taps // 2 <= halo and halo % SUBLANES == 0
    tq = min(512, seq)
    nt = seq // tq
    hb = tq // halo
    ccol = offs["cv"] // (2 * c)
    kern = functools.partial(_conv_kernel, halo=halo, taps=taps)
    return pl.pallas_call(
        kern,
        grid=(batch, nt),
        in_specs=[
            pl.BlockSpec((halo, 2 * c), lambda b, i: (jnp.maximum((b * nt + i) * hb - 1, 0), ccol)),
            pl.BlockSpec((tq, 2 * c), lambda b, i: (b * nt + i, ccol)),
            pl.BlockSpec((halo, 2 * c),
                         lambda b, i: (jnp.minimum((b * nt + i + 1) * hb, batch * nt * hb - 1), ccol)),
            _layer_spec((taps, c), layer),
            _layer_spec((1, c), layer),
            _layer_spec((1, c), layer),
            _layer_spec((1, c), layer),
        ],
        out_specs=pl.BlockSpec((tq, c), lambda b, i: (b * nt + i, 0)),
        out_shape=jax.ShapeDtypeStruct((batch * seq, c), BF16),
        scratch_shapes=[pltpu.VMEM((SUBLANES, tq + 2 * halo, c), F32)],
        compiler_params=_params(2),
        name="conv",
    )(proj, proj, proj, conv_w, conv_b, ln_g, ln_b)


def _out_ln_kernel(x_ref, a_ref, d_ref, c_ref, w_ref, g_ref, b_ref, o_ref, *, alpha, heads, hdv):
    hv, nd = a_ref.shape[1] // heads, d_ref.shape[1]
    na = heads * hdv
    mix = jnp.dot(d_ref[...], w_ref[na:na + nd, :], preferred_element_type=F32)
    mix += jnp.dot(c_ref[...], w_ref[na + nd:, :], preferred_element_type=F32)
    for h in range(heads):
        mix += jnp.dot(a_ref[:, h * hv:(h + 1) * hv], w_ref[h * hdv:h * hdv + hv, :], preferred_element_type=F32)
    o_ref[...] = _layer_norm(alpha * x_ref[...] + mix, g_ref[...], b_ref[...])


def _out_ln(x, gla_o, dil_o, conv_o, w, g, b, layer, *, alpha, tm, heads, hdv):
    m, d = x.shape
    row = lambda i: (i, 0)
    assert (heads - 1) * hdv + gla_o.shape[1] // heads <= w.shape[1] and hdv % BF16_ROWS == 0
    return pl.pallas_call(
        functools.partial(_out_ln_kernel, alpha=alpha, heads=heads, hdv=hdv),
        grid=(m // tm,),
        in_specs=[
            pl.BlockSpec((tm, d), row),
            pl.BlockSpec((tm, gla_o.shape[1]), row),
            pl.BlockSpec((tm, dil_o.shape[1]), row),
            pl.BlockSpec((tm, conv_o.shape[1]), row),
            _layer_spec(w.shape[1:], layer),
            _layer_spec((1, d), layer),
            _layer_spec((1, d), layer),
        ],
        out_specs=pl.BlockSpec((tm, d), row),
        out_shape=jax.ShapeDtypeStruct((m, d), F32),
        compiler_params=_params(1),
        name="out_ln",
    )(x, gla_o, dil_o, conv_o, w, g, b)


def _pad_heads(w, heads, width):
    lead = w.shape[:-1]
    hd = w.shape[-1] // heads
    w = w.reshape(lead + (heads, hd))
    w = jnp.pad(w, [(0, 0)] * len(lead) + [(0, 0), (0, width - hd)])
    return w.reshape(lead + (heads * width,))


def kernel(x, ffn1_w_gate, ffn1_w_up, ffn1_w_down, ln1_g, ln1_b, w_in, gla_decay_w_fwd, gla_decay_b_fwd, gla_decay_w_bwd, gla_decay_b_bwd, gla_norm_g, conv_w, conv_b, conv_ln_g, conv_ln_b, w_out, ln2_g, ln2_b, ffn2_w_gate, ffn2_w_up, ffn2_w_down, ln3_g, ln3_b):
    batch, seq, d_model = x.shape
    depth = ffn1_w_gate.shape[0]
    rank, gla_dk = gla_decay_w_fwd.shape[1:]
    gla_dv = gla_norm_g.shape[1]
    conv_c = conv_w.shape[2]
    dil_dim = w_out.shape[1] - gla_dv - conv_c
    dil_hd = dil_dim // DIL_HEADS
    hdk, hdv = gla_dk // GLA_HEADS, gla_dv // GLA_HEADS
    hk, hv = _round_up(hdk, LANES), _round_up(hdv, LANES)
    assert dil_hd == LANES and 2 * rank <= LANES and seq % GLA_CHUNK == 0

    alpha = (2.0 * depth) ** 0.25
    m = batch * seq
    tm = min(512, m)

    def layout(widths):
        offs, total = {}, 0
        for name, wdt in widths:
            offs[name] = total
            total += wdt
        return offs

    offs_a = layout((("gq", GLA_HEADS * hk), ("gk", GLA_HEADS * hk), ("gv", GLA_HEADS * hv),
                     ("gg", GLA_HEADS * hv), ("r", LANES)))
    offs_b = layout((("cv", 2 * conv_c), ("dq", dil_dim), ("dk", dil_dim), ("dv", dil_dim)))

    src = layout((("gq", gla_dk), ("gk", gla_dk), ("gv", gla_dv), ("r", 2 * rank), ("gg", gla_dv),
                  ("dil", 3 * dil_dim), ("cv", 2 * conv_c)))
    runs = []
    for name, hd in (("gq", hdk), ("gk", hdk), ("gv", hdv), ("gg", hdv)):
        for h in range(GLA_HEADS):
            runs += [(src[name] + h * hd + c0, min(LANES, hd - c0)) for c0 in range(0, _round_up(hd, LANES), LANES)]
    runs.append((src["r"], 2 * rank))
    assert len(runs) * LANES == offs_a["r"] + LANES
    runs_b = [(src[name] + c0, LANES) for name, wdt in (("cv", 2 * conv_c), ("dil", 3 * dil_dim))
              for c0 in range(0, wdt, LANES)]
    w_in_a, w_in_b = _regroup(w_in, (runs, runs_b), cb=min(MXU_WIDTH, d_model))

    def head_mats(wdec, row0):
        w4 = _pad_heads(wdec, GLA_HEADS, hk).reshape(depth, rank, GLA_HEADS, hk).transpose(0, 2, 1, 3)
        return jnp.pad(w4, ((0, 0), (0, 0), (row0, LANES - rank - row0), (0, 0))).astype(BF16)

    wdec_f, wdec_b = head_mats(gla_decay_w_fwd, 0), head_mats(gla_decay_w_bwd, rank)
    bdec_f = _pad_heads(gla_decay_b_fwd, GLA_HEADS, hk).reshape(depth, GLA_HEADS, 1, hk)
    bdec_b = _pad_heads(gla_decay_b_bwd, GLA_HEADS, hk).reshape(depth, GLA_HEADS, 1, hk)
    norm_g = _pad_heads(gla_norm_g, GLA_HEADS, hv).reshape(depth, GLA_HEADS, 1, hv)

    w_out_b = w_out.astype(BF16)

    ffn_src = ((ffn1_w_gate, ffn1_w_up, ffn1_w_down), (ffn2_w_gate, ffn2_w_up, ffn2_w_down))
    ffn_w = tuple(w[0].astype(BF16) for w in ffn_src[0])
    vec = lambda p: p[:, None, :]
    slopes = jnp.asarray(2.0 ** (-8.0 * np.arange(1, DIL_HEADS + 1) / DIL_HEADS), F32)

    xf = x.reshape(m, d_model)
    for l in range(depth):
        xf, ffn_w = _ffn_ln(xf, *ffn_w, vec(ln1_g), vec(ln1_b), l, (ffn_src[1], l), alpha=alpha, tm=tm)
        proj_a = _in_proj(xf, (w_in_a,), l, BF16, tm=tm, name="in_proj_gla")
        proj_b = _in_proj(xf, (w_in_b,), l, F32, tm=tm, name="in_proj_mix")
        gla_o = _gla(proj_a, wdec_f, wdec_b, bdec_f, bdec_b, norm_g, l, batch=batch, seq=seq, offs=offs_a,
                     hk=hk, hv=hv, q_scale=hdk ** -0.5, dv_head=hdv)
        dil_o = _dil(proj_b, slopes, batch=batch, seq=seq, offs=offs_b, hd=dil_hd)
        conv_o = _conv(proj_b, conv_w, vec(conv_b), vec(conv_ln_g), vec(conv_ln_b), l, batch=batch, seq=seq,
                       offs=offs_b)
        xf = _out_ln(xf, gla_o, dil_o, conv_o, w_out_b, vec(ln2_g), vec(ln2_b), l, alpha=alpha, tm=tm,
                     heads=GLA_HEADS, hdv=hdv)
        nxt = (ffn_src[0], l + 1) if l + 1 < depth else None
        xf, ffn_w = _ffn_ln(xf, *ffn_w, vec(ln3_g), vec(ln3_b), l, nxt, alpha=alpha, tm=tm)
    return xf.reshape(batch, seq, d_model)
```

```python
import functools

import numpy as np
import jax
import jax.numpy as jnp
from jax import lax
from jax.experimental import pallas as pl
from jax.experimental.pallas import tpu as pltpu

GLA_HEADS = 4
GLA_TAU = 16.0
GLA_CHUNK = 64
DIL_HEADS = 6
DIL_BRANCHES = ((128, 1), (512, 4), (2048, 16))
LN_EPS = 1e-5
MASK_VALUE = -1e30

LANES = 128
SUBLANES = 8
BF16_ROWS = 16
MXU_WIDTH = 256
VMEM_LIMIT_BYTES = 56 * 1024 * 1024

F32 = jnp.float32
BF16 = jnp.bfloat16
_NT_DIMS = (((1,), (1,)), ((), ()))
_TN_DIMS = (((0,), (0,)), ((), ()))


def _round_up(n, m):
    return (n + m - 1) // m * m


def _layer_norm(y, g, b):
    mu = jnp.mean(y, axis=-1, keepdims=True)
    yc = y - mu
    var = jnp.mean(yc * yc, axis=-1, keepdims=True)
    return yc * lax.rsqrt(var + LN_EPS) * g + b


def _params(n_grid_axes):
    return pltpu.CompilerParams(dimension_semantics=("arbitrary",) * n_grid_axes,
                                vmem_limit_bytes=VMEM_LIMIT_BYTES)


def _layer_spec(shape, layer):
    zeros = (0,) * len(shape)
    return pl.BlockSpec((None,) + tuple(shape), lambda *_: (layer,) + zeros)


def _swiglu_part(xb, wg, wu, wd):
    gate = jnp.dot(xb, wg, preferred_element_type=F32)
    up = jnp.dot(xb, wu, preferred_element_type=F32)
    h = (gate * jax.nn.sigmoid(gate) * up).astype(BF16)
    return jnp.dot(h, wd, preferred_element_type=F32)


def _ffn_ln_kernel(x_ref, wg_ref, wu_ref, wd_ref, wgt_ref, wut_ref, wdt_ref, g_ref, b_ref, *rest, alpha, n_cast):
    src_refs, o_ref, dst_refs, xb_ref = rest[:n_cast], rest[n_cast], rest[n_cast + 1:-1], rest[-1]
    f = pl.program_id(1)

    @pl.when(f == 0)
    def _():
        xb = x_ref[...].astype(BF16)
        xb_ref[...] = xb
        o_ref[...] = _swiglu_part(xb, wgt_ref[...], wut_ref[...], wdt_ref[...])

    o_ref[...] += _swiglu_part(xb_ref[...], wg_ref[...], wu_ref[...], wd_ref[...])
    for src, dst in zip(src_refs, dst_refs):
        dst[...] = src[...].astype(BF16)

    @pl.when(f == pl.num_programs(1) - 1)
    def _():
        y = alpha * x_ref[...] + 0.5 * o_ref[...]
        o_ref[...] = _layer_norm(y, g_ref[...], b_ref[...])


def _ffn_tiles(d_ff):
    for tf in (3 * MXU_WIDTH, 2 * MXU_WIDTH, MXU_WIDTH):
        tail = d_ff % tf
        if tail and tail % LANES == 0 and (d_ff - tail) % tail == 0:
            return tf, tail
    raise NotImplementedError(f"no FFN tiling for d_ff={d_ff}")


def _slab_rows(rows, n_steps):
    for r in range(BF16_ROWS, rows + 1, BF16_ROWS):
        if rows % r == 0 and rows // r <= n_steps:
            return r
    raise NotImplementedError(f"cannot spread {rows} rows over {n_steps} steps")


def _ffn_ln(x, wg, wu, wd, g, b, layer, cast_next=None, *, alpha, tm):
    m, d = x.shape
    d_ff = wg.shape[1]
    tf, tail = _ffn_tiles(d_ff)
    n_f = d_ff // tf
    tail_blk = (d_ff - tail) // tail
    in_specs = [
        pl.BlockSpec((tm, d), lambda i, f: (i, 0)),
        pl.BlockSpec((d, tf), lambda i, f: (0, f)),
        pl.BlockSpec((d, tf), lambda i, f: (0, f)),
        pl.BlockSpec((tf, d), lambda i, f: (f, 0)),
        pl.BlockSpec((d, tail), lambda i, f: (0, tail_blk)),
        pl.BlockSpec((d, tail), lambda i, f: (0, tail_blk)),
        pl.BlockSpec((tail, d), lambda i, f: (tail_blk, 0)),
        _layer_spec((1, d), layer),
        _layer_spec((1, d), layer),
    ]
    out_specs = [pl.BlockSpec((tm, d), lambda i, f: (i, 0))]
    out_shape = [jax.ShapeDtypeStruct((m, d), F32)]
    args = [x, wg, wu, wd, wg, wu, wd, g, b]
    if cast_next is not None:
        srcs, nxt = cast_next
        n_steps = (m // tm) * n_f
        for w in srcs:
            rows, cols = w.shape[1:]
            r = _slab_rows(rows, n_steps)
            slab = lambda i, f, last=rows // r - 1: jnp.minimum(i * n_f + f, last)
            in_specs.append(pl.BlockSpec((None, r, cols), lambda i, f, slab=slab: (nxt, slab(i, f), 0)))
            out_specs.append(pl.BlockSpec((r, cols), lambda i, f, slab=slab: (slab(i, f), 0)))
            out_shape.append(jax.ShapeDtypeStruct((rows, cols), BF16))
            args.append(w)
    outs = pl.pallas_call(
        functools.partial(_ffn_ln_kernel, alpha=alpha, n_cast=len(args) - 9),
        grid=(m // tm, n_f),
        in_specs=in_specs,
        out_specs=out_specs,
        out_shape=out_shape,
        scratch_shapes=[pltpu.VMEM((tm, d), BF16)],
        compiler_params=_params(2),
        name="ffn_ln",
    )(*args)
    return outs[0], tuple(outs[1:])


def _in_proj_kernel(x_ref, *refs):
    w_refs, o_ref = refs[:-1], refs[-1]
    xb = x_ref[...].astype(BF16)
    col = 0
    for w_ref in w_refs:
        n = w_ref.shape[1]
        o_ref[:, col:col + n] = jnp.dot(xb, w_ref[...], preferred_element_type=F32).astype(o_ref.dtype)
        col += n


def _in_proj(x, ws, layer, out_dtype, *, tm, name):
    m, d = x.shape
    n = sum(w.shape[2] for w in ws)
    return pl.pallas_call(
        _in_proj_kernel,
        grid=(m // tm,),
        in_specs=[pl.BlockSpec((tm, d), lambda i: (i, 0))] + [
            pl.BlockSpec((None, d, w.shape[2]), lambda i: (layer, 0, 0), pipeline_mode=pl.Buffered(1)) for w in ws],
        out_specs=pl.BlockSpec((tm, n), lambda i: (i, 0)),
        out_shape=jax.ShapeDtypeStruct((m, n), out_dtype),
        compiler_params=_params(1),
        name=name,
    )(x, *ws)


def _regroup_kernel(wt_ref, *o_refs, runs):
    cb = wt_ref.shape[1]
    eye = (lax.broadcasted_iota(jnp.int32, (cb, cb), 0) == lax.broadcasted_iota(jnp.int32, (cb, cb), 1)).astype(BF16)
    row = lax.broadcasted_iota(jnp.int32, (LANES, cb), 0)
    for o_ref, o_runs in zip(o_refs, runs):
        for blk, (start, width) in enumerate(o_runs):
            rows = wt_ref[start:start + LANES, :]
            if width < LANES:
                rows = jnp.where(row < width, rows, 0.0)
            o_ref[:, blk * LANES:(blk + 1) * LANES] = lax.dot_general(
                eye, rows.astype(BF16), _NT_DIMS, preferred_element_type=F32).astype(o_ref.dtype)


def _regroup(w, runs, *, cb):
    depth, d, n = w.shape
    assert all(s % SUBLANES == 0 and s + LANES <= n for o_runs in runs for s, _ in o_runs)
    return pl.pallas_call(
        functools.partial(_regroup_kernel, runs=tuple(tuple(r) for r in runs)),
        grid=(depth, d // cb),
        in_specs=[pl.BlockSpec((None, n, cb), lambda l, i: (l, 0, i))],
        out_specs=[pl.BlockSpec((None, cb, LANES * len(r)), lambda l, i: (l, i, 0)) for r in runs],
        out_shape=[jax.ShapeDtypeStruct((depth, d, LANES * len(r)), BF16) for r in runs],
        compiler_params=_params(2),
        name="regroup_w_in",
    )(jnp.swapaxes(w, 1, 2))


def _log_sigmoid(z):
    return jnp.minimum(z, 0.0) - jnp.log(1.0 + jnp.exp(-jnp.abs(z)))


def _gla_kernel(q_ref, k_ref, v_ref, gg_ref, r_ref, wf_ref, wb_ref, bf_ref, bb_ref, ng_ref,
                o_ref, of_ref, ob_ref, st_ref, bc_ref, *, seq, tile, chunk, hk, hv, q_scale, inv_dv):
    n_tiles = seq // tile
    n_chunks = tile // chunk
    n_heads = q_ref.shape[-1] // hk
    shift = chunk.bit_length() - 1
    row = lax.broadcasted_iota(jnp.int32, (tile, tile), 0)
    col = lax.broadcasted_iota(jnp.int32, (tile, tile), 1)
    same_chunk = lax.shift_right_logical(row, shift) == lax.shift_right_logical(col, shift)
    tris = (same_chunk & (col <= row), same_chunk & (col >= row))
    oacc_refs = (of_ref, ob_ref)
    w_refs, b_refs = (wf_ref, wb_ref), (bf_ref, bb_ref)
    chains = [(h, dn) for h in range(n_heads) for dn in (0, 1)]
    every = range(len(chains))
    kcols = [slice(h * hk, (h + 1) * hk) for h, _ in chains]
    vcols = [slice(h * hv, (h + 1) * hv) for h, _ in chains]

    def tile_rows(i):
        return [pl.ds(pl.multiple_of((i if dn == 0 else n_tiles - 1 - i) * tile, tile), tile) for _, dn in chains]

    def gate_logits(rows):
        return [jnp.dot(r_ref[rows[c], :], w_refs[dn][h], preferred_element_type=F32) + b_refs[dn][h]
                for c, (h, dn) in enumerate(chains)]

    def chunk_cumsums(z):
        la = [_log_sigmoid(zc) * (1.0 / GLA_TAU) for zc in z]
        la_hi = [x.astype(BF16) for x in la]
        la_lo = [(x - hi.astype(F32)).astype(BF16) for x, hi in zip(la, la_hi)]
        cum = [jnp.dot(tris[dn].astype(BF16), jnp.concatenate([la_hi[c], la_lo[c]], axis=1),
                       preferred_element_type=F32) for c, (_, dn) in enumerate(chains)]
        return [x[:, :hk] + x[:, hk:] for x in cum]

    def step(i):
        rows = tile_rows(i)
        bcum = [bc_ref[c] for c in every]
        z_next = gate_logits(tile_rows(jnp.minimum(i + 1, n_tiles - 1)))

        vb = [v_ref[rows[c], vcols[c]] for c in every]
        q_dec = [(q_ref[rows[c], kcols[c]].astype(F32) * q_scale * jnp.exp(bcum[c])).astype(BF16) for c in every]
        k_inv = [k_ref[rows[c], kcols[c]].astype(F32) * jnp.exp(-bcum[c]) for c in every]
        att = [lax.dot_general(q_dec[c], k_inv[c].astype(BF16), _NT_DIMS, preferred_element_type=F32) for c in every]
        bcum_next = chunk_cumsums(z_next)
        att = [jnp.where(tris[dn], att[c], 0.0).astype(BF16) for c, (_, dn) in enumerate(chains)]
        o_intra = [jnp.dot(att[c], vb[c], preferred_element_type=F32) for c in every]

        o_inter = [[None] * n_chunks for _ in every]
        for s in range(n_chunks):
            for c, (_, dn) in enumerate(chains):
                ck = s if dn == 0 else n_chunks - 1 - s
                lo = ck * chunk
                end_row = lo + chunk - 1 if dn == 0 else lo
                e_end = jnp.exp(bcum[c][end_row:end_row + 1, :])
                st = st_ref[c]
                o_inter[c][ck] = lax.dot_general(q_dec[c][lo:lo + chunk], st.astype(BF16), _NT_DIMS,
                                                 preferred_element_type=F32)
                k_end = (k_inv[c][lo:lo + chunk] * e_end).astype(BF16)
                upd = lax.dot_general(vb[c][lo:lo + chunk], k_end, _TN_DIMS, preferred_element_type=F32)
                st_ref[c] = st * e_end + upd
        for c, (_, dn) in enumerate(chains):
            oacc_refs[dn][rows[c], vcols[c]] = o_intra[c] + jnp.concatenate(o_inter[c], axis=0)
            bc_ref[c] = bcum_next[c]

    st_ref[...] = jnp.zeros_like(st_ref)
    for c, x in enumerate(chunk_cumsums(gate_logits(tile_rows(0)))):
        bc_ref[c] = x

    def body(i, carry):
        step(i)
        return carry

    lax.fori_loop(0, n_tiles, body, 0)

    def finish(i, carry):
        rows = pl.ds(pl.multiple_of(i * tile, tile), tile)
        for h in range(n_heads):
            cols = slice(h * hv, (h + 1) * hv)
            o = of_ref[rows, cols] + ob_ref[rows, cols]
            ms = jnp.sum(o * o, axis=-1, keepdims=True) * inv_dv
            gate = gg_ref[rows, cols].astype(F32)
            o = o * lax.rsqrt(ms + LN_EPS) * ng_ref[h] * (gate * jax.nn.sigmoid(gate))
            o_ref[rows, cols] = o.astype(o_ref.dtype)
        return carry

    lax.fori_loop(0, n_tiles, finish, 0)


def _gla(proj, wdec_f, wdec_b, bdec_f, bdec_b, norm_g, layer, *, batch, seq, offs, hk, hv, q_scale, dv_head):
    heads = wdec_f.shape[1]
    hps = 2 if heads % 2 == 0 else 1
    tile = min(256, seq)
    kern = functools.partial(_gla_kernel, seq=seq, tile=tile, chunk=GLA_CHUNK, hk=hk, hv=hv, q_scale=q_scale,
                             inv_dv=1.0 / dv_head)
    wk, wv = hps * hk, hps * hv
    assert all(offs[n] % w == 0 for n, w in (("gq", wk), ("gk", wk), ("gv", wv), ("gg", wv)))
    cq, ck, cv, cg, cr = offs["gq"] // wk, offs["gk"] // wk, offs["gv"] // wv, offs["gg"] // wv, offs["r"] // LANES
    per_head = lambda rows, cols: pl.BlockSpec((None, hps, rows, cols), lambda b, g: (layer, g, 0, 0))
    return pl.pallas_call(
        kern,
        grid=(batch, heads // hps),
        in_specs=[
            pl.BlockSpec((seq, wk), lambda b, g: (b, cq + g)),
            pl.BlockSpec((seq, wk), lambda b, g: (b, ck + g)),
            pl.BlockSpec((seq, wv), lambda b, g: (b, cv + g)),
            pl.BlockSpec((seq, wv), lambda b, g: (b, cg + g)),
            pl.BlockSpec((seq, LANES), lambda b, g: (b, cr)),
            per_head(LANES, hk), per_head(LANES, hk),
            per_head(1, hk), per_head(1, hk), per_head(1, hv),
        ],
        out_specs=pl.BlockSpec((seq, wv), lambda b, g: (b, g)),
        out_shape=jax.ShapeDtypeStruct((batch * seq, heads * hv), BF16),
        scratch_shapes=[pltpu.VMEM((seq, wv), F32), pltpu.VMEM((seq, wv), F32),
                        pltpu.VMEM((2 * hps, hv, hk), F32), pltpu.VMEM((2 * hps, tile, hk), F32)],
        compiler_params=_params(2),
        name="gla",
    )(proj, proj, proj, proj, proj, wdec_f, wdec_b, bdec_f, bdec_b, norm_g)


def _dil_cfgs(seq):
    cfgs = []
    for w, d in DIL_BRANCHES:
        reach = (w // 2) // d
        length = seq // d
        qb = min(128, length)
        kb = min(length, qb + 2 * reach)
        n_qb = length // qb
        rel_starts = sorted({min(max(jb * qb - reach, 0), length - kb) - jb * qb for jb in range(n_qb)},
                            reverse=True)
        cfgs.append((d, reach, length, qb, kb, n_qb, rel_starts))
    return cfgs


def _dil_kernel(slopes_ref, q_ref, k_ref, v_ref, o_ref, *stats, seq, scale):
    n_br = len(DIL_BRANCHES)
    m_refs, l_refs, acc_refs, bias_refs = (stats[i * n_br:(i + 1) * n_br] for i in range(4))
    qkv16 = [stats[4 * n_br + 3 * n:4 * n_br + 3 * n + 3] for n in range(n_br)]
    qkv32 = [None, *[stats[7 * n_br + 3 * (n - 1):7 * n_br + 3 * n] for n in range(1, n_br - 1)]]
    slope = slopes_ref[pl.program_id(1)]
    hd = q_ref.shape[-1]
    dils = [d for _, d in DIL_BRANCHES]

    cfgs = _dil_cfgs(seq)
    for n, (d, reach, length, qb, kb, n_qb, rel_starts) in enumerate(cfgs):
        rel = lax.broadcasted_iota(jnp.int32, (qb, kb), 1) - lax.broadcasted_iota(jnp.int32, (qb, kb), 0)
        for vi, rs in enumerate(rel_starts):
            dist = jnp.abs(rel + rs)
            bias_refs[n][vi] = jnp.where(dist <= reach, -(slope * float(d)) * dist.astype(F32), MASK_VALUE)

    rt = min(256, seq)
    srcs = (q_ref, k_ref, v_ref)
    for a in range(3):
        for t0 in range(0, seq, rt):
            x = srcs[a][t0:t0 + rt, :]
            qkv16[0][a][t0:t0 + rt, :] = (x * scale if a == 0 else x).astype(BF16)
    for n in range(1, n_br):
        assert dils[n] % dils[n - 1] == 0
        step, l_prev, l_cur = dils[n] // dils[n - 1], seq // dils[n - 1], seq // dils[n]
        for a in range(3):
            src = srcs[a] if n == 1 else qkv32[n - 1][a]
            for bp in range(dils[n - 1]):
                for r2 in range(step):
                    for t0 in range(0, l_cur, rt):
                        rows = min(rt, l_cur - t0)
                        x = src[pl.ds(bp * l_prev + r2 + step * t0, rows, stride=step), :]
                        dst = (bp * step + r2) * l_cur + t0
                        if n < n_br - 1:
                            qkv32[n][a][dst:dst + rows, :] = x
                        qkv16[n][a][dst:dst + rows, :] = (x * scale if a == 0 else x).astype(BF16)

    def _residue(n, b):
        if n == 0:
            return 0
        step = dils[n] // dils[n - 1]
        return lax.rem(b, jnp.int32(step)) * dils[n - 1] + _residue(n - 1, lax.div(b, jnp.int32(step)))

    def blocks(jobs):
        place = []
        for n, idx in jobs:
            d, reach, length, qb, kb, n_qb, rel_starts = cfgs[n]
            if d == 1:
                blk, jb = 0, idx
            else:
                blk, jb = lax.div(idx, jnp.int32(n_qb)), lax.rem(idx, jnp.int32(n_qb))
            i0 = jb * qb
            ks = jnp.clip(i0 - reach, 0, length - kb)
            variant = sum(((ks - i0) <= rs).astype(jnp.int32) for rs in rel_starts[1:])
            base = blk * length
            qrows = pl.ds(pl.multiple_of(base + i0, qb), qb)
            krows = pl.ds(pl.multiple_of(base + ks, 16), kb)
            orows = qrows if d == 1 else pl.ds(_residue(n, blk) + d * i0, qb, stride=d)
            place.append((n, qrows, krows, orows, variant, qb, kb))
        s = []
        for n, qrows, krows, orows, variant, qb, kb in place:
            s.append(lax.dot_general(qkv16[n][0][qrows, :], qkv16[n][1][krows, :], _NT_DIMS,
                                     preferred_element_type=F32) + bias_refs[n][variant])
        m = [jnp.max(sj, axis=-1, keepdims=True) for sj in s]
        p = [jnp.exp(sj - mj).astype(BF16) for sj, mj in zip(s, m)]
        for (n, qrows, krows, orows, variant, qb, kb), mj, pj in zip(place, m, p):
            v_ones = jnp.concatenate([qkv16[n][2][krows, :], jnp.ones((kb, hd), BF16)], axis=1)
            acc_l = jnp.dot(pj, v_ones, preferred_element_type=F32)
            acc_refs[n][orows, :] = acc_l[:, :hd]
            l_refs[n][orows, :] = acc_l[:, hd:]
            m_refs[n][orows, :] = jnp.broadcast_to(mj, (qb, hd))

    counts = [seq // min(128, seq // d) for _, d in DIL_BRANCHES]
    group = 4
    if len(set(counts)) == 1 and counts[0] % group == 0:
        def body(i, carry):
            blocks([(n, group * i + u) for u in range(group) for n in range(n_br)])
            return carry
        lax.fori_loop(0, counts[0] // group, body, 0)
    else:
        for n in range(n_br):
            def body(idx, carry, n=n):
                blocks([(n, idx)])
                return carry
            lax.fori_loop(0, counts[n], body, 0)

    rt = min(256, seq)

    def finish(i, carry):
        rows = pl.ds(pl.multiple_of(i * rt, rt), rt)
        ms = [m_ref[rows, :] for m_ref in m_refs]
        m_all = functools.reduce(jnp.maximum, ms)
        num = jnp.zeros((rt, hd), F32)
        den = jnp.zeros((rt, hd), F32)
        for n in range(n_br):
            wgt = jnp.exp(ms[n] - m_all)
            num += wgt * acc_refs[n][rows, :]
            den += wgt * l_refs[n][rows, :]
        o_ref[rows, :] = (num / den).astype(o_ref.dtype)
        return carry

    lax.fori_loop(0, seq // rt, finish, 0)


def _dil(proj, slopes, *, batch, seq, offs, hd):
    cq, ck, cv = offs["dq"] // hd, offs["dk"] // hd, offs["dv"] // hd
    kern = functools.partial(_dil_kernel, seq=seq, scale=hd ** -0.5)
    return pl.pallas_call(
        kern,
        grid=(batch, DIL_HEADS),
        in_specs=[
            pl.BlockSpec(memory_space=pltpu.SMEM),
            pl.BlockSpec((seq, hd), lambda b, h: (b, cq + h)),
            pl.BlockSpec((seq, hd), lambda b, h: (b, ck + h)),
            pl.BlockSpec((seq, hd), lambda b, h: (b, cv + h)),
        ],
        out_specs=pl.BlockSpec((seq, hd), lambda b, h: (b, h)),
        out_shape=jax.ShapeDtypeStruct((batch * seq, DIL_HEADS * hd), BF16),
        scratch_shapes=([pltpu.VMEM((seq, hd), F32)] * (3 * len(DIL_BRANCHES))
                        + [pltpu.VMEM((len(c[6]), c[3], c[4]), F32) for c in _dil_cfgs(seq)]
                        + [pltpu.VMEM((seq, hd), BF16)] * (3 * len(DIL_BRANCHES))
                        + [pltpu.VMEM((seq, hd), F32)] * (3 * (len(DIL_BRANCHES) - 2))),
        compiler_params=_params(2),
        name="dil",
    )(slopes, proj, proj, proj)


def _conv_kernel(prev_ref, cur_ref, next_ref, w_ref, cb_ref, g_ref, b_ref, o_ref, u_ref, *, halo, taps):
    i = pl.program_id(1)
    c = o_ref.shape[-1]
    tq = o_ref.shape[0]

    def glu(ref):
        x = ref[...]
        return x[:, :c] * jax.nn.sigmoid(x[:, c:])

    ext = tq + 2 * halo
    u_ref[0, 0:halo, :] = jnp.where(i > 0, glu(prev_ref), 0.0)
    u_ref[0, halo:halo + tq, :] = glu(cur_ref)
    u_ref[0, halo + tq:, :] = jnp.where(i < pl.num_programs(1) - 1, glu(next_ref), 0.0)
    for s in range(1, SUBLANES):
        u_ref[s, 0:ext - SUBLANES, :] = u_ref[0, pl.ds(s, ext - SUBLANES), :]

    pad = taps // 2
    rc = min(32, tq)
    first = halo - pad
    span = (first + taps - 1) // SUBLANES * SUBLANES

    def chunk(j, carry):
        r0 = pl.multiple_of(j * rc, rc)
        acc = jnp.zeros((rc, c), F32) + cb_ref[...]
        for s in range(SUBLANES):
            win = u_ref[s, pl.ds(r0, rc + span), :]
            for t in range(taps):
                a, ts = divmod(first + t, SUBLANES)
                if ts == s:
                    acc = acc + win[a * SUBLANES:a * SUBLANES + rc] * w_ref[t:t + 1, :]
        y = _layer_norm(acc, g_ref[...], b_ref[...])
        o_ref[pl.ds(r0, rc), :] = (y * jax.nn.sigmoid(y)).astype(o_ref.dtype)
        return carry

    lax.fori_loop(0, tq // rc, chunk, 0, unroll=2)


def _conv(proj, conv_w, conv_b, ln_g, ln_b, layer, *, batch, seq, offs):
    taps, c = conv_w.shape[1:]
    halo = 16
    assert taps // 2 <= halo and halo % SUBLANES == 0
    tq = min(512, seq)
    nt = seq // tq
    hb = tq // halo
    ccol = offs["cv"] // (2 * c)
    kern = functools.partial(_conv_kernel, halo=halo, taps=taps)
    return pl.pallas_call(
        kern,
        grid=(batch, nt),
        in_specs=[
            pl.BlockSpec((halo, 2 * c), lambda b, i: (jnp.maximum((b * nt + i) * hb - 1, 0), ccol)),
            pl.BlockSpec((tq, 2 * c), lambda b, i: (b * nt + i, ccol)),
            pl.BlockSpec((halo, 2 * c),
                         lambda b, i: (jnp.minimum((b * nt + i + 1) * hb, batch * nt * hb - 1), ccol)),
            _layer_spec((taps, c), layer),
            _layer_spec((1, c), layer),
            _layer_spec((1, c), layer),
            _layer_spec((1, c), layer),
        ],
        out_specs=pl.BlockSpec((tq, c), lambda b, i: (b * nt + i, 0)),
        out_shape=jax.ShapeDtypeStruct((batch * seq, c), BF16),
        scratch_shapes=[pltpu.VMEM((SUBLANES, tq + 2 * halo, c), F32)],
        compiler_params=_params(2),
        name="conv",
    )(proj, proj, proj, conv_w, conv_b, ln_g, ln_b)


def _out_ln_kernel(x_ref, a_ref, d_ref, c_ref, w_ref, g_ref, b_ref, o_ref, *, alpha, heads, hdv):
    hv, nd = a_ref.shape[1] // heads, d_ref.shape[1]
    na = heads * hdv
    tm = x_ref.shape[0]
    sub = tm // 2 if tm % (2 * BF16_ROWS) == 0 else tm
    for r0 in range(0, tm, sub):
        rows = slice(r0, r0 + sub)
        mix = jnp.dot(d_ref[rows, :], w_ref[na:na + nd, :], preferred_element_type=F32)
        mix += jnp.dot(c_ref[rows, :], w_ref[na + nd:, :], preferred_element_type=F32)
        for h in range(heads):
            mix += jnp.dot(a_ref[rows, h * hv:(h + 1) * hv], w_ref[h * hdv:h * hdv + hv, :],
                           preferred_element_type=F32)
        o_ref[rows, :] = _layer_norm(alpha * x_ref[rows, :] + mix, g_ref[...], b_ref[...])


def _out_ln(x, gla_o, dil_o, conv_o, w, g, b, layer, *, alpha, tm, heads, hdv):
    m, d = x.shape
    row = lambda i: (i, 0)
    assert (heads - 1) * hdv + gla_o.shape[1] // heads <= w.shape[1] and hdv % BF16_ROWS == 0
    return pl.pallas_call(
        functools.partial(_out_ln_kernel, alpha=alpha, heads=heads, hdv=hdv),
        grid=(m // tm,),
        in_specs=[
            pl.BlockSpec((tm, d), row),
            pl.BlockSpec((tm, gla_o.shape[1]), row),
            pl.BlockSpec((tm, dil_o.shape[1]), row),
            pl.BlockSpec((tm, conv_o.shape[1]), row),
            _layer_spec(w.shape[1:], layer),
            _layer_spec((1, d), layer),
            _layer_spec((1, d), layer),
        ],
        out_specs=pl.BlockSpec((tm, d), row),
        out_shape=jax.ShapeDtypeStruct((m, d), F32),
        compiler_params=_params(1),
        name="out_ln",
    )(x, gla_o, dil_o, conv_o, w, g, b)


def _pad_heads(w, heads, width):
    lead = w.shape[:-1]
    hd = w.shape[-1] // heads
    w = w.reshape(lead + (heads, hd))
    w = jnp.pad(w, [(0, 0)] * len(lead) + [(0, 0), (0, width - hd)])
    return w.reshape(lead + (heads * width,))


def kernel(x, ffn1_w_gate, ffn1_w_up, ffn1_w_down, ln1_g, ln1_b, w_in, gla_decay_w_fwd, gla_decay_b_fwd, gla_decay_w_bwd, gla_decay_b_bwd, gla_norm_g, conv_w, conv_b, conv_ln_g, conv_ln_b, w_out, ln2_g, ln2_b, ffn2_w_gate, ffn2_w_up, ffn2_w_down, ln3_g, ln3_b):
    batch, seq, d_model = x.shape
    depth = ffn1_w_gate.shape[0]
    rank, gla_dk = gla_decay_w_fwd.shape[1:]
    gla_dv = gla_norm_g.shape[1]
    conv_c = conv_w.shape[2]
    dil_dim = w_out.shape[1] - gla_dv - conv_c
    dil_hd = dil_dim // DIL_HEADS
    hdk, hdv = gla_dk // GLA_HEADS, gla_dv // GLA_HEADS
    hk, hv = _round_up(hdk, LANES), _round_up(hdv, LANES)
    assert dil_hd == LANES and 2 * rank <= LANES and seq % GLA_CHUNK == 0

    alpha = (2.0 * depth) ** 0.25
    m = batch * seq
    tm = min(512, m)

    def layout(widths):
        offs, total = {}, 0
        for name, wdt in widths:
            offs[name] = total
            total += wdt
        return offs

    offs_a = layout((("gq", GLA_HEADS * hk), ("gk", GLA_HEADS * hk), ("gv", GLA_HEADS * hv),
                     ("gg", GLA_HEADS * hv), ("r", LANES)))
    offs_b = layout((("cv", 2 * conv_c), ("dq", dil_dim), ("dk", dil_dim), ("dv", dil_dim)))

    src = layout((("gq", gla_dk), ("gk", gla_dk), ("gv", gla_dv), ("r", 2 * rank), ("gg", gla_dv),
                  ("dil", 3 * dil_dim), ("cv", 2 * conv_c)))
    runs = []
    for name, hd in (("gq", hdk), ("gk", hdk), ("gv", hdv), ("gg", hdv)):
        for h in range(GLA_HEADS):
            runs += [(src[name] + h * hd + c0, min(LANES, hd - c0)) for c0 in range(0, _round_up(hd, LANES), LANES)]
    runs.append((src["r"], 2 * rank))
    assert len(runs) * LANES == offs_a["r"] + LANES
    runs_b = [(src[name] + c0, LANES) for name, wdt in (("cv", 2 * conv_c), ("dil", 3 * dil_dim))
              for c0 in range(0, wdt, LANES)]
    w_in_a, w_in_b = _regroup(w_in, (runs, runs_b), cb=min(MXU_WIDTH, d_model))

    def head_mats(wdec, row0):
        w4 = _pad_heads(wdec, GLA_HEADS, hk).reshape(depth, rank, GLA_HEADS, hk).transpose(0, 2, 1, 3)
        return jnp.pad(w4, ((0, 0), (0, 0), (row0, LANES - rank - row0), (0, 0))).astype(BF16)

    wdec_f, wdec_b = head_mats(gla_decay_w_fwd, 0), head_mats(gla_decay_w_bwd, rank)
    bdec_f = _pad_heads(gla_decay_b_fwd, GLA_HEADS, hk).reshape(depth, GLA_HEADS, 1, hk)
    bdec_b = _pad_heads(gla_decay_b_bwd, GLA_HEADS, hk).reshape(depth, GLA_HEADS, 1, hk)
    norm_g = _pad_heads(gla_norm_g, GLA_HEADS, hv).reshape(depth, GLA_HEADS, 1, hv)

    w_out_b = w_out.astype(BF16)

    ffn_src = ((ffn1_w_gate, ffn1_w_up, ffn1_w_down), (ffn2_w_gate, ffn2_w_up, ffn2_w_down))
    ffn_w = tuple(w[0].astype(BF16) for w in ffn_src[0])
    vec = lambda p: p[:, None, :]
    slopes = jnp.asarray(2.0 ** (-8.0 * np.arange(1, DIL_HEADS + 1) / DIL_HEADS), F32)

    xf = x.reshape(m, d_model)
    for l in range(depth):
        xf, ffn_w = _ffn_ln(xf, *ffn_w, vec(ln1_g), vec(ln1_b), l, (ffn_src[1], l), alpha=alpha, tm=tm)
        proj_a = _in_proj(xf, (w_in_a,), l, BF16, tm=tm, name="in_proj_gla")
        proj_b = _in_proj(xf, (w_in_b,), l, F32, tm=tm, name="in_proj_mix")
        gla_o = _gla(proj_a, wdec_f, wdec_b, bdec_f, bdec_b, norm_g, l, batch=batch, seq=seq, offs=offs_a,
                     hk=hk, hv=hv, q_scale=hdk ** -0.5, dv_head=hdv)
        dil_o = _dil(proj_b, slopes, batch=batch, seq=seq, offs=offs_b, hd=dil_hd)
        conv_o = _conv(proj_b, conv_w, vec(conv_b), vec(conv_ln_g), vec(conv_ln_b), l, batch=batch, seq=seq,
                       offs=offs_b)
        xf = _out_ln(xf, gla_o, dil_o, conv_o, w_out_b, vec(ln2_g), vec(ln2_b), l, alpha=alpha, tm=tm,
                     heads=GLA_HEADS, hdv=hdv)
        nxt = (ffn_src[0], l + 1) if l + 1 < depth else None
        xf, ffn_w = _ffn_ln(xf, *ffn_w, vec(ln3_g), vec(ln3_b), l, nxt, alpha=alpha, tm=tm)
    return xf.reshape(batch, seq, d_model)
```

```python
import functools

import numpy as np
import jax
import jax.numpy as jnp
from jax import lax
from jax.experimental import pallas as pl
from jax.experimental.pallas import tpu as pltpu

GLA_HEADS = 4
GLA_TAU = 16.0
GLA_CHUNK = 64
DIL_HEADS = 6
DIL_BRANCHES = ((128, 1), (512, 4), (2048, 16))
LN_EPS = 1e-5
MASK_VALUE = -1e30

LANES = 128
SUBLANES = 8
BF16_ROWS = 16
MXU_WIDTH = 256
VMEM_LIMIT_BYTES = 56 * 1024 * 1024

F32 = jnp.float32
BF16 = jnp.bfloat16
_NT_DIMS = (((1,), (1,)), ((), ()))
_TN_DIMS = (((0,), (0,)), ((), ()))


def _round_up(n, m):
    return (n + m - 1) // m * m


def _layer_norm(y, g, b, scale=1.0):
    mu = jnp.mean(y, axis=-1, keepdims=True)
    yc = y - mu
    var = jnp.mean(yc * yc, axis=-1, keepdims=True)
    return yc * lax.rsqrt(var + scale * scale * LN_EPS) * g + b


def _params(n_grid_axes):
    return pltpu.CompilerParams(dimension_semantics=("arbitrary",) * n_grid_axes,
                                vmem_limit_bytes=VMEM_LIMIT_BYTES)


def _layer_spec(shape, layer):
    zeros = (0,) * len(shape)
    return pl.BlockSpec((None,) + tuple(shape), lambda *_: (layer,) + zeros)


def _swiglu_part(xb, wg, wu, wd):
    gate = jnp.dot(xb, wg, preferred_element_type=F32)
    up = jnp.dot(xb, wu, preferred_element_type=F32)
    h = (gate * jax.nn.sigmoid(gate) * up).astype(BF16)
    return jnp.dot(h, wd, preferred_element_type=F32)


def _ffn_ln_kernel(x_ref, wg_ref, wu_ref, wd_ref, wgt_ref, wut_ref, wdt_ref, g_ref, b_ref, *rest, alpha, n_cast):
    src_refs, o_ref, dst_refs, xb_ref, acc_ref = rest[:n_cast], rest[n_cast], rest[n_cast + 1:-2], rest[-2], rest[-1]
    f = pl.program_id(1)

    @pl.when(f == 0)
    def _():
        x = x_ref[...]
        xb = x.astype(BF16)
        xb_ref[...] = xb
        acc_ref[...] = _swiglu_part(xb, wgt_ref[...], wut_ref[...], wdt_ref[...]) + (2.0 * alpha) * x

    @pl.when(f >= 0)
    def _():
        acc_ref[...] += _swiglu_part(xb_ref[...], wg_ref[...], wu_ref[...], wd_ref[...])
        for src, dst in zip(src_refs, dst_refs):
            dst[...] = src[...].astype(BF16)

    @pl.when(f == pl.num_programs(1) - 1)
    def _():
        o_ref[...] = _layer_norm(acc_ref[...], g_ref[...], b_ref[...], scale=2.0)


def _ffn_tiles(d_ff):
    for tf in (3 * MXU_WIDTH, 2 * MXU_WIDTH, MXU_WIDTH):
        tail = d_ff % tf
        if tail and tail % LANES == 0 and (d_ff - tail) % tail == 0:
            return tf, tail
    raise NotImplementedError(f"no FFN tiling for d_ff={d_ff}")


def _slab_rows(rows, n_steps):
    for r in range(BF16_ROWS, rows + 1, BF16_ROWS):
        if rows % r == 0 and rows // r <= n_steps:
            return r
    raise NotImplementedError(f"cannot spread {rows} rows over {n_steps} steps")


def _ffn_ln(x, wg, wu, wd, g, b, layer, cast_next=None, *, alpha, tm):
    m, d = x.shape
    d_ff = wg.shape[1]
    tf, tail = _ffn_tiles(d_ff)
    n_f = d_ff // tf
    tail_blk = (d_ff - tail) // tail
    n_tiles = m // tm
    in_specs = [
        pl.BlockSpec((tm, d), lambda i, f: (i, 0)),
        pl.BlockSpec((d, tf), lambda i, f: (0, f)),
        pl.BlockSpec((d, tf), lambda i, f: (0, f)),
        pl.BlockSpec((tf, d), lambda i, f: (f, 0)),
        pl.BlockSpec((d, tail), lambda i, f: (0, tail_blk)),
        pl.BlockSpec((d, tail), lambda i, f: (0, tail_blk)),
        pl.BlockSpec((tail, d), lambda i, f: (tail_blk, 0)),
        _layer_spec((1, d), layer),
        _layer_spec((1, d), layer),
    ]
    out_specs = [pl.BlockSpec((tm, d), lambda i, f: (i, 0), pipeline_mode=pl.Buffered(1))]
    out_shape = [jax.ShapeDtypeStruct((m, d), F32)]
    args = [x, wg, wu, wd, wg, wu, wd, g, b]
    if cast_next is not None:
        srcs, nxt = cast_next
        n_steps = n_tiles * n_f
        for w in srcs:
            rows, cols = w.shape[1:]
            r = _slab_rows(rows, n_steps)
            slab = lambda i, f, last=rows // r - 1: jnp.minimum(i * n_f + f, last)
            in_specs.append(pl.BlockSpec((None, r, cols), lambda i, f, slab=slab: (nxt, slab(i, f), 0)))
            out_specs.append(pl.BlockSpec((r, cols), lambda i, f, slab=slab: (slab(i, f), 0)))
            out_shape.append(jax.ShapeDtypeStruct((rows, cols), BF16))
            args.append(w)
    outs = pl.pallas_call(
        functools.partial(_ffn_ln_kernel, alpha=alpha, n_cast=len(args) - 9),
        grid=(n_tiles, n_f),
        in_specs=in_specs,
        out_specs=out_specs,
        out_shape=out_shape,
        scratch_shapes=[pltpu.VMEM((tm, d), BF16), pltpu.VMEM((tm, d), F32)],
        compiler_params=_params(2),
        name="ffn_ln",
    )(*args)
    return outs[0], tuple(outs[1:])


def _in_proj_kernel(x_ref, *refs):
    w_refs, o_ref = refs[:-1], refs[-1]
    xb = x_ref[...].astype(BF16)
    col = 0
    for w_ref in w_refs:
        n = w_ref.shape[1]
        o_ref[:, col:col + n] = jnp.dot(xb, w_ref[...], preferred_element_type=F32).astype(o_ref.dtype)
        col += n


def _in_proj(x, ws, layer, out_dtype, *, tm, name):
    m, d = x.shape
    n = sum(w.shape[2] for w in ws)
    return pl.pallas_call(
        _in_proj_kernel,
        grid=(m // tm,),
        in_specs=[pl.BlockSpec((tm, d), lambda i: (i, 0))] + [
            pl.BlockSpec((None, d, w.shape[2]), lambda i: (layer, 0, 0), pipeline_mode=pl.Buffered(1)) for w in ws],
        out_specs=pl.BlockSpec((tm, n), lambda i: (i, 0)),
        out_shape=jax.ShapeDtypeStruct((m, n), out_dtype),
        compiler_params=_params(1),
        name=name,
    )(x, *ws)


def _regroup_kernel(wt_ref, *o_refs, runs):
    cb = wt_ref.shape[1]
    eye = (lax.broadcasted_iota(jnp.int32, (cb, cb), 0) == lax.broadcasted_iota(jnp.int32, (cb, cb), 1)).astype(BF16)
    row = lax.broadcasted_iota(jnp.int32, (LANES, cb), 0)
    for o_ref, o_runs in zip(o_refs, runs):
        for blk, (start, width) in enumerate(o_runs):
            rows = wt_ref[start:start + LANES, :]
            if width < LANES:
                rows = jnp.where(row < width, rows, 0.0)
            o_ref[:, blk * LANES:(blk + 1) * LANES] = lax.dot_general(
                eye, rows.astype(BF16), _NT_DIMS, preferred_element_type=F32).astype(o_ref.dtype)


def _regroup(w, runs, *, cb):
    depth, d, n = w.shape
    assert all(s % SUBLANES == 0 and s + LANES <= n for o_runs in runs for s, _ in o_runs)
    return pl.pallas_call(
        functools.partial(_regroup_kernel, runs=tuple(tuple(r) for r in runs)),
        grid=(depth, d // cb),
        in_specs=[pl.BlockSpec((None, n, cb), lambda l, i: (l, 0, i))],
        out_specs=[pl.BlockSpec((None, cb, LANES * len(r)), lambda l, i: (l, i, 0)) for r in runs],
        out_shape=[jax.ShapeDtypeStruct((depth, d, LANES * len(r)), BF16) for r in runs],
        compiler_params=_params(2),
        name="regroup_w_in",
    )(jnp.swapaxes(w, 1, 2))


def _log_sigmoid(z):
    return jnp.minimum(z, 0.0) - jnp.log(1.0 + jnp.exp(-jnp.abs(z)))


def _gla_kernel(q_ref, k_ref, v_ref, gg_ref, r_ref, wf_ref, wb_ref, bf_ref, bb_ref, ng_ref,
                o_ref, of_ref, ob_ref, st_ref, bc_ref, *, seq, tile, chunk, hk, hv, q_scale, inv_dv):
    n_tiles = seq // tile
    n_chunks = tile // chunk
    n_heads = q_ref.shape[-1] // hk
    shift = chunk.bit_length() - 1
    row = lax.broadcasted_iota(jnp.int32, (tile, tile), 0)
    col = lax.broadcasted_iota(jnp.int32, (tile, tile), 1)
    same_chunk = lax.shift_right_logical(row, shift) == lax.shift_right_logical(col, shift)
    tris = (same_chunk & (col <= row), same_chunk & (col >= row))
    oacc_refs = (of_ref, ob_ref)
    w_refs, b_refs = (wf_ref, wb_ref), (bf_ref, bb_ref)
    chains = [(h, dn) for h in range(n_heads) for dn in (0, 1)]
    every = range(len(chains))
    kcols = [slice(h * hk, (h + 1) * hk) for h, _ in chains]
    vcols = [slice(h * hv, (h + 1) * hv) for h, _ in chains]

    def tile_rows(i):
        return [pl.ds(pl.multiple_of((i if dn == 0 else n_tiles - 1 - i) * tile, tile), tile) for _, dn in chains]

    def gate_logits(rows):
        return [jnp.dot(r_ref[rows[c], :], w_refs[dn][h], preferred_element_type=F32) + b_refs[dn][h]
                for c, (h, dn) in enumerate(chains)]

    def chunk_cumsums(z):
        la = [_log_sigmoid(zc) * (1.0 / GLA_TAU) for zc in z]
        la_hi = [x.astype(BF16) for x in la]
        la_lo = [(x - hi.astype(F32)).astype(BF16) for x, hi in zip(la, la_hi)]
        cum = [jnp.dot(tris[dn].astype(BF16), jnp.concatenate([la_hi[c], la_lo[c]], axis=1),
                       preferred_element_type=F32) for c, (_, dn) in enumerate(chains)]
        return [x[:, :hk] + x[:, hk:] for x in cum]

    def step(i):
        rows = tile_rows(i)
        bcum = [bc_ref[c] for c in every]
        z_next = gate_logits(tile_rows(jnp.minimum(i + 1, n_tiles - 1)))

        vb = [v_ref[rows[c], vcols[c]] for c in every]
        q_dec = [(q_ref[rows[c], kcols[c]].astype(F32) * q_scale * jnp.exp(bcum[c])).astype(BF16) for c in every]
        k_inv = [k_ref[rows[c], kcols[c]].astype(F32) * jnp.exp(-bcum[c]) for c in every]
        att = [lax.dot_general(q_dec[c], k_inv[c].astype(BF16), _NT_DIMS, preferred_element_type=F32) for c in every]
        bcum_next = chunk_cumsums(z_next)
        att = [jnp.where(tris[dn], att[c], 0.0).astype(BF16) for c, (_, dn) in enumerate(chains)]
        o_intra = [jnp.dot(att[c], vb[c], preferred_element_type=F32) for c in every]

        o_inter = [[None] * n_chunks for _ in every]
        for s in range(n_chunks):
            for c, (_, dn) in enumerate(chains):
                ck = s if dn == 0 else n_chunks - 1 - s
                lo = ck * chunk
                end_row = lo + chunk - 1 if dn == 0 else lo
                e_end = jnp.exp(bcum[c][end_row:end_row + 1, :])
                st = st_ref[c]
                o_inter[c][ck] = lax.dot_general(q_dec[c][lo:lo + chunk], st.astype(BF16), _NT_DIMS,
                                                 preferred_element_type=F32)
                k_end = (k_inv[c][lo:lo + chunk] * e_end).astype(BF16)
                upd = lax.dot_general(vb[c][lo:lo + chunk], k_end, _TN_DIMS, preferred_element_type=F32)
                st_ref[c] = st * e_end + upd
        for c, (_, dn) in enumerate(chains):
            oacc_refs[dn][rows[c], vcols[c]] = o_intra[c] + jnp.concatenate(o_inter[c], axis=0)
            bc_ref[c] = bcum_next[c]

    st_ref[...] = jnp.zeros_like(st_ref)
    for c, x in enumerate(chunk_cumsums(gate_logits(tile_rows(0)))):
        bc_ref[c] = x

    def finish(t):
        rows = pl.ds(pl.multiple_of(t * tile, tile), tile)
        for h in range(n_heads):
            cols = slice(h * hv, (h + 1) * hv)
            o = of_ref[rows, cols] + ob_ref[rows, cols]
            ms = jnp.sum(o * o, axis=-1, keepdims=True) * inv_dv
            gate = gg_ref[rows, cols].astype(F32)
            o = o * lax.rsqrt(ms + LN_EPS) * ng_ref[h] * (gate * jax.nn.sigmoid(gate))
            o_ref[rows, cols] = o.astype(o_ref.dtype)

    def first_half(i, carry):
        step(i)
        return carry

    def second_half(i, carry):
        step(i)
        finish(i)
        finish(n_tiles - 1 - i)
        return carry

    if n_tiles % 2 == 0:
        lax.fori_loop(0, n_tiles // 2, first_half, 0)
        lax.fori_loop(n_tiles // 2, n_tiles, second_half, 0)
    else:
        lax.fori_loop(0, n_tiles, first_half, 0)
        lax.fori_loop(0, n_tiles, lambda t, carry: (finish(t), carry)[1], 0)


def _gla(proj, wdec_f, wdec_b, bdec_f, bdec_b, norm_g, layer, *, batch, seq, offs, hk, hv, q_scale, dv_head):
    heads = wdec_f.shape[1]
    hps = 2 if heads % 2 == 0 else 1
    tile = min(256, seq)
    kern = functools.partial(_gla_kernel, seq=seq, tile=tile, chunk=GLA_CHUNK, hk=hk, hv=hv, q_scale=q_scale,
                             inv_dv=1.0 / dv_head)
    wk, wv = hps * hk, hps * hv
    assert all(offs[n] % w == 0 for n, w in (("gq", wk), ("gk", wk), ("gv", wv), ("gg", wv)))
    cq, ck, cv, cg, cr = offs["gq"] // wk, offs["gk"] // wk, offs["gv"] // wv, offs["gg"] // wv, offs["r"] // LANES
    per_head = lambda rows, cols: pl.BlockSpec((None, hps, rows, cols), lambda b, g: (layer, g, 0, 0))
    return pl.pallas_call(
        kern,
        grid=(batch, heads // hps),
        in_specs=[
            pl.BlockSpec((seq, wk), lambda b, g: (b, cq + g)),
            pl.BlockSpec((seq, wk), lambda b, g: (b, ck + g)),
            pl.BlockSpec((seq, wv), lambda b, g: (b, cv + g)),
            pl.BlockSpec((seq, wv), lambda b, g: (b, cg + g)),
            pl.BlockSpec((seq, LANES), lambda b, g: (b, cr)),
            per_head(LANES, hk), per_head(LANES, hk),
            per_head(1, hk), per_head(1, hk), per_head(1, hv),
        ],
        out_specs=pl.BlockSpec((seq, wv), lambda b, g: (b, g)),
        out_shape=jax.ShapeDtypeStruct((batch * seq, heads * hv), BF16),
        scratch_shapes=[pltpu.VMEM((seq, wv), F32), pltpu.VMEM((seq, wv), F32),
                        pltpu.VMEM((2 * hps, hv, hk), F32), pltpu.VMEM((2 * hps, tile, hk), F32)],
        compiler_params=_params(2),
        name="gla",
    )(proj, proj, proj, proj, proj, wdec_f, wdec_b, bdec_f, bdec_b, norm_g)


def _dil_cfgs(seq):
    cfgs = []
    for w, d in DIL_BRANCHES:
        reach = (w // 2) // d
        length = seq // d
        qb = min(128, length)
        kb = min(length, qb + 2 * reach)
        n_qb = length // qb
        rel_starts = sorted({min(max(jb * qb - reach, 0), length - kb) - jb * qb for jb in range(n_qb)},
                            reverse=True)
        cfgs.append((d, reach, length, qb, kb, n_qb, rel_starts))
    return cfgs


def _dil_kernel(slopes_ref, q_ref, k_ref, v_ref, o_ref, *stats, seq, scale):
    n_br = len(DIL_BRANCHES)
    m_refs, l_refs, acc_refs, bias_refs = (stats[i * n_br:(i + 1) * n_br] for i in range(4))
    qkv16 = [stats[4 * n_br + 3 * n:4 * n_br + 3 * n + 3] for n in range(n_br)]
    qkv32 = [None, *[stats[7 * n_br + 3 * (n - 1):7 * n_br + 3 * n] for n in range(1, n_br - 1)]]
    slope = slopes_ref[pl.program_id(1)]
    hd = q_ref.shape[-1]
    dils = [d for _, d in DIL_BRANCHES]

    cfgs = _dil_cfgs(seq)
    for n, (d, reach, length, qb, kb, n_qb, rel_starts) in enumerate(cfgs):
        rel = lax.broadcasted_iota(jnp.int32, (qb, kb), 1) - lax.broadcasted_iota(jnp.int32, (qb, kb), 0)
        for vi, rs in enumerate(rel_starts):
            dist = jnp.abs(rel + rs)
            bias_refs[n][vi] = jnp.where(dist <= reach, -(slope * float(d)) * dist.astype(F32), MASK_VALUE)

    rt = min(256, seq)
    srcs = (q_ref, k_ref, v_ref)
    for a in range(3):
        for t0 in range(0, seq, rt):
            x = srcs[a][t0:t0 + rt, :]
            qkv16[0][a][t0:t0 + rt, :] = (x * scale if a == 0 else x).astype(BF16)
    for n in range(1, n_br):
        assert dils[n] % dils[n - 1] == 0
        step, l_prev, l_cur = dils[n] // dils[n - 1], seq // dils[n - 1], seq // dils[n]
        for a in range(3):
            src = srcs[a] if n == 1 else qkv32[n - 1][a]
            for bp in range(dils[n - 1]):
                for r2 in range(step):
                    for t0 in range(0, l_cur, rt):
                        rows = min(rt, l_cur - t0)
                        x = src[pl.ds(bp * l_prev + r2 + step * t0, rows, stride=step), :]
                        dst = (bp * step + r2) * l_cur + t0
                        if n < n_br - 1:
                            qkv32[n][a][dst:dst + rows, :] = x
                        qkv16[n][a][dst:dst + rows, :] = (x * scale if a == 0 else x).astype(BF16)

    def _residue(n, b):
        if n == 0:
            return 0
        step = dils[n] // dils[n - 1]
        return lax.rem(b, jnp.int32(step)) * dils[n - 1] + _residue(n - 1, lax.div(b, jnp.int32(step)))

    def blocks(jobs):
        place = []
        for n, idx in jobs:
            d, reach, length, qb, kb, n_qb, rel_starts = cfgs[n]
            if d == 1:
                blk, jb = 0, idx
            else:
                blk, jb = lax.div(idx, jnp.int32(n_qb)), lax.rem(idx, jnp.int32(n_qb))
            i0 = jb * qb
            ks = jnp.clip(i0 - reach, 0, length - kb)
            variant = sum(((ks - i0) <= rs).astype(jnp.int32) for rs in rel_starts[1:])
            base = blk * length
            qrows = pl.ds(pl.multiple_of(base + i0, qb), qb)
            krows = pl.ds(pl.multiple_of(base + ks, 16), kb)
            orows = qrows if d == 1 else pl.ds(_residue(n, blk) + d * i0, qb, stride=d)
            place.append((n, qrows, krows, orows, variant, qb, kb))
        s = []
        for n, qrows, krows, orows, variant, qb, kb in place:
            s.append(lax.dot_general(qkv16[n][0][qrows, :], qkv16[n][1][krows, :], _NT_DIMS,
                                     preferred_element_type=F32) + bias_refs[n][variant])
        m = [jnp.max(sj, axis=-1, keepdims=True) for sj in s]
        p = [jnp.exp(sj - mj).astype(BF16) for sj, mj in zip(s, m)]
        for (n, qrows, krows, orows, variant, qb, kb), mj, pj in zip(place, m, p):
            v_ones = jnp.concatenate([qkv16[n][2][krows, :], jnp.ones((kb, hd), BF16)], axis=1)
            acc_l = jnp.dot(pj, v_ones, preferred_element_type=F32)
            acc_refs[n][orows, :] = acc_l[:, :hd]
            l_refs[n][orows, :] = acc_l[:, hd:]
            m_refs[n][orows, :] = jnp.broadcast_to(mj, (qb, hd))

    counts = [seq // min(128, seq // d) for _, d in DIL_BRANCHES]
    group = 4
    if len(set(counts)) == 1 and counts[0] % group == 0:
        def body(i, carry):
            blocks([(n, group * i + u) for u in range(group) for n in range(n_br)])
            return carry
        lax.fori_loop(0, counts[0] // group, body, 0)
    else:
        for n in range(n_br):
            def body(idx, carry, n=n):
                blocks([(n, idx)])
                return carry
            lax.fori_loop(0, counts[n], body, 0)

    rt = min(256, seq)

    def finish(i, carry):
        rows = pl.ds(pl.multiple_of(i * rt, rt), rt)
        ms = [m_ref[rows, :] for m_ref in m_refs]
        m_all = functools.reduce(jnp.maximum, ms)
        num = jnp.zeros((rt, hd), F32)
        den = jnp.zeros((rt, hd), F32)
        for n in range(n_br):
            wgt = jnp.exp(ms[n] - m_all)
            num += wgt * acc_refs[n][rows, :]
            den += wgt * l_refs[n][rows, :]
        o_ref[rows, :] = (num / den).astype(o_ref.dtype)
        return carry

    lax.fori_loop(0, seq // rt, finish, 0)


def _dil(proj, slopes, *, batch, seq, offs, hd):
    cq, ck, cv = offs["dq"] // hd, offs["dk"] // hd, offs["dv"] // hd
    kern = functools.partial(_dil_kernel, seq=seq, scale=hd ** -0.5)
    return pl.pallas_call(
        kern,
        grid=(batch, DIL_HEADS),
        in_specs=[
            pl.BlockSpec(memory_space=pltpu.SMEM),
            pl.BlockSpec((seq, hd), lambda b, h: (b, cq + h)),
            pl.BlockSpec((seq, hd), lambda b, h: (b, ck + h)),
            pl.BlockSpec((seq, hd), lambda b, h: (b, cv + h)),
        ],
        out_specs=pl.BlockSpec((seq, hd), lambda b, h: (b, h)),
        out_shape=jax.ShapeDtypeStruct((batch * seq, DIL_HEADS * hd), BF16),
        scratch_shapes=([pltpu.VMEM((seq, hd), F32)] * (3 * len(DIL_BRANCHES))
                        + [pltpu.VMEM((len(c[6]), c[3], c[4]), F32) for c in _dil_cfgs(seq)]
                        + [pltpu.VMEM((seq, hd), BF16)] * (3 * len(DIL_BRANCHES))
                        + [pltpu.VMEM((seq, hd), F32)] * (3 * (len(DIL_BRANCHES) - 2))),
        compiler_params=_params(2),
        name="dil",
    )(slopes, proj, proj, proj)


def _conv_kernel(prev_ref, cur_ref, next_ref, w_ref, cb_ref, g_ref, b_ref, o_ref, u_ref, *, halo, taps):
    i = pl.program_id(1)
    c = o_ref.shape[-1]
    tq = o_ref.shape[0]

    def glu(ref):
        x = ref[...]
        return x[:, :c] * jax.nn.sigmoid(x[:, c:])

    ext = tq + 2 * halo
    u_ref[0, 0:halo, :] = jnp.where(i > 0, glu(prev_ref), 0.0)
    u_ref[0, halo:halo + tq, :] = glu(cur_ref)
    u_ref[0, halo + tq:, :] = jnp.where(i < pl.num_programs(1) - 1, glu(next_ref), 0.0)
    for s in range(1, SUBLANES):
        u_ref[s, 0:ext - SUBLANES, :] = u_ref[0, pl.ds(s, ext - SUBLANES), :]

    pad = taps // 2
    rc = min(32, tq)
    first = halo - pad
    span = (first + taps - 1) // SUBLANES * SUBLANES

    def chunk(j, carry):
        r0 = pl.multiple_of(j * rc, rc)
        acc = jnp.zeros((rc, c), F32) + cb_ref[...]
        for s in range(SUBLANES):
            win = u_ref[s, pl.ds(r0, rc + span), :]
            for t in range(taps):
                a, ts = divmod(first + t, SUBLANES)
                if ts == s:
                    acc = acc + win[a * SUBLANES:a * SUBLANES + rc] * w_ref[t:t + 1, :]
        y = _layer_norm(acc, g_ref[...], b_ref[...])
        o_ref[pl.ds(r0, rc), :] = (y * jax.nn.sigmoid(y)).astype(o_ref.dtype)
        return carry

    lax.fori_loop(0, tq // rc, chunk, 0, unroll=2)


def _conv(proj, conv_w, conv_b, ln_g, ln_b, layer, *, batch, seq, offs):
    taps, c = conv_w.shape[1:]
    halo = 16
    assert taps // 2 <= halo and halo % SUBLANES == 0
    tq = min(512, seq)
    nt = seq // tq
    hb = tq // halo
    ccol = offs["cv"] // (2 * c)
    kern = functools.partial(_conv_kernel, halo=halo, taps=taps)
    return pl.pallas_call(
        kern,
        grid=(batch, nt),
        in_specs=[
            pl.BlockSpec((halo, 2 * c), lambda b, i: (jnp.maximum((b * nt + i) * hb - 1, 0), ccol)),
            pl.BlockSpec((tq, 2 * c), lambda b, i: (b * nt + i, ccol)),
            pl.BlockSpec((halo, 2 * c),
                         lambda b, i: (jnp.minimum((b * nt + i + 1) * hb, batch * nt * hb - 1), ccol)),
            _layer_spec((taps, c), layer),
            _layer_spec((1, c), layer),
            _layer_spec((1, c), layer),
            _layer_spec((1, c), layer),
        ],
        out_specs=pl.BlockSpec((tq, c), lambda b, i: (b * nt + i, 0)),
        out_shape=jax.ShapeDtypeStruct((batch * seq, c), BF16),
        scratch_shapes=[pltpu.VMEM((SUBLANES, tq + 2 * halo, c), F32)],
        compiler_params=_params(2),
        name="conv",
    )(proj, proj, proj, conv_w, conv_b, ln_g, ln_b)


def _out_ln_kernel(x_ref, a_ref, d_ref, c_ref, w_ref, g_ref, b_ref, o_ref, *, alpha, heads, hdv):
    hv, nd = a_ref.shape[1] // heads, d_ref.shape[1]
    na = heads * hdv
    tm = x_ref.shape[0]
    sub = tm // 2 if tm % (2 * BF16_ROWS) == 0 else tm
    for r0 in range(0, tm, sub):
        rows = slice(r0, r0 + sub)
        mix = jnp.dot(d_ref[rows, :], w_ref[na:na + nd, :], preferred_element_type=F32)
        mix += jnp.dot(c_ref[rows, :], w_ref[na + nd:, :], preferred_element_type=F32)
        for h in range(heads):
            mix += jnp.dot(a_ref[rows, h * hv:(h + 1) * hv], w_ref[h * hdv:h * hdv + hv, :],
                           preferred_element_type=F32)
        o_ref[rows, :] = _layer_norm(alpha * x_ref[rows, :] + mix, g_ref[...], b_ref[...])


def _out_ln(x, gla_o, dil_o, conv_o, w, g, b, layer, *, alpha, tm, heads, hdv):
    m, d = x.shape
    row = lambda i: (i, 0)
    assert (heads - 1) * hdv + gla_o.shape[1] // heads <= w.shape[1] and hdv % BF16_ROWS == 0
    return pl.pallas_call(
        functools.partial(_out_ln_kernel, alpha=alpha, heads=heads, hdv=hdv),
        grid=(m // tm,),
        in_specs=[
            pl.BlockSpec((tm, d), row),
            pl.BlockSpec((tm, gla_o.shape[1]), row),
            pl.BlockSpec((tm, dil_o.shape[1]), row),
            pl.BlockSpec((tm, conv_o.shape[1]), row),
            _layer_spec(w.shape[1:], layer),
            _layer_spec((1, d), layer),
            _layer_spec((1, d), layer),
        ],
        out_specs=pl.BlockSpec((tm, d), row),
        out_shape=jax.ShapeDtypeStruct((m, d), F32),
        compiler_params=_params(1),
        name="out_ln",
    )(x, gla_o, dil_o, conv_o, w, g, b)


def _pad_heads(w, heads, width):
    lead = w.shape[:-1]
    hd = w.shape[-1] // heads
    w = w.reshape(lead + (heads, hd))
    w = jnp.pad(w, [(0, 0)] * len(lead) + [(0, 0), (0, width - hd)])
    return w.reshape(lead + (heads * width,))


def kernel(x, ffn1_w_gate, ffn1_w_up, ffn1_w_down, ln1_g, ln1_b, w_in, gla_decay_w_fwd, gla_decay_b_fwd, gla_decay_w_bwd, gla_decay_b_bwd, gla_norm_g, conv_w, conv_b, conv_ln_g, conv_ln_b, w_out, ln2_g, ln2_b, ffn2_w_gate, ffn2_w_up, ffn2_w_down, ln3_g, ln3_b):
    batch, seq, d_model = x.shape
    depth = ffn1_w_gate.shape[0]
    rank, gla_dk = gla_decay_w_fwd.shape[1:]
    gla_dv = gla_norm_g.shape[1]
    conv_c = conv_w.shape[2]
    dil_dim = w_out.shape[1] - gla_dv - conv_c
    dil_hd = dil_dim // DIL_HEADS
    hdk, hdv = gla_dk // GLA_HEADS, gla_dv // GLA_HEADS
    hk, hv = _round_up(hdk, LANES), _round_up(hdv, LANES)
    assert dil_hd == LANES and 2 * rank <= LANES and seq % GLA_CHUNK == 0

    alpha = (2.0 * depth) ** 0.25
    m = batch * seq
    tm = min(512, m)

    def layout(widths):
        offs, total = {}, 0
        for name, wdt in widths:
            offs[name] = total
            total += wdt
        return offs

    offs_a = layout((("gq", GLA_HEADS * hk), ("gk", GLA_HEADS * hk), ("gv", GLA_HEADS * hv),
                     ("gg", GLA_HEADS * hv), ("r", LANES)))
    offs_b = layout((("cv", 2 * conv_c), ("dq", dil_dim), ("dk", dil_dim), ("dv", dil_dim)))

    src = layout((("gq", gla_dk), ("gk", gla_dk), ("gv", gla_dv), ("r", 2 * rank), ("gg", gla_dv),
                  ("dil", 3 * dil_dim), ("cv", 2 * conv_c)))
    runs = []
    for name, hd in (("gq", hdk), ("gk", hdk), ("gv", hdv), ("gg", hdv)):
        for h in range(GLA_HEADS):
            runs += [(src[name] + h * hd + c0, min(LANES, hd - c0)) for c0 in range(0, _round_up(hd, LANES), LANES)]
    runs.append((src["r"], 2 * rank))
    assert len(runs) * LANES == offs_a["r"] + LANES
    runs_b = [(src[name] + c0, LANES) for name, wdt in (("cv", 2 * conv_c), ("dil", 3 * dil_dim))
              for c0 in range(0, wdt, LANES)]
    w_in_a, w_in_b = _regroup(w_in, (runs, runs_b), cb=min(MXU_WIDTH, d_model))

    def head_mats(wdec, row0):
        w4 = _pad_heads(wdec, GLA_HEADS, hk).reshape(depth, rank, GLA_HEADS, hk).transpose(0, 2, 1, 3)
        return jnp.pad(w4, ((0, 0), (0, 0), (row0, LANES - rank - row0), (0, 0))).astype(BF16)

    wdec_f, wdec_b = head_mats(gla_decay_w_fwd, 0), head_mats(gla_decay_w_bwd, rank)
    bdec_f = _pad_heads(gla_decay_b_fwd, GLA_HEADS, hk).reshape(depth, GLA_HEADS, 1, hk)
    bdec_b = _pad_heads(gla_decay_b_bwd, GLA_HEADS, hk).reshape(depth, GLA_HEADS, 1, hk)
    norm_g = _pad_heads(gla_norm_g, GLA_HEADS, hv).reshape(depth, GLA_HEADS, 1, hv)

    w_out_b = w_out.astype(BF16)

    ffn_src = ((ffn1_w_gate, ffn1_w_up, ffn1_w_down), (ffn2_w_gate, ffn2_w_up, ffn2_w_down))
    ffn_w = tuple(w[0].astype(BF16) for w in ffn_src[0])
    vec = lambda p: p[:, None, :]
    slopes = jnp.asarray(2.0 ** (-8.0 * np.arange(1, DIL_HEADS + 1) / DIL_HEADS), F32)

    xf = x.reshape(m, d_model)
    for l in range(depth):
        xf, ffn_w = _ffn_ln(xf, *ffn_w, vec(ln1_g), vec(ln1_b), l, (ffn_src[1], l), alpha=alpha, tm=tm)
        proj_a = _in_proj(xf, (w_in_a,), l, BF16, tm=tm, name="in_proj_gla")
        proj_b = _in_proj(xf, (w_in_b,), l, F32, tm=tm, name="in_proj_mix")
        gla_o = _gla(proj_a, wdec_f, wdec_b, bdec_f, bdec_b, norm_g, l, batch=batch, seq=seq, offs=offs_a,
                     hk=hk, hv=hv, q_scale=hdk ** -0.5, dv_head=hdv)
        dil_o = _dil(proj_b, slopes, batch=batch, seq=seq, offs=offs_b, hd=dil_hd)
        conv_o = _conv(proj_b, conv_w, vec(conv_b), vec(conv_ln_g), vec(conv_ln_b), l, batch=batch, seq=seq,
                       offs=offs_b)
        xf = _out_ln(xf, gla_o, dil_o, conv_o, w_out_b, vec(ln2_g), vec(ln2_b), l, alpha=alpha, tm=tm,
                     heads=GLA_HEADS, hdv=hdv)
        nxt = (ffn_src[0], l + 1) if l + 1 < depth else None
        xf, ffn_w = _ffn_ln(xf, *ffn_w, vec(ln3_g), vec(ln3_b), l, nxt, alpha=alpha, tm=tm)
    return xf.reshape(batch, seq, d_model)
```

```python
import functools

import numpy as np
import jax
import jax.numpy as jnp
from jax import lax
from jax.experimental import pallas as pl
from jax.experimental.pallas import tpu as pltpu

GLA_HEADS = 4
GLA_TAU = 16.0
GLA_CHUNK = 64
DIL_HEADS = 6
DIL_BRANCHES = ((128, 1), (512, 4), (2048, 16))
LN_EPS = 1e-5
MASK_VALUE = -1e30

LANES = 128
SUBLANES = 8
BF16_ROWS = 16
MXU_WIDTH = 256
VMEM_LIMIT_BYTES = 56 * 1024 * 1024

F32 = jnp.float32
BF16 = jnp.bfloat16
_NT_DIMS = (((1,), (1,)), ((), ()))
_TN_DIMS = (((0,), (0,)), ((), ()))


def _round_up(n, m):
    return (n + m - 1) // m * m


def _layer_norm(y, g, b, scale=1.0):
    mu = jnp.mean(y, axis=-1, keepdims=True)
    yc = y - mu
    var = jnp.mean(yc * yc, axis=-1, keepdims=True)
    return yc * lax.rsqrt(var + scale * scale * LN_EPS) * g + b


def _params(n_grid_axes):
    return pltpu.CompilerParams(dimension_semantics=("arbitrary",) * n_grid_axes,
                                vmem_limit_bytes=VMEM_LIMIT_BYTES)


def _layer_spec(shape, layer):
    zeros = (0,) * len(shape)
    return pl.BlockSpec((None,) + tuple(shape), lambda *_: (layer,) + zeros)


def _swiglu_part(xb, wg, wu, wd):
    gate = jnp.dot(xb, wg, preferred_element_type=F32)
    up = jnp.dot(xb, wu, preferred_element_type=F32)
    h = (gate * jax.nn.sigmoid(gate) * up).astype(BF16)
    return jnp.dot(h, wd, preferred_element_type=F32)


def _ffn_ln_kernel(x_ref, wg_ref, wu_ref, wd_ref, wgt_ref, wut_ref, wdt_ref, g_ref, b_ref, *rest, alpha, n_cast):
    src_refs, o_ref, dst_refs, xb_ref = rest[:n_cast], rest[n_cast], rest[n_cast + 1:-1], rest[-1]
    f = pl.program_id(1)

    @pl.when(f == 0)
    def _():
        x = x_ref[...]
        xb = x.astype(BF16)
        xb_ref[...] = xb
        o_ref[...] = _swiglu_part(xb, wgt_ref[...], wut_ref[...], wdt_ref[...]) + (2.0 * alpha) * x

    o_ref[...] += _swiglu_part(xb_ref[...], wg_ref[...], wu_ref[...], wd_ref[...])
    for src, dst in zip(src_refs, dst_refs):
        dst[...] = src[...].astype(BF16)

    @pl.when(f == pl.num_programs(1) - 1)
    def _():
        o_ref[...] = _layer_norm(o_ref[...], g_ref[...], b_ref[...], scale=2.0)


def _ffn_tiles(d_ff):
    for tf in (3 * MXU_WIDTH, 2 * MXU_WIDTH, MXU_WIDTH):
        tail = d_ff % tf
        if tail and tail % LANES == 0 and (d_ff - tail) % tail == 0:
            return tf, tail
    raise NotImplementedError(f"no FFN tiling for d_ff={d_ff}")


def _slab_rows(rows, n_steps):
    for r in range(BF16_ROWS, rows + 1, BF16_ROWS):
        if rows % r == 0 and rows // r <= n_steps:
            return r
    raise NotImplementedError(f"cannot spread {rows} rows over {n_steps} steps")


def _ffn_ln(x, wg, wu, wd, g, b, layer, cast_next=None, *, alpha, tm):
    m, d = x.shape
    d_ff = wg.shape[1]
    tf, tail = _ffn_tiles(d_ff)
    n_f = d_ff // tf
    tail_blk = (d_ff - tail) // tail
    in_specs = [
        pl.BlockSpec((tm, d), lambda i, f: (i, 0)),
        pl.BlockSpec((d, tf), lambda i, f: (0, f)),
        pl.BlockSpec((d, tf), lambda i, f: (0, f)),
        pl.BlockSpec((tf, d), lambda i, f: (f, 0)),
        pl.BlockSpec((d, tail), lambda i, f: (0, tail_blk)),
        pl.BlockSpec((d, tail), lambda i, f: (0, tail_blk)),
        pl.BlockSpec((tail, d), lambda i, f: (tail_blk, 0)),
        _layer_spec((1, d), layer),
        _layer_spec((1, d), layer),
    ]
    out_specs = [pl.BlockSpec((tm, d), lambda i, f: (i, 0))]
    out_shape = [jax.ShapeDtypeStruct((m, d), F32)]
    args = [x, wg, wu, wd, wg, wu, wd, g, b]
    if cast_next is not None:
        srcs, nxt = cast_next
        n_steps = (m // tm) * n_f
        for w in srcs:
            rows, cols = w.shape[1:]
            r = _slab_rows(rows, n_steps)
            slab = lambda i, f, last=rows // r - 1: jnp.minimum(i * n_f + f, last)
            in_specs.append(pl.BlockSpec((None, r, cols), lambda i, f, slab=slab: (nxt, slab(i, f), 0)))
            out_specs.append(pl.BlockSpec((r, cols), lambda i, f, slab=slab: (slab(i, f), 0)))
            out_shape.append(jax.ShapeDtypeStruct((rows, cols), BF16))
            args.append(w)
    outs = pl.pallas_call(
        functools.partial(_ffn_ln_kernel, alpha=alpha, n_cast=len(args) - 9),
        grid=(m // tm, n_f),
        in_specs=in_specs,
        out_specs=out_specs,
        out_shape=out_shape,
        scratch_shapes=[pltpu.VMEM((tm, d), BF16)],
        compiler_params=_params(2),
        name="ffn_ln",
    )(*args)
    return outs[0], tuple(outs[1:])


def _in_proj_kernel(x_ref, *refs):
    w_refs, o_ref = refs[:-1], refs[-1]
    xb = x_ref[...].astype(BF16)
    col = 0
    for w_ref in w_refs:
        n = w_ref.shape[1]
        o_ref[:, col:col + n] = jnp.dot(xb, w_ref[...], preferred_element_type=F32).astype(o_ref.dtype)
        col += n


def _in_proj(x, ws, layer, out_dtype, *, tm, name):
    m, d = x.shape
    n = sum(w.shape[2] for w in ws)
    return pl.pallas_call(
        _in_proj_kernel,
        grid=(m // tm,),
        in_specs=[pl.BlockSpec((tm, d), lambda i: (i, 0))] + [
            pl.BlockSpec((None, d, w.shape[2]), lambda i: (layer, 0, 0), pipeline_mode=pl.Buffered(1)) for w in ws],
        out_specs=pl.BlockSpec((tm, n), lambda i: (i, 0)),
        out_shape=jax.ShapeDtypeStruct((m, n), out_dtype),
        compiler_params=_params(1),
        name=name,
    )(x, *ws)


def _regroup_kernel(wt_ref, *o_refs, runs):
    cb = wt_ref.shape[1]
    eye = (lax.broadcasted_iota(jnp.int32, (cb, cb), 0) == lax.broadcasted_iota(jnp.int32, (cb, cb), 1)).astype(BF16)
    row = lax.broadcasted_iota(jnp.int32, (LANES, cb), 0)
    for o_ref, o_runs in zip(o_refs, runs):
        for blk, (start, width) in enumerate(o_runs):
            rows = wt_ref[start:start + LANES, :]
            if width < LANES:
                rows = jnp.where(row < width, rows, 0.0)
            o_ref[:, blk * LANES:(blk + 1) * LANES] = lax.dot_general(
                eye, rows.astype(BF16), _NT_DIMS, preferred_element_type=F32).astype(o_ref.dtype)


def _regroup(w, runs, *, cb):
    depth, d, n = w.shape
    assert all(s % SUBLANES == 0 and s + LANES <= n for o_runs in runs for s, _ in o_runs)
    return pl.pallas_call(
        functools.partial(_regroup_kernel, runs=tuple(tuple(r) for r in runs)),
        grid=(depth, d // cb),
        in_specs=[pl.BlockSpec((None, n, cb), lambda l, i: (l, 0, i))],
        out_specs=[pl.BlockSpec((None, cb, LANES * len(r)), lambda l, i: (l, i, 0)) for r in runs],
        out_shape=[jax.ShapeDtypeStruct((depth, d, LANES * len(r)), BF16) for r in runs],
        compiler_params=_params(2),
        name="regroup_w_in",
    )(jnp.swapaxes(w, 1, 2))


def _log_sigmoid(z):
    return jnp.minimum(z, 0.0) - jnp.log(1.0 + jnp.exp(-jnp.abs(z)))


def _gla_kernel(q_ref, k_ref, v_ref, gg_ref, r_ref, wf_ref, wb_ref, bf_ref, bb_ref, ng_ref,
                o_ref, of_ref, ob_ref, st_ref, bc_ref, *, seq, tile, chunk, hk, hv, q_scale, inv_dv):
    n_tiles = seq // tile
    n_chunks = tile // chunk
    n_heads = q_ref.shape[-1] // hk
    shift = chunk.bit_length() - 1
    row = lax.broadcasted_iota(jnp.int32, (tile, tile), 0)
    col = lax.broadcasted_iota(jnp.int32, (tile, tile), 1)
    same_chunk = lax.shift_right_logical(row, shift) == lax.shift_right_logical(col, shift)
    tris = (same_chunk & (col <= row), same_chunk & (col >= row))
    oacc_refs = (of_ref, ob_ref)
    w_refs, b_refs = (wf_ref, wb_ref), (bf_ref, bb_ref)
    chains = [(h, dn) for h in range(n_heads) for dn in (0, 1)]
    every = range(len(chains))
    kcols = [slice(h * hk, (h + 1) * hk) for h, _ in chains]
    vcols = [slice(h * hv, (h + 1) * hv) for h, _ in chains]

    def tile_rows(i):
        return [pl.ds(pl.multiple_of((i if dn == 0 else n_tiles - 1 - i) * tile, tile), tile) for _, dn in chains]

    def gate_logits(rows):
        return [jnp.dot(r_ref[rows[c], :], w_refs[dn][h], preferred_element_type=F32) + b_refs[dn][h]
                for c, (h, dn) in enumerate(chains)]

    def chunk_cumsums(z):
        la = [_log_sigmoid(zc) * (1.0 / GLA_TAU) for zc in z]
        la_hi = [x.astype(BF16) for x in la]
        la_lo = [(x - hi.astype(F32)).astype(BF16) for x, hi in zip(la, la_hi)]
        cum = [jnp.dot(tris[dn].astype(BF16), jnp.concatenate([la_hi[c], la_lo[c]], axis=1),
                       preferred_element_type=F32) for c, (_, dn) in enumerate(chains)]
        return [x[:, :hk] + x[:, hk:] for x in cum]

    def step(i):
        rows = tile_rows(i)
        bcum = [bc_ref[c] for c in every]
        z_next = gate_logits(tile_rows(jnp.minimum(i + 1, n_tiles - 1)))

        vb = [v_ref[rows[c], vcols[c]] for c in every]
        q_dec = [(q_ref[rows[c], kcols[c]].astype(F32) * q_scale * jnp.exp(bcum[c])).astype(BF16) for c in every]
        k_inv = [k_ref[rows[c], kcols[c]].astype(F32) * jnp.exp(-bcum[c]) for c in every]
        att = [lax.dot_general(q_dec[c], k_inv[c].astype(BF16), _NT_DIMS, preferred_element_type=F32) for c in every]
        bcum_next = chunk_cumsums(z_next)
        att = [jnp.where(tris[dn], att[c], 0.0).astype(BF16) for c, (_, dn) in enumerate(chains)]
        o_intra = [jnp.dot(att[c], vb[c], preferred_element_type=F32) for c in every]

        o_inter = [[None] * n_chunks for _ in every]
        for s in range(n_chunks):
            for c, (_, dn) in enumerate(chains):
                ck = s if dn == 0 else n_chunks - 1 - s
                lo = ck * chunk
                end_row = lo + chunk - 1 if dn == 0 else lo
                e_end = jnp.exp(bcum[c][end_row:end_row + 1, :])
                st = st_ref[c]
                o_inter[c][ck] = lax.dot_general(q_dec[c][lo:lo + chunk], st.astype(BF16), _NT_DIMS,
                                                 preferred_element_type=F32)
                k_end = (k_inv[c][lo:lo + chunk] * e_end).astype(BF16)
                upd = lax.dot_general(vb[c][lo:lo + chunk], k_end, _TN_DIMS, preferred_element_type=F32)
                st_ref[c] = st * e_end + upd
        for c, (_, dn) in enumerate(chains):
            oacc_refs[dn][rows[c], vcols[c]] = o_intra[c] + jnp.concatenate(o_inter[c], axis=0)
            bc_ref[c] = bcum_next[c]

    st_ref[...] = jnp.zeros_like(st_ref)
    for c, x in enumerate(chunk_cumsums(gate_logits(tile_rows(0)))):
        bc_ref[c] = x

    def finish(t):
        rows = pl.ds(pl.multiple_of(t * tile, tile), tile)
        for h in range(n_heads):
            cols = slice(h * hv, (h + 1) * hv)
            o = of_ref[rows, cols] + ob_ref[rows, cols]
            ms = jnp.sum(o * o, axis=-1, keepdims=True) * inv_dv
            gate = gg_ref[rows, cols].astype(F32)
            o = o * lax.rsqrt(ms + LN_EPS) * ng_ref[h] * (gate * jax.nn.sigmoid(gate))
            o_ref[rows, cols] = o.astype(o_ref.dtype)

    def first_half(i, carry):
        step(i)
        return carry

    def second_half(i, carry):
        step(i)
        finish(i)
        finish(n_tiles - 1 - i)
        return carry

    if n_tiles % 2 == 0:
        lax.fori_loop(0, n_tiles // 2, first_half, 0)
        lax.fori_loop(n_tiles // 2, n_tiles, second_half, 0)
    else:
        lax.fori_loop(0, n_tiles, first_half, 0)
        lax.fori_loop(0, n_tiles, lambda t, carry: (finish(t), carry)[1], 0)


def _gla(proj, wdec_f, wdec_b, bdec_f, bdec_b, norm_g, layer, *, batch, seq, offs, hk, hv, q_scale, dv_head):
    heads = wdec_f.shape[1]
    hps = 2 if heads % 2 == 0 else 1
    tile = min(256, seq)
    kern = functools.partial(_gla_kernel, seq=seq, tile=tile, chunk=GLA_CHUNK, hk=hk, hv=hv, q_scale=q_scale,
                             inv_dv=1.0 / dv_head)
    wk, wv = hps * hk, hps * hv
    assert all(offs[n] % w == 0 for n, w in (("gq", wk), ("gk", wk), ("gv", wv), ("gg", wv)))
    cq, ck, cv, cg, cr = offs["gq"] // wk, offs["gk"] // wk, offs["gv"] // wv, offs["gg"] // wv, offs["r"] // LANES
    per_head = lambda rows, cols: pl.BlockSpec((None, hps, rows, cols), lambda b, g: (layer, g, 0, 0))
    return pl.pallas_call(
        kern,
        grid=(batch, heads // hps),
        in_specs=[
            pl.BlockSpec((seq, wk), lambda b, g: (b, cq + g)),
            pl.BlockSpec((seq, wk), lambda b, g: (b, ck + g)),
            pl.BlockSpec((seq, wv), lambda b, g: (b, cv + g)),
            pl.BlockSpec((seq, wv), lambda b, g: (b, cg + g)),
            pl.BlockSpec((seq, LANES), lambda b, g: (b, cr)),
            per_head(LANES, hk), per_head(LANES, hk),
            per_head(1, hk), per_head(1, hk), per_head(1, hv),
        ],
        out_specs=pl.BlockSpec((seq, wv), lambda b, g: (b, g)),
        out_shape=jax.ShapeDtypeStruct((batch * seq, heads * hv), BF16),
        scratch_shapes=[pltpu.VMEM((seq, wv), F32), pltpu.VMEM((seq, wv), F32),
                        pltpu.VMEM((2 * hps, hv, hk), F32), pltpu.VMEM((2 * hps, tile, hk), F32)],
        compiler_params=_params(2),
        name="gla",
    )(proj, proj, proj, proj, proj, wdec_f, wdec_b, bdec_f, bdec_b, norm_g)


def _dil_cfgs(seq):
    cfgs = []
    for w, d in DIL_BRANCHES:
        reach = (w // 2) // d
        length = seq // d
        qb = min(128, length)
        kb = min(length, qb + 2 * reach)
        n_qb = length // qb
        rel_starts = sorted({min(max(jb * qb - reach, 0), length - kb) - jb * qb for jb in range(n_qb)},
                            reverse=True)
        cfgs.append((d, reach, length, qb, kb, n_qb, rel_starts))
    return cfgs


def _dil_kernel(slopes_ref, q_ref, k_ref, v_ref, o_ref, *stats, seq, scale):
    n_br = len(DIL_BRANCHES)
    m_refs, l_refs, acc_refs, bias_refs = (stats[i * n_br:(i + 1) * n_br] for i in range(4))
    qkv16 = [stats[4 * n_br + 3 * n:4 * n_br + 3 * n + 3] for n in range(n_br)]
    qkv32 = [None, *[stats[7 * n_br + 3 * (n - 1):7 * n_br + 3 * n] for n in range(1, n_br - 1)]]
    slope = slopes_ref[pl.program_id(1)]
    hd = q_ref.shape[-1]
    dils = [d for _, d in DIL_BRANCHES]

    cfgs = _dil_cfgs(seq)
    for n, (d, reach, length, qb, kb, n_qb, rel_starts) in enumerate(cfgs):
        rel = lax.broadcasted_iota(jnp.int32, (qb, kb), 1) - lax.broadcasted_iota(jnp.int32, (qb, kb), 0)
        for vi, rs in enumerate(rel_starts):
            dist = jnp.abs(rel + rs)
            bias_refs[n][vi] = jnp.where(dist <= reach, -(slope * float(d)) * dist.astype(F32), MASK_VALUE)

    rt = min(256, seq)
    srcs = (q_ref, k_ref, v_ref)
    for a in range(3):
        for t0 in range(0, seq, rt):
            x = srcs[a][t0:t0 + rt, :]
            qkv16[0][a][t0:t0 + rt, :] = (x * scale if a == 0 else x).astype(BF16)
    for n in range(1, n_br):
        assert dils[n] % dils[n - 1] == 0
        step, l_prev, l_cur = dils[n] // dils[n - 1], seq // dils[n - 1], seq // dils[n]
        for a in range(3):
            src = srcs[a] if n == 1 else qkv32[n - 1][a]
            for bp in range(dils[n - 1]):
                for r2 in range(step):
                    for t0 in range(0, l_cur, rt):
                        rows = min(rt, l_cur - t0)
                        x = src[pl.ds(bp * l_prev + r2 + step * t0, rows, stride=step), :]
                        dst = (bp * step + r2) * l_cur + t0
                        if n < n_br - 1:
                            qkv32[n][a][dst:dst + rows, :] = x
                        qkv16[n][a][dst:dst + rows, :] = (x * scale if a == 0 else x).astype(BF16)

    def _residue(n, b):
        if n == 0:
            return 0
        step = dils[n] // dils[n - 1]
        return lax.rem(b, jnp.int32(step)) * dils[n - 1] + _residue(n - 1, lax.div(b, jnp.int32(step)))

    def blocks(jobs):
        place = []
        for n, idx in jobs:
            d, reach, length, qb, kb, n_qb, rel_starts = cfgs[n]
            if d == 1:
                blk, jb = 0, idx
            else:
                blk, jb = lax.div(idx, jnp.int32(n_qb)), lax.rem(idx, jnp.int32(n_qb))
            i0 = jb * qb
            ks = jnp.clip(i0 - reach, 0, length - kb)
            variant = sum(((ks - i0) <= rs).astype(jnp.int32) for rs in rel_starts[1:])
            base = blk * length
            qrows = pl.ds(pl.multiple_of(base + i0, qb), qb)
            krows = pl.ds(pl.multiple_of(base + ks, 16), kb)
            orows = qrows if d == 1 else pl.ds(_residue(n, blk) + d * i0, qb, stride=d)
            place.append((n, qrows, krows, orows, variant, qb, kb))
        s = []
        for n, qrows, krows, orows, variant, qb, kb in place:
            s.append(lax.dot_general(qkv16[n][0][qrows, :], qkv16[n][1][krows, :], _NT_DIMS,
                                     preferred_element_type=F32) + bias_refs[n][variant])
        m = [jnp.max(sj, axis=-1, keepdims=True) for sj in s]
        p = [jnp.exp(sj - mj).astype(BF16) for sj, mj in zip(s, m)]
        for (n, qrows, krows, orows, variant, qb, kb), mj, pj in zip(place, m, p):
            v_ones = jnp.concatenate([qkv16[n][2][krows, :], jnp.ones((kb, hd), BF16)], axis=1)
            acc_l = jnp.dot(pj, v_ones, preferred_element_type=F32)
            acc_refs[n][orows, :] = acc_l[:, :hd]
            l_refs[n][orows, :] = acc_l[:, hd:]
            m_refs[n][orows, :] = jnp.broadcast_to(mj, (qb, hd))

    counts = [seq // min(128, seq // d) for _, d in DIL_BRANCHES]
    group = 4
    if len(set(counts)) == 1 and counts[0] % group == 0:
        def body(i, carry):
            blocks([(n, group * i + u) for u in range(group) for n in range(n_br)])
            return carry
        lax.fori_loop(0, counts[0] // group, body, 0)
    else:
        for n in range(n_br):
            def body(idx, carry, n=n):
                blocks([(n, idx)])
                return carry
            lax.fori_loop(0, counts[n], body, 0)

    rt = min(256, seq)

    def finish(i, carry):
        rows = pl.ds(pl.multiple_of(i * rt, rt), rt)
        ms = [m_ref[rows, :] for m_ref in m_refs]
        m_all = functools.reduce(jnp.maximum, ms)
        num = jnp.zeros((rt, hd), F32)
        den = jnp.zeros((rt, hd), F32)
        for n in range(n_br):
            wgt = jnp.exp(ms[n] - m_all)
            num += wgt * acc_refs[n][rows, :]
            den += wgt * l_refs[n][rows, :]
        o_ref[rows, :] = (num / den).astype(o_ref.dtype)
        return carry

    lax.fori_loop(0, seq // rt, finish, 0)


def _dil(proj, slopes, *, batch, seq, offs, hd):
    cq, ck, cv = offs["dq"] // hd, offs["dk"] // hd, offs["dv"] // hd
    kern = functools.partial(_dil_kernel, seq=seq, scale=hd ** -0.5)
    return pl.pallas_call(
        kern,
        grid=(batch, DIL_HEADS),
        in_specs=[
            pl.BlockSpec(memory_space=pltpu.SMEM),
            pl.BlockSpec((seq, hd), lambda b, h: (b, cq + h)),
            pl.BlockSpec((seq, hd), lambda b, h: (b, ck + h)),
            pl.BlockSpec((seq, hd), lambda b, h: (b, cv + h)),
        ],
        out_specs=pl.BlockSpec((seq, hd), lambda b, h: (b, h)),
        out_shape=jax.ShapeDtypeStruct((batch * seq, DIL_HEADS * hd), BF16),
        scratch_shapes=([pltpu.VMEM((seq, hd), F32)] * (3 * len(DIL_BRANCHES))
                        + [pltpu.VMEM((len(c[6]), c[3], c[4]), F32) for c in _dil_cfgs(seq)]
                        + [pltpu.VMEM((seq, hd), BF16)] * (3 * len(DIL_BRANCHES))
                        + [pltpu.VMEM((seq, hd), F32)] * (3 * (len(DIL_BRANCHES) - 2))),
        compiler_params=_params(2),
        name="dil",
    )(slopes, proj, proj, proj)


def _conv_kernel(prev_ref, cur_ref, next_ref, w_ref, cb_ref, g_ref, b_ref, o_ref, u_ref, *, halo, taps):
    i = pl.program_id(1)
    c = o_ref.shape[-1]
    tq = o_ref.shape[0]

    def glu(ref):
        x = ref[...]
        return x[:, :c] * jax.nn.sigmoid(x[:, c:])

    ext = tq + 2 * halo
    u_ref[0, 0:halo, :] = jnp.where(i > 0, glu(prev_ref), 0.0)
    u_ref[0, halo:halo + tq, :] = glu(cur_ref)
    u_ref[0, halo + tq:, :] = jnp.where(i < pl.num_programs(1) - 1, glu(next_ref), 0.0)
    for s in range(1, SUBLANES):
        u_ref[s, 0:ext - SUBLANES, :] = u_ref[0, pl.ds(s, ext - SUBLANES), :]

    pad = taps // 2
    rc = min(32, tq)
    first = halo - pad
    span = (first + taps - 1) // SUBLANES * SUBLANES

    def chunk(j, carry):
        r0 = pl.multiple_of(j * rc, rc)
        acc = jnp.zeros((rc, c), F32) + cb_ref[...]
        for s in range(SUBLANES):
            win = u_ref[s, pl.ds(r0, rc + span), :]
            for t in range(taps):
                a, ts = divmod(first + t, SUBLANES)
                if ts == s:
                    acc = acc + win[a * SUBLANES:a * SUBLANES + rc] * w_ref[t:t + 1, :]
        y = _layer_norm(acc, g_ref[...], b_ref[...])
        o_ref[pl.ds(r0, rc), :] = (y * jax.nn.sigmoid(y)).astype(o_ref.dtype)
        return carry

    lax.fori_loop(0, tq // rc, chunk, 0, unroll=2)


def _conv(proj, conv_w, conv_b, ln_g, ln_b, layer, *, batch, seq, offs):
    taps, c = conv_w.shape[1:]
    halo = 16
    assert taps // 2 <= halo and halo % SUBLANES == 0
    tq = min(512, seq)
    nt = seq // tq
    hb = tq // halo
    ccol = offs["cv"] // (2 * c)
    kern = functools.partial(_conv_kernel, halo=halo, taps=taps)
    return pl.pallas_call(
        kern,
        grid=(batch, nt),
        in_specs=[
            pl.BlockSpec((halo, 2 * c), lambda b, i: (jnp.maximum((b * nt + i) * hb - 1, 0), ccol)),
            pl.BlockSpec((tq, 2 * c), lambda b, i: (b * nt + i, ccol)),
            pl.BlockSpec((halo, 2 * c),
                         lambda b, i: (jnp.minimum((b * nt + i + 1) * hb, batch * nt * hb - 1), ccol)),
            _layer_spec((taps, c), layer),
            _layer_spec((1, c), layer),
            _layer_spec((1, c), layer),
            _layer_spec((1, c), layer),
        ],
        out_specs=pl.BlockSpec((tq, c), lambda b, i: (b * nt + i, 0)),
        out_shape=jax.ShapeDtypeStruct((batch * seq, c), BF16),
        scratch_shapes=[pltpu.VMEM((SUBLANES, tq + 2 * halo, c), F32)],
        compiler_params=_params(2),
        name="conv",
    )(proj, proj, proj, conv_w, conv_b, ln_g, ln_b)


def _out_ln_kernel(x_ref, a_ref, d_ref, c_ref, w_ref, g_ref, b_ref, o_ref, *, alpha, heads, hdv):
    hv, nd = a_ref.shape[1] // heads, d_ref.shape[1]
    na = heads * hdv
    tm = x_ref.shape[0]
    sub = tm // 2 if tm % (2 * BF16_ROWS) == 0 else tm
    for r0 in range(0, tm, sub):
        rows = slice(r0, r0 + sub)
        mix = jnp.dot(d_ref[rows, :], w_ref[na:na + nd, :], preferred_element_type=F32)
        mix += jnp.dot(c_ref[rows, :], w_ref[na + nd:, :], preferred_element_type=F32)
        for h in range(heads):
            mix += jnp.dot(a_ref[rows, h * hv:(h + 1) * hv], w_ref[h * hdv:h * hdv + hv, :],
                           preferred_element_type=F32)
        o_ref[rows, :] = _layer_norm(alpha * x_ref[rows, :] + mix, g_ref[...], b_ref[...])


def _out_ln(x, gla_o, dil_o, conv_o, w, g, b, layer, *, alpha, tm, heads, hdv):
    m, d = x.shape
    row = lambda i: (i, 0)
    assert (heads - 1) * hdv + gla_o.shape[1] // heads <= w.shape[1] and hdv % BF16_ROWS == 0
    return pl.pallas_call(
        functools.partial(_out_ln_kernel, alpha=alpha, heads=heads, hdv=hdv),
        grid=(m // tm,),
        in_specs=[
            pl.BlockSpec((tm, d), row),
            pl.BlockSpec((tm, gla_o.shape[1]), row),
            pl.BlockSpec((tm, dil_o.shape[1]), row),
            pl.BlockSpec((tm, conv_o.shape[1]), row),
            _layer_spec(w.shape[1:], layer),
            _layer_spec((1, d), layer),
            _layer_spec((1, d), layer),
        ],
        out_specs=pl.BlockSpec((tm, d), row),
        out_shape=jax.ShapeDtypeStruct((m, d), F32),
        compiler_params=_params(1),
        name="out_ln",
    )(x, gla_o, dil_o, conv_o, w, g, b)


def _pad_heads(w, heads, width):
    lead = w.shape[:-1]
    hd = w.shape[-1] // heads
    w = w.reshape(lead + (heads, hd))
    w = jnp.pad(w, [(0, 0)] * len(lead) + [(0, 0), (0, width - hd)])
    return w.reshape(lead + (heads * width,))


def kernel(x, ffn1_w_gate, ffn1_w_up, ffn1_w_down, ln1_g, ln1_b, w_in, gla_decay_w_fwd, gla_decay_b_fwd, gla_decay_w_bwd, gla_decay_b_bwd, gla_norm_g, conv_w, conv_b, conv_ln_g, conv_ln_b, w_out, ln2_g, ln2_b, ffn2_w_gate, ffn2_w_up, ffn2_w_down, ln3_g, ln3_b):
    batch, seq, d_model = x.shape
    depth = ffn1_w_gate.shape[0]
    rank, gla_dk = gla_decay_w_fwd.shape[1:]
    gla_dv = gla_norm_g.shape[1]
    conv_c = conv_w.shape[2]
    dil_dim = w_out.shape[1] - gla_dv - conv_c
    dil_hd = dil_dim // DIL_HEADS
    hdk, hdv = gla_dk // GLA_HEADS, gla_dv // GLA_HEADS
    hk, hv = _round_up(hdk, LANES), _round_up(hdv, LANES)
    assert dil_hd == LANES and 2 * rank <= LANES and seq % GLA_CHUNK == 0

    alpha = (2.0 * depth) ** 0.25
    m = batch * seq
    tm = min(512, m)

    def layout(widths):
        offs, total = {}, 0
        for name, wdt in widths:
            offs[name] = total
            total += wdt
        return offs

    offs_a = layout((("gq", GLA_HEADS * hk), ("gk", GLA_HEADS * hk), ("gv", GLA_HEADS * hv),
                     ("gg", GLA_HEADS * hv), ("r", LANES)))
    offs_b = layout((("cv", 2 * conv_c), ("dq", dil_dim), ("dk", dil_dim), ("dv", dil_dim)))

    src = layout((("gq", gla_dk), ("gk", gla_dk), ("gv", gla_dv), ("r", 2 * rank), ("gg", gla_dv),
                  ("dil", 3 * dil_dim), ("cv", 2 * conv_c)))
    runs = []
    for name, hd in (("gq", hdk), ("gk", hdk), ("gv", hdv), ("gg", hdv)):
        for h in range(GLA_HEADS):
            runs += [(src[name] + h * hd + c0, min(LANES, hd - c0)) for c0 in range(0, _round_up(hd, LANES), LANES)]
    runs.append((src["r"], 2 * rank))
    assert len(runs) * LANES == offs_a["r"] + LANES
    runs_b = [(src[name] + c0, LANES) for name, wdt in (("cv", 2 * conv_c), ("dil", 3 * dil_dim))
              for c0 in range(0, wdt, LANES)]
    w_in_a, w_in_b = _regroup(w_in, (runs, runs_b), cb=min(MXU_WIDTH, d_model))

    def head_mats(wdec, row0):
        w4 = _pad_heads(wdec, GLA_HEADS, hk).reshape(depth, rank, GLA_HEADS, hk).transpose(0, 2, 1, 3)
        return jnp.pad(w4, ((0, 0), (0, 0), (row0, LANES - rank - row0), (0, 0))).astype(BF16)

    wdec_f, wdec_b = head_mats(gla_decay_w_fwd, 0), head_mats(gla_decay_w_bwd, rank)
    bdec_f = _pad_heads(gla_decay_b_fwd, GLA_HEADS, hk).reshape(depth, GLA_HEADS, 1, hk)
    bdec_b = _pad_heads(gla_decay_b_bwd, GLA_HEADS, hk).reshape(depth, GLA_HEADS, 1, hk)
    norm_g = _pad_heads(gla_norm_g, GLA_HEADS, hv).reshape(depth, GLA_HEADS, 1, hv)

    w_out_b = w_out.astype(BF16)

    ffn_src = ((ffn1_w_gate, ffn1_w_up, ffn1_w_down), (ffn2_w_gate, ffn2_w_up, ffn2_w_down))
    ffn_w = tuple(w[0].astype(BF16) for w in ffn_src[0])
    vec = lambda p: p[:, None, :]
    slopes = jnp.asarray(2.0 ** (-8.0 * np.arange(1, DIL_HEADS + 1) / DIL_HEADS), F32)

    xf = x.reshape(m, d_model)
    for l in range(depth):
        xf, ffn_w = _ffn_ln(xf, *ffn_w, vec(ln1_g), vec(ln1_b), l, (ffn_src[1], l), alpha=alpha, tm=tm)
        proj_a = _in_proj(xf, (w_in_a,), l, BF16, tm=tm, name="in_proj_gla")
        proj_b = _in_proj(xf, (w_in_b,), l, F32, tm=tm, name="in_proj_mix")
        gla_o = _gla(proj_a, wdec_f, wdec_b, bdec_f, bdec_b, norm_g, l, batch=batch, seq=seq, offs=offs_a,
                     hk=hk, hv=hv, q_scale=hdk ** -0.5, dv_head=hdv)
        dil_o = _dil(proj_b, slopes, batch=batch, seq=seq, offs=offs_b, hd=dil_hd)
        conv_o = _conv(proj_b, conv_w, vec(conv_b), vec(conv_ln_g), vec(conv_ln_b), l, batch=batch, seq=seq,
                       offs=offs_b)
        xf = _out_ln(xf, gla_o, dil_o, conv_o, w_out_b, vec(ln2_g), vec(ln2_b), l, alpha=alpha, tm=tm,
                     heads=GLA_HEADS, hdv=hdv)
        nxt = (ffn_src[0], l + 1) if l + 1 < depth else None
        xf, ffn_w = _ffn_ln(xf, *ffn_w, vec(ln3_g), vec(ln3_b), l, nxt, alpha=alpha, tm=tm)
    return xf.reshape(batch, seq, d_model)
```

```python
import functools

import numpy as np
import jax
import jax.numpy as jnp
from jax import lax
from jax.experimental import pallas as pl
from jax.experimental.pallas import tpu as pltpu

GLA_HEADS = 4
GLA_TAU = 16.0
GLA_CHUNK = 64
DIL_HEADS = 6
DIL_BRANCHES = ((128, 1), (512, 4), (2048, 16))
LN_EPS = 1e-5
MASK_VALUE = -1e30

LANES = 128
SUBLANES = 8
BF16_ROWS = 16
MXU_WIDTH = 256
VMEM_LIMIT_BYTES = 56 * 1024 * 1024

F32 = jnp.float32
BF16 = jnp.bfloat16
_NT_DIMS = (((1,), (1,)), ((), ()))
_TN_DIMS = (((0,), (0,)), ((), ()))


def _round_up(n, m):
    return (n + m - 1) // m * m


def _layer_norm(y, g, b, scale=1.0):
    mu = jnp.mean(y, axis=-1, keepdims=True)
    yc = y - mu
    var = jnp.mean(yc * yc, axis=-1, keepdims=True)
    return yc * lax.rsqrt(var + scale * scale * LN_EPS) * g + b


def _params(n_grid_axes):
    return pltpu.CompilerParams(dimension_semantics=("arbitrary",) * n_grid_axes,
                                vmem_limit_bytes=VMEM_LIMIT_BYTES)


def _layer_spec(shape, layer):
    zeros = (0,) * len(shape)
    return pl.BlockSpec((None,) + tuple(shape), lambda *_: (layer,) + zeros)


def _swiglu_part(xb, wg, wu, wd):
    n = wg.shape[1]
    step = MXU_WIDTH if n % MXU_WIDTH == 0 else n
    out = None
    for c0 in range(0, n, step):
        gate = jnp.dot(xb, wg[:, c0:c0 + step], preferred_element_type=F32)
        up = jnp.dot(xb, wu[:, c0:c0 + step], preferred_element_type=F32)
        h = (gate * jax.nn.sigmoid(gate) * up).astype(BF16)
        part = jnp.dot(h, wd[c0:c0 + step, :], preferred_element_type=F32)
        out = part if out is None else out + part
    return out


def _ffn_ln_kernel(x_ref, wg_ref, wu_ref, wd_ref, wgt_ref, wut_ref, wdt_ref, g_ref, b_ref, *rest, alpha, n_cast):
    src_refs, o_ref, dst_refs, xb_ref = rest[:n_cast], rest[n_cast], rest[n_cast + 1:-1], rest[-1]
    f = pl.program_id(1)

    @pl.when(f == 0)
    def _():
        x = x_ref[...]
        xb = x.astype(BF16)
        xb_ref[...] = xb
        o_ref[...] = _swiglu_part(xb, wgt_ref[...], wut_ref[...], wdt_ref[...]) + (2.0 * alpha) * x

    o_ref[...] += _swiglu_part(xb_ref[...], wg_ref[...], wu_ref[...], wd_ref[...])
    for src, dst in zip(src_refs, dst_refs):
        dst[...] = src[...].astype(BF16)

    @pl.when(f == pl.num_programs(1) - 1)
    def _():
        o_ref[...] = _layer_norm(o_ref[...], g_ref[...], b_ref[...], scale=2.0)


def _ffn_tiles(d_ff):
    for tf in (3 * MXU_WIDTH, 2 * MXU_WIDTH, MXU_WIDTH):
        tail = d_ff % tf
        if tail and tail % LANES == 0 and (d_ff - tail) % tail == 0:
            return tf, tail
    raise NotImplementedError(f"no FFN tiling for d_ff={d_ff}")


def _slab_rows(rows, n_steps):
    for r in range(BF16_ROWS, rows + 1, BF16_ROWS):
        if rows % r == 0 and rows // r <= n_steps:
            return r
    raise NotImplementedError(f"cannot spread {rows} rows over {n_steps} steps")


def _ffn_ln(x, wg, wu, wd, g, b, layer, cast_next=None, *, alpha, tm):
    m, d = x.shape
    d_ff = wg.shape[1]
    tf, tail = _ffn_tiles(d_ff)
    n_f = d_ff // tf
    tail_blk = (d_ff - tail) // tail
    in_specs = [
        pl.BlockSpec((tm, d), lambda i, f: (i, 0)),
        pl.BlockSpec((d, tf), lambda i, f: (0, f)),
        pl.BlockSpec((d, tf), lambda i, f: (0, f)),
        pl.BlockSpec((tf, d), lambda i, f: (f, 0)),
        pl.BlockSpec((d, tail), lambda i, f: (0, tail_blk)),
        pl.BlockSpec((d, tail), lambda i, f: (0, tail_blk)),
        pl.BlockSpec((tail, d), lambda i, f: (tail_blk, 0)),
        _layer_spec((1, d), layer),
        _layer_spec((1, d), layer),
    ]
    out_specs = [pl.BlockSpec((tm, d), lambda i, f: (i, 0))]
    out_shape = [jax.ShapeDtypeStruct((m, d), F32)]
    args = [x, wg, wu, wd, wg, wu, wd, g, b]
    if cast_next is not None:
        srcs, nxt = cast_next
        n_steps = (m // tm) * n_f
        for w in srcs:
            rows, cols = w.shape[1:]
            r = _slab_rows(rows, n_steps)
            slab = lambda i, f, last=rows // r - 1: jnp.minimum(i * n_f + f, last)
            in_specs.append(pl.BlockSpec((None, r, cols), lambda i, f, slab=slab: (nxt, slab(i, f), 0)))
            out_specs.append(pl.BlockSpec((r, cols), lambda i, f, slab=slab: (slab(i, f), 0)))
            out_shape.append(jax.ShapeDtypeStruct((rows, cols), BF16))
            args.append(w)
    outs = pl.pallas_call(
        functools.partial(_ffn_ln_kernel, alpha=alpha, n_cast=len(args) - 9),
        grid=(m // tm, n_f),
        in_specs=in_specs,
        out_specs=out_specs,
        out_shape=out_shape,
        scratch_shapes=[pltpu.VMEM((tm, d), BF16)],
        compiler_params=_params(2),
        name="ffn_ln",
    )(*args)
    return outs[0], tuple(outs[1:])


def _in_proj_kernel(x_ref, *refs):
    w_refs, o_ref = refs[:-1], refs[-1]
    xb = x_ref[...].astype(BF16)
    col = 0
    for w_ref in w_refs:
        n = w_ref.shape[1]
        o_ref[:, col:col + n] = jnp.dot(xb, w_ref[...], preferred_element_type=F32).astype(o_ref.dtype)
        col += n


def _in_proj(x, ws, layer, out_dtype, *, tm, name):
    m, d = x.shape
    n = sum(w.shape[2] for w in ws)
    return pl.pallas_call(
        _in_proj_kernel,
        grid=(m // tm,),
        in_specs=[pl.BlockSpec((tm, d), lambda i: (i, 0))] + [
            pl.BlockSpec((None, d, w.shape[2]), lambda i: (layer, 0, 0), pipeline_mode=pl.Buffered(1)) for w in ws],
        out_specs=pl.BlockSpec((tm, n), lambda i: (i, 0)),
        out_shape=jax.ShapeDtypeStruct((m, n), out_dtype),
        compiler_params=_params(1),
        name=name,
    )(x, *ws)


def _regroup_kernel(wt_ref, *o_refs, runs):
    cb = wt_ref.shape[1]
    eye = (lax.broadcasted_iota(jnp.int32, (cb, cb), 0) == lax.broadcasted_iota(jnp.int32, (cb, cb), 1)).astype(BF16)
    row = lax.broadcasted_iota(jnp.int32, (LANES, cb), 0)
    for o_ref, o_runs in zip(o_refs, runs):
        for blk, (start, width) in enumerate(o_runs):
            rows = wt_ref[start:start + LANES, :]
            if width < LANES:
                rows = jnp.where(row < width, rows, 0.0)
            o_ref[:, blk * LANES:(blk + 1) * LANES] = lax.dot_general(
                eye, rows.astype(BF16), _NT_DIMS, preferred_element_type=F32).astype(o_ref.dtype)


def _regroup(w, runs, *, cb):
    depth, d, n = w.shape
    assert all(s % SUBLANES == 0 and s + LANES <= n for o_runs in runs for s, _ in o_runs)
    return pl.pallas_call(
        functools.partial(_regroup_kernel, runs=tuple(tuple(r) for r in runs)),
        grid=(depth, d // cb),
        in_specs=[pl.BlockSpec((None, n, cb), lambda l, i: (l, 0, i))],
        out_specs=[pl.BlockSpec((None, cb, LANES * len(r)), lambda l, i: (l, i, 0)) for r in runs],
        out_shape=[jax.ShapeDtypeStruct((depth, d, LANES * len(r)), BF16) for r in runs],
        compiler_params=_params(2),
        name="regroup_w_in",
    )(jnp.swapaxes(w, 1, 2))


def _log_sigmoid(z):
    return jnp.minimum(z, 0.0) - jnp.log(1.0 + jnp.exp(-jnp.abs(z)))


def _gla_kernel(q_ref, k_ref, v_ref, gg_ref, r_ref, wf_ref, wb_ref, bf_ref, bb_ref, ng_ref,
                o_ref, of_ref, ob_ref, st_ref, bc_ref, *, seq, tile, chunk, hk, hv, q_scale, inv_dv):
    n_tiles = seq // tile
    n_chunks = tile // chunk
    n_heads = q_ref.shape[-1] // hk
    shift = chunk.bit_length() - 1
    row = lax.broadcasted_iota(jnp.int32, (tile, tile), 0)
    col = lax.broadcasted_iota(jnp.int32, (tile, tile), 1)
    same_chunk = lax.shift_right_logical(row, shift) == lax.shift_right_logical(col, shift)
    tris = (same_chunk & (col <= row), same_chunk & (col >= row))
    oacc_refs = (of_ref, ob_ref)
    w_refs, b_refs = (wf_ref, wb_ref), (bf_ref, bb_ref)
    chains = [(h, dn) for h in range(n_heads) for dn in (0, 1)]
    every = range(len(chains))
    kcols = [slice(h * hk, (h + 1) * hk) for h, _ in chains]
    vcols = [slice(h * hv, (h + 1) * hv) for h, _ in chains]

    def tile_rows(i):
        return [pl.ds(pl.multiple_of((i if dn == 0 else n_tiles - 1 - i) * tile, tile), tile) for _, dn in chains]

    def gate_logits(rows):
        return [jnp.dot(r_ref[rows[c], :], w_refs[dn][h], preferred_element_type=F32) + b_refs[dn][h]
                for c, (h, dn) in enumerate(chains)]

    def chunk_cumsums(z):
        la = [_log_sigmoid(zc) * (1.0 / GLA_TAU) for zc in z]
        la_hi = [x.astype(BF16) for x in la]
        la_lo = [(x - hi.astype(F32)).astype(BF16) for x, hi in zip(la, la_hi)]
        cum = [jnp.dot(tris[dn].astype(BF16), jnp.concatenate([la_hi[c], la_lo[c]], axis=1),
                       preferred_element_type=F32) for c, (_, dn) in enumerate(chains)]
        return [x[:, :hk] + x[:, hk:] for x in cum]

    def step(i):
        rows = tile_rows(i)
        bcum = [bc_ref[c] for c in every]
        z_next = gate_logits(tile_rows(jnp.minimum(i + 1, n_tiles - 1)))

        vb = [v_ref[rows[c], vcols[c]] for c in every]
        q_dec = [(q_ref[rows[c], kcols[c]].astype(F32) * q_scale * jnp.exp(bcum[c])).astype(BF16) for c in every]
        k_inv = [k_ref[rows[c], kcols[c]].astype(F32) * jnp.exp(-bcum[c]) for c in every]
        att = [lax.dot_general(q_dec[c], k_inv[c].astype(BF16), _NT_DIMS, preferred_element_type=F32) for c in every]
        bcum_next = chunk_cumsums(z_next)
        att = [jnp.where(tris[dn], att[c], 0.0).astype(BF16) for c, (_, dn) in enumerate(chains)]
        o_intra = [jnp.dot(att[c], vb[c], preferred_element_type=F32) for c in every]

        o_inter = [[None] * n_chunks for _ in every]
        for s in range(n_chunks):
            for c, (_, dn) in enumerate(chains):
                ck = s if dn == 0 else n_chunks - 1 - s
                lo = ck * chunk
                end_row = lo + chunk - 1 if dn == 0 else lo
                e_end = jnp.exp(bcum[c][end_row:end_row + 1, :])
                st = st_ref[c]
                o_inter[c][ck] = lax.dot_general(q_dec[c][lo:lo + chunk], st.astype(BF16), _NT_DIMS,
                                                 preferred_element_type=F32)
                k_end = (k_inv[c][lo:lo + chunk] * e_end).astype(BF16)
                upd = lax.dot_general(vb[c][lo:lo + chunk], k_end, _TN_DIMS, preferred_element_type=F32)
                st_ref[c] = st * e_end + upd
        for c, (_, dn) in enumerate(chains):
            oacc_refs[dn][rows[c], vcols[c]] = o_intra[c] + jnp.concatenate(o_inter[c], axis=0)
            bc_ref[c] = bcum_next[c]

    st_ref[...] = jnp.zeros_like(st_ref)
    for c, x in enumerate(chunk_cumsums(gate_logits(tile_rows(0)))):
        bc_ref[c] = x

    def finish(t):
        rows = pl.ds(pl.multiple_of(t * tile, tile), tile)
        for h in range(n_heads):
            cols = slice(h * hv, (h + 1) * hv)
            o = of_ref[rows, cols] + ob_ref[rows, cols]
            ms = jnp.sum(o * o, axis=-1, keepdims=True) * inv_dv
            gate = gg_ref[rows, cols].astype(F32)
            o = o * lax.rsqrt(ms + LN_EPS) * ng_ref[h] * (gate * jax.nn.sigmoid(gate))
            o_ref[rows, cols] = o.astype(o_ref.dtype)

    def first_half(i, carry):
        step(i)
        return carry

    def second_half(i, carry):
        step(i)
        finish(i)
        finish(n_tiles - 1 - i)
        return carry

    if n_tiles % 2 == 0:
        lax.fori_loop(0, n_tiles // 2, first_half, 0)
        lax.fori_loop(n_tiles // 2, n_tiles, second_half, 0)
    else:
        lax.fori_loop(0, n_tiles, first_half, 0)
        lax.fori_loop(0, n_tiles, lambda t, carry: (finish(t), carry)[1], 0)


def _gla(proj, wdec_f, wdec_b, bdec_f, bdec_b, norm_g, layer, *, batch, seq, offs, hk, hv, q_scale, dv_head):
    heads = wdec_f.shape[1]
    hps = 2 if heads % 2 == 0 else 1
    tile = min(256, seq)
    kern = functools.partial(_gla_kernel, seq=seq, tile=tile, chunk=GLA_CHUNK, hk=hk, hv=hv, q_scale=q_scale,
                             inv_dv=1.0 / dv_head)
    wk, wv = hps * hk, hps * hv
    assert all(offs[n] % w == 0 for n, w in (("gq", wk), ("gk", wk), ("gv", wv), ("gg", wv)))
    cq, ck, cv, cg, cr = offs["gq"] // wk, offs["gk"] // wk, offs["gv"] // wv, offs["gg"] // wv, offs["r"] // LANES
    per_head = lambda rows, cols: pl.BlockSpec((None, hps, rows, cols), lambda b, g: (layer, g, 0, 0))
    return pl.pallas_call(
        kern,
        grid=(batch, heads // hps),
        in_specs=[
            pl.BlockSpec((seq, wk), lambda b, g: (b, cq + g)),
            pl.BlockSpec((seq, wk), lambda b, g: (b, ck + g)),
            pl.BlockSpec((seq, wv), lambda b, g: (b, cv + g)),
            pl.BlockSpec((seq, wv), lambda b, g: (b, cg + g)),
            pl.BlockSpec((seq, LANES), lambda b, g: (b, cr)),
            per_head(LANES, hk), per_head(LANES, hk),
            per_head(1, hk), per_head(1, hk), per_head(1, hv),
        ],
        out_specs=pl.BlockSpec((seq, wv), lambda b, g: (b, g)),
        out_shape=jax.ShapeDtypeStruct((batch * seq, heads * hv), BF16),
        scratch_shapes=[pltpu.VMEM((seq, wv), F32), pltpu.VMEM((seq, wv), F32),
                        pltpu.VMEM((2 * hps, hv, hk), F32), pltpu.VMEM((2 * hps, tile, hk), F32)],
        compiler_params=_params(2),
        name="gla",
    )(proj, proj, proj, proj, proj, wdec_f, wdec_b, bdec_f, bdec_b, norm_g)


def _dil_cfgs(seq):
    cfgs = []
    for w, d in DIL_BRANCHES:
        reach = (w // 2) // d
        length = seq // d
        qb = min(128, length)
        kb = min(length, qb + 2 * reach)
        n_qb = length // qb
        rel_starts = sorted({min(max(jb * qb - reach, 0), length - kb) - jb * qb for jb in range(n_qb)},
                            reverse=True)
        cfgs.append((d, reach, length, qb, kb, n_qb, rel_starts))
    return cfgs


def _dil_kernel(slopes_ref, q_ref, k_ref, v_ref, o_ref, *stats, seq, scale):
    n_br = len(DIL_BRANCHES)
    m_refs, l_refs, acc_refs, bias_refs = (stats[i * n_br:(i + 1) * n_br] for i in range(4))
    qkv16 = [stats[4 * n_br + 3 * n:4 * n_br + 3 * n + 3] for n in range(n_br)]
    qkv32 = [None, *[stats[7 * n_br + 3 * (n - 1):7 * n_br + 3 * n] for n in range(1, n_br - 1)]]
    slope = slopes_ref[pl.program_id(1)]
    hd = q_ref.shape[-1]
    dils = [d for _, d in DIL_BRANCHES]

    cfgs = _dil_cfgs(seq)
    for n, (d, reach, length, qb, kb, n_qb, rel_starts) in enumerate(cfgs):
        rel = lax.broadcasted_iota(jnp.int32, (qb, kb), 1) - lax.broadcasted_iota(jnp.int32, (qb, kb), 0)
        for vi, rs in enumerate(rel_starts):
            dist = jnp.abs(rel + rs)
            bias_refs[n][vi] = jnp.where(dist <= reach, -(slope * float(d)) * dist.astype(F32), MASK_VALUE)

    rt = min(256, seq)
    srcs = (q_ref, k_ref, v_ref)
    for a in range(3):
        for t0 in range(0, seq, rt):
            x = srcs[a][t0:t0 + rt, :]
            qkv16[0][a][t0:t0 + rt, :] = (x * scale if a == 0 else x).astype(BF16)
    for n in range(1, n_br):
        assert dils[n] % dils[n - 1] == 0
        step, l_prev, l_cur = dils[n] // dils[n - 1], seq // dils[n - 1], seq // dils[n]
        for a in range(3):
            src = srcs[a] if n == 1 else qkv32[n - 1][a]
            for bp in range(dils[n - 1]):
                for r2 in range(step):
                    for t0 in range(0, l_cur, rt):
                        rows = min(rt, l_cur - t0)
                        x = src[pl.ds(bp * l_prev + r2 + step * t0, rows, stride=step), :]
                        dst = (bp * step + r2) * l_cur + t0
                        if n < n_br - 1:
                            qkv32[n][a][dst:dst + rows, :] = x
                        qkv16[n][a][dst:dst + rows, :] = (x * scale if a == 0 else x).astype(BF16)

    def _residue(n, b):
        if n == 0:
            return 0
        step = dils[n] // dils[n - 1]
        return lax.rem(b, jnp.int32(step)) * dils[n - 1] + _residue(n - 1, lax.div(b, jnp.int32(step)))

    def blocks(jobs):
        place = []
        for n, idx in jobs:
            d, reach, length, qb, kb, n_qb, rel_starts = cfgs[n]
            if d == 1:
                blk, jb = 0, idx
            else:
                blk, jb = lax.div(idx, jnp.int32(n_qb)), lax.rem(idx, jnp.int32(n_qb))
            i0 = jb * qb
            ks = jnp.clip(i0 - reach, 0, length - kb)
            variant = sum(((ks - i0) <= rs).astype(jnp.int32) for rs in rel_starts[1:])
            base = blk * length
            qrows = pl.ds(pl.multiple_of(base + i0, qb), qb)
            krows = pl.ds(pl.multiple_of(base + ks, 16), kb)
            orows = qrows if d == 1 else pl.ds(_residue(n, blk) + d * i0, qb, stride=d)
            place.append((n, qrows, krows, orows, variant, qb, kb))
        s = []
        for n, qrows, krows, orows, variant, qb, kb in place:
            s.append(lax.dot_general(qkv16[n][0][qrows, :], qkv16[n][1][krows, :], _NT_DIMS,
                                     preferred_element_type=F32) + bias_refs[n][variant])
        m = [jnp.max(sj, axis=-1, keepdims=True) for sj in s]
        p = [jnp.exp(sj - mj).astype(BF16) for sj, mj in zip(s, m)]
        for (n, qrows, krows, orows, variant, qb, kb), mj, pj in zip(place, m, p):
            v_ones = jnp.concatenate([qkv16[n][2][krows, :], jnp.ones((kb, hd), BF16)], axis=1)
            acc_l = jnp.dot(pj, v_ones, preferred_element_type=F32)
            acc_refs[n][orows, :] = acc_l[:, :hd]
            l_refs[n][orows, :] = acc_l[:, hd:]
            m_refs[n][orows, :] = jnp.broadcast_to(mj, (qb, hd))

    counts = [seq // min(128, seq // d) for _, d in DIL_BRANCHES]
    group = 4
    if len(set(counts)) == 1 and counts[0] % group == 0:
        def body(i, carry):
            blocks([(n, group * i + u) for u in range(group) for n in range(n_br)])
            return carry
        lax.fori_loop(0, counts[0] // group, body, 0)
    else:
        for n in range(n_br):
            def body(idx, carry, n=n):
                blocks([(n, idx)])
                return carry
            lax.fori_loop(0, counts[n], body, 0)

    rt = min(256, seq)

    def finish(i, carry):
        rows = pl.ds(pl.multiple_of(i * rt, rt), rt)
        ms = [m_ref[rows, :] for m_ref in m_refs]
        m_all = functools.reduce(jnp.maximum, ms)
        num = jnp.zeros((rt, hd), F32)
        den = jnp.zeros((rt, hd), F32)
        for n in range(n_br):
            wgt = jnp.exp(ms[n] - m_all)
            num += wgt * acc_refs[n][rows, :]
            den += wgt * l_refs[n][rows, :]
        o_ref[rows, :] = (num / den).astype(o_ref.dtype)
        return carry

    lax.fori_loop(0, seq // rt, finish, 0)


def _dil(proj, slopes, *, batch, seq, offs, hd):
    cq, ck, cv = offs["dq"] // hd, offs["dk"] // hd, offs["dv"] // hd
    kern = functools.partial(_dil_kernel, seq=seq, scale=hd ** -0.5)
    return pl.pallas_call(
        kern,
        grid=(batch, DIL_HEADS),
        in_specs=[
            pl.BlockSpec(memory_space=pltpu.SMEM),
            pl.BlockSpec((seq, hd), lambda b, h: (b, cq + h)),
            pl.BlockSpec((seq, hd), lambda b, h: (b, ck + h)),
            pl.BlockSpec((seq, hd), lambda b, h: (b, cv + h)),
        ],
        out_specs=pl.BlockSpec((seq, hd), lambda b, h: (b, h)),
        out_shape=jax.ShapeDtypeStruct((batch * seq, DIL_HEADS * hd), BF16),
        scratch_shapes=([pltpu.VMEM((seq, hd), F32)] * (3 * len(DIL_BRANCHES))
                        + [pltpu.VMEM((len(c[6]), c[3], c[4]), F32) for c in _dil_cfgs(seq)]
                        + [pltpu.VMEM((seq, hd), BF16)] * (3 * len(DIL_BRANCHES))
                        + [pltpu.VMEM((seq, hd), F32)] * (3 * (len(DIL_BRANCHES) - 2))),
        compiler_params=_params(2),
        name="dil",
    )(slopes, proj, proj, proj)


def _conv_kernel(prev_ref, cur_ref, next_ref, w_ref, cb_ref, g_ref, b_ref, o_ref, u_ref, *, halo, taps):
    i = pl.program_id(1)
    c = o_ref.shape[-1]
    tq = o_ref.shape[0]

    def glu(ref):
        x = ref[...]
        return x[:, :c] * jax.nn.sigmoid(x[:, c:])

    ext = tq + 2 * halo
    u_ref[0, 0:halo, :] = jnp.where(i > 0, glu(prev_ref), 0.0)
    u_ref[0, halo:halo + tq, :] = glu(cur_ref)
    u_ref[0, halo + tq:, :] = jnp.where(i < pl.num_programs(1) - 1, glu(next_ref), 0.0)
    for s in range(1, SUBLANES):
        u_ref[s, 0:ext - SUBLANES, :] = u_ref[0, pl.ds(s, ext - SUBLANES), :]

    pad = taps // 2
    rc = min(32, tq)
    first = halo - pad
    span = (first + taps - 1) // SUBLANES * SUBLANES

    def chunk(j, carry):
        r0 = pl.multiple_of(j * rc, rc)
        acc = jnp.zeros((rc, c), F32) + cb_ref[...]
        for s in range(SUBLANES):
            win = u_ref[s, pl.ds(r0, rc + span), :]
            for t in range(taps):
                a, ts = divmod(first + t, SUBLANES)
                if ts == s:
                    acc = acc + win[a * SUBLANES:a * SUBLANES + rc] * w_ref[t:t + 1, :]
        y = _layer_norm(acc, g_ref[...], b_ref[...])
        o_ref[pl.ds(r0, rc), :] = (y * jax.nn.sigmoid(y)).astype(o_ref.dtype)
        return carry

    lax.fori_loop(0, tq // rc, chunk, 0, unroll=2)


def _conv(proj, conv_w, conv_b, ln_g, ln_b, layer, *, batch, seq, offs):
    taps, c = conv_w.shape[1:]
    halo = 16
    assert taps // 2 <= halo and halo % SUBLANES == 0
    tq = min(512, seq)
    nt = seq // tq
    hb = tq // halo
    ccol = offs["cv"] // (2 * c)
    kern = functools.partial(_conv_kernel, halo=halo, taps=taps)
    return pl.pallas_call(
        kern,
        grid=(batch, nt),
        in_specs=[
            pl.BlockSpec((halo, 2 * c), lambda b, i: (jnp.maximum((b * nt + i) * hb - 1, 0), ccol)),
            pl.BlockSpec((tq, 2 * c), lambda b, i: (b * nt + i, ccol)),
            pl.BlockSpec((halo, 2 * c),
                         lambda b, i: (jnp.minimum((b * nt + i + 1) * hb, batch * nt * hb - 1), ccol)),
            _layer_spec((taps, c), layer),
            _layer_spec((1, c), layer),
            _layer_spec((1, c), layer),
            _layer_spec((1, c), layer),
        ],
        out_specs=pl.BlockSpec((tq, c), lambda b, i: (b * nt + i, 0)),
        out_shape=jax.ShapeDtypeStruct((batch * seq, c), BF16),
        scratch_shapes=[pltpu.VMEM((SUBLANES, tq + 2 * halo, c), F32)],
        compiler_params=_params(2),
        name="conv",
    )(proj, proj, proj, conv_w, conv_b, ln_g, ln_b)


def _out_ln_kernel(x_ref, a_ref, d_ref, c_ref, w_ref, g_ref, b_ref, o_ref, *, alpha, heads, hdv):
    hv, nd = a_ref.shape[1] // heads, d_ref.shape[1]
    na = heads * hdv
    tm = x_ref.shape[0]
    sub = tm // 2 if tm % (2 * BF16_ROWS) == 0 else tm
    for r0 in range(0, tm, sub):
        rows = slice(r0, r0 + sub)
        mix = jnp.dot(d_ref[rows, :], w_ref[na:na + nd, :], preferred_element_type=F32)
        mix += jnp.dot(c_ref[rows, :], w_ref[na + nd:, :], preferred_element_type=F32)
        for h in range(heads):
            mix += jnp.dot(a_ref[rows, h * hv:(h + 1) * hv], w_ref[h * hdv:h * hdv + hv, :],
                           preferred_element_type=F32)
        o_ref[rows, :] = _layer_norm(alpha * x_ref[rows, :] + mix, g_ref[...], b_ref[...])


def _out_ln(x, gla_o, dil_o, conv_o, w, g, b, layer, *, alpha, tm, heads, hdv):
    m, d = x.shape
    row = lambda i: (i, 0)
    assert (heads - 1) * hdv + gla_o.shape[1] // heads <= w.shape[1] and hdv % BF16_ROWS == 0
    return pl.pallas_call(
        functools.partial(_out_ln_kernel, alpha=alpha, heads=heads, hdv=hdv),
        grid=(m // tm,),
        in_specs=[
            pl.BlockSpec((tm, d), row),
            pl.BlockSpec((tm, gla_o.shape[1]), row),
            pl.BlockSpec((tm, dil_o.shape[1]), row),
            pl.BlockSpec((tm, conv_o.shape[1]), row),
            _layer_spec(w.shape[1:], layer),
            _layer_spec((1, d), layer),
            _layer_spec((1, d), layer),
        ],
        out_specs=pl.BlockSpec((tm, d), row),
        out_shape=jax.ShapeDtypeStruct((m, d), F32),
        compiler_params=_params(1),
        name="out_ln",
    )(x, gla_o, dil_o, conv_o, w, g, b)


def _pad_heads(w, heads, width):
    lead = w.shape[:-1]
    hd = w.shape[-1] // heads
    w = w.reshape(lead + (heads, hd))
    w = jnp.pad(w, [(0, 0)] * len(lead) + [(0, 0), (0, width - hd)])
    return w.reshape(lead + (heads * width,))


def kernel(x, ffn1_w_gate, ffn1_w_up, ffn1_w_down, ln1_g, ln1_b, w_in, gla_decay_w_fwd, gla_decay_b_fwd, gla_decay_w_bwd, gla_decay_b_bwd, gla_norm_g, conv_w, conv_b, conv_ln_g, conv_ln_b, w_out, ln2_g, ln2_b, ffn2_w_gate, ffn2_w_up, ffn2_w_down, ln3_g, ln3_b):
    batch, seq, d_model = x.shape
    depth = ffn1_w_gate.shape[0]
    rank, gla_dk = gla_decay_w_fwd.shape[1:]
    gla_dv = gla_norm_g.shape[1]
    conv_c = conv_w.shape[2]
    dil_dim = w_out.shape[1] - gla_dv - conv_c
    dil_hd = dil_dim // DIL_HEADS
    hdk, hdv = gla_dk // GLA_HEADS, gla_dv // GLA_HEADS
    hk, hv = _round_up(hdk, LANES), _round_up(hdv, LANES)
    assert dil_hd == LANES and 2 * rank <= LANES and seq % GLA_CHUNK == 0

    alpha = (2.0 * depth) ** 0.25
    m = batch * seq
    tm = min(512, m)

    def layout(widths):
        offs, total = {}, 0
        for name, wdt in widths:
            offs[name] = total
            total += wdt
        return offs

    offs_a = layout((("gq", GLA_HEADS * hk), ("gk", GLA_HEADS * hk), ("gv", GLA_HEADS * hv),
                     ("gg", GLA_HEADS * hv), ("r", LANES)))
    offs_b = layout((("cv", 2 * conv_c), ("dq", dil_dim), ("dk", dil_dim), ("dv", dil_dim)))

    src = layout((("gq", gla_dk), ("gk", gla_dk), ("gv", gla_dv), ("r", 2 * rank), ("gg", gla_dv),
                  ("dil", 3 * dil_dim), ("cv", 2 * conv_c)))
    runs = []
    for name, hd in (("gq", hdk), ("gk", hdk), ("gv", hdv), ("gg", hdv)):
        for h in range(GLA_HEADS):
            runs += [(src[name] + h * hd + c0, min(LANES, hd - c0)) for c0 in range(0, _round_up(hd, LANES), LANES)]
    runs.append((src["r"], 2 * rank))
    assert len(runs) * LANES == offs_a["r"] + LANES
    runs_b = [(src[name] + c0, LANES) for name, wdt in (("cv", 2 * conv_c), ("dil", 3 * dil_dim))
              for c0 in range(0, wdt, LANES)]
    w_in_a, w_in_b = _regroup(w_in, (runs, runs_b), cb=min(MXU_WIDTH, d_model))

    def head_mats(wdec, row0):
        w4 = _pad_heads(wdec, GLA_HEADS, hk).reshape(depth, rank, GLA_HEADS, hk).transpose(0, 2, 1, 3)
        return jnp.pad(w4, ((0, 0), (0, 0), (row0, LANES - rank - row0), (0, 0))).astype(BF16)

    wdec_f, wdec_b = head_mats(gla_decay_w_fwd, 0), head_mats(gla_decay_w_bwd, rank)
    bdec_f = _pad_heads(gla_decay_b_fwd, GLA_HEADS, hk).reshape(depth, GLA_HEADS, 1, hk)
    bdec_b = _pad_heads(gla_decay_b_bwd, GLA_HEADS, hk).reshape(depth, GLA_HEADS, 1, hk)
    norm_g = _pad_heads(gla_norm_g, GLA_HEADS, hv).reshape(depth, GLA_HEADS, 1, hv)

    w_out_b = w_out.astype(BF16)

    ffn_src = ((ffn1_w_gate, ffn1_w_up, ffn1_w_down), (ffn2_w_gate, ffn2_w_up, ffn2_w_down))
    ffn_w = tuple(w[0].astype(BF16) for w in ffn_src[0])
    vec = lambda p: p[:, None, :]
    slopes = jnp.asarray(2.0 ** (-8.0 * np.arange(1, DIL_HEADS + 1) / DIL_HEADS), F32)

    xf = x.reshape(m, d_model)
    for l in range(depth):
        xf, ffn_w = _ffn_ln(xf, *ffn_w, vec(ln1_g), vec(ln1_b), l, (ffn_src[1], l), alpha=alpha, tm=tm)
        proj_a = _in_proj(xf, (w_in_a,), l, BF16, tm=tm, name="in_proj_gla")
        proj_b = _in_proj(xf, (w_in_b,), l, F32, tm=tm, name="in_proj_mix")
        gla_o = _gla(proj_a, wdec_f, wdec_b, bdec_f, bdec_b, norm_g, l, batch=batch, seq=seq, offs=offs_a,
                     hk=hk, hv=hv, q_scale=hdk ** -0.5, dv_head=hdv)
        dil_o = _dil(proj_b, slopes, batch=batch, seq=seq, offs=offs_b, hd=dil_hd)
        conv_o = _conv(proj_b, conv_w, vec(conv_b), vec(conv_ln_g), vec(conv_ln_b), l, batch=batch, seq=seq,
                       offs=offs_b)
        xf = _out_ln(xf, gla_o, dil_o, conv_o, w_out_b, vec(ln2_g), vec(ln2_b), l, alpha=alpha, tm=tm,
                     heads=GLA_HEADS, hdv=hdv)
        nxt = (ffn_src[0], l + 1) if l + 1 < depth else None
        xf, ffn_w = _ffn_ln(xf, *ffn_w, vec(ln3_g), vec(ln3_b), l, nxt, alpha=alpha, tm=tm)
    return xf.reshape(batch, seq, d_model)
```
